```python
import math
import jax, jax.numpy as jnp
from jax import lax
import numpy as np

D_MODEL = 1024
BATCH = 4
SEQ = 8192
DEPTH = 1
DEC_BATCH = 16
DEC_SEQ = 32
PAST_LEN = 4096

CHUNK = 64
GDN_HEADS = 8
GDN_DK = 128
GDN_DV = 128
CONV_W = 4
ATT_HEADS = 8
ATT_KV_HEADS = 2
HEAD_DIM = 128
IDX_HEADS = 16
IDX_DIM = 64
TOPK_MAX = 256
Q_BLOCK = 128
ROPE_THETA = 500000.0
N_EXPERTS = 256
EXPERT_TOPK = 8
N_GROUPS = 8
TOPK_GROUPS = 4
EXPERT_DIM = 256
SHARED_DIM = 256
ROUTED_SCALE = 2.5
MOE_BLOCK = 128
DN_ALPHA = (2.0 * DEPTH) ** 0.25
DN_BETA = (8.0 * DEPTH) ** -0.25
LN_EPS = 1e-5
RMS_EPS = 1e-6

GDN_QK = GDN_HEADS * GDN_DK
GDN_V = GDN_HEADS * GDN_DV
CONV_CH = 2 * GDN_QK + GDN_V
ATT_Q = ATT_HEADS * HEAD_DIM
ATT_KV = ATT_KV_HEADS * HEAD_DIM
IDX_Q = IDX_HEADS * IDX_DIM
IN_SIZES = (CONV_CH, GDN_V, GDN_HEADS, GDN_HEADS, ATT_Q, ATT_KV, ATT_KV, IDX_Q, IDX_DIM, IDX_HEADS, D_MODEL, D_MODEL)
IN_COLS = sum(IN_SIZES)

kernel_name = "streaming_gdn_dsa_moe_step"


def layer_norm(x, g, b):
    xf = x.astype(jnp.float32)
    mu = jnp.mean(xf, -1, keepdims=True)
    var = jnp.mean(jnp.square(xf - mu), -1, keepdims=True)
    return ((xf - mu) * lax.rsqrt(var + LN_EPS) * g + b).astype(x.dtype)


def l2norm(x):
    xf = x.astype(jnp.float32)
    return xf * lax.rsqrt(jnp.sum(xf * xf, -1, keepdims=True) + 1e-6)


def partial_rope(x, pos):
    rot = x.shape[-1] // 4
    half = rot // 2
    inv_freq = ROPE_THETA ** (-(2.0 / rot) * jnp.arange(half, dtype=jnp.float32))
    ang = pos.astype(jnp.float32)[:, None] * inv_freq[None, :]
    cos = jnp.cos(ang)[None, :, None, :]
    sin = jnp.sin(ang)[None, :, None, :]
    xf = x.astype(jnp.float32)
    x1, x2, rest = xf[..., :half], xf[..., half:rot], xf[..., rot:]
    out = jnp.concatenate([x1 * cos - x2 * sin, x2 * cos + x1 * sin, rest], -1)
    return out.astype(x.dtype)


def causal_conv(x, buf, w):
    L = x.shape[1]
    xp = jnp.concatenate([buf.astype(x.dtype), x], 1)
    y = w[0] * xp[:, 0:L]
    for j in range(1, CONV_W):
        y = y + w[j] * xp[:, j:j + L]
    return jax.nn.silu(y), xp[:, xp.shape[1] - (CONV_W - 1):]


def gated_delta_rule(q, k, v, g, beta, S0):
    B, L, H, DK = q.shape
    DV = v.shape[-1]
    C = min(CHUNK, L)
    n = L // C

    def to_chunks(t):
        t = t.reshape((B, n, C, H) + t.shape[3:])
        return jnp.moveaxis(t, (1, 3), (0, 2))

    qc, kc, vc, bc = to_chunks(q), to_chunks(k), to_chunks(v), to_chunks(beta)
    gc = jnp.cumsum(to_chunks(g), -1)
    ii = jnp.arange(C)
    incl = ii[:, None] >= ii[None, :]
    strict = ii[:, None] > ii[None, :]
    decay = jnp.exp(jnp.where(incl, gc[..., :, None] - gc[..., None, :], -jnp.inf))
    kb = kc * bc[..., None]
    m = jnp.where(strict, jnp.einsum('nbhid,nbhjd->nbhij', kb, kc) * decay, 0.0)
    a = m + jnp.eye(C, dtype=m.dtype)
    rhs = jnp.concatenate([vc * bc[..., None], kb * jnp.exp(gc)[..., None]], -1)
    sol = lax.linalg.triangular_solve(a, rhs, left_side=True, lower=True, unit_diagonal=True)
    u, w = sol[..., :DV], sol[..., DV:]
    attn = jnp.einsum('nbhid,nbhjd->nbhij', qc, kc) * decay

    def step(S, xs):
        qi, ki, ui, wi, gi, ai = xs
        v_new = ui - jnp.einsum('bhck,bhkv->bhcv', wi, S)
        o = (jnp.einsum('bhck,bhkv->bhcv', qi * jnp.exp(gi)[..., None], S)
             + jnp.einsum('bhij,bhjv->bhiv', ai, v_new))
        glast = gi[..., -1]
        S = (S * jnp.exp(glast)[..., None, None]
             + jnp.einsum('bhck,bhcv->bhkv', ki * jnp.exp(glast[..., None] - gi)[..., None], v_new))
        return S, o

    S, o = lax.scan(step, S0, (qc, kc, u, w, gc, attn))
    o = jnp.moveaxis(o, (0, 2), (1, 3)).reshape(B, L, H, DV)
    return o, S


def gdn_branch(h_qkv, z, a_raw, b_raw, conv_buf, S0, conv_w, a_log, dt_bias, norm_g):
    B, L, _ = h_qkv.shape
    c, new_buf = causal_conv(h_qkv, conv_buf, conv_w)
    q, k, v = jnp.split(c, [GDN_QK, 2 * GDN_QK], -1)
    q = l2norm(q.reshape(B, L, GDN_HEADS, GDN_DK)) * (GDN_DK ** -0.5)
    k = l2norm(k.reshape(B, L, GDN_HEADS, GDN_DK))
    v = v.reshape(B, L, GDN_HEADS, GDN_DV).astype(jnp.float32)
    g = -jnp.exp(a_log.astype(jnp.float32)) * jax.nn.softplus(a_raw.astype(jnp.float32) + dt_bias.astype(jnp.float32))
    beta = jax.nn.sigmoid(b_raw.astype(jnp.float32))
    o, S = gated_delta_rule(q, k, v, g, beta, S0.astype(jnp.float32))
    o = o * lax.rsqrt(jnp.mean(o * o, -1, keepdims=True) + RMS_EPS) * norm_g.astype(jnp.float32)
    o = o * jax.nn.silu(z.astype(jnp.float32).reshape(B, L, GDN_HEADS, GDN_DV))
    return o.reshape(B, L, GDN_V).astype(h_qkv.dtype), new_buf, S.astype(S0.dtype)


def dsa_branch(q, k_all, v_all, qi, ki_all, wi, q_pos):
    B, T = q.shape[:2]
    L = k_all.shape[1]
    topk = min(TOPK_MAX, L // 4)
    qb = min(Q_BLOCK, T)
    nb = T // qb
    rep = ATT_HEADS // ATT_KV_HEADS
    limit = jnp.minimum((q_pos // CHUNK + 1) * CHUNK, L)
    key_pos = jnp.arange(L)

    def blocks(t):
        return jnp.moveaxis(t.reshape((B, nb, qb) + t.shape[2:]), 1, 0)

    def attend(xs):
        q_b, qi_b, wi_b, lim_b = xs
        admiss = key_pos[None, :] < lim_b[:, None]
        s = jnp.einsum('bqhd,bsd->bqhs', qi_b, ki_all).astype(jnp.float32)
        score = jnp.einsum('bqh,bqhs->bqs', wi_b.astype(jnp.float32), jax.nn.relu(s))
        score = jnp.where(admiss[None], score, -jnp.inf)
        _, idx = lax.top_k(score, topk)
        valid = idx < lim_b[None, :, None]
        kg = jax.vmap(lambda kk, ix: kk[ix])(k_all, idx)
        vg = jax.vmap(lambda vv, ix: vv[ix])(v_all, idx)
        qg = q_b.reshape(B, qb, ATT_KV_HEADS, rep, HEAD_DIM)
        logits = jnp.einsum('bqgrd,bqkgd->bqgrk', qg, kg).astype(jnp.float32) * (HEAD_DIM ** -0.5)
        logits = jnp.where(valid[:, :, None, None, :], logits, -jnp.inf)
        p = jax.nn.softmax(logits, -1).astype(vg.dtype)
        o = jnp.einsum('bqgrk,bqkgd->bqgrd', p, vg)
        return o.reshape(B, qb, ATT_Q)

    out = lax.map(attend, (blocks(q), blocks(qi), blocks(wi), limit.reshape(nb, qb)))
    return jnp.moveaxis(out, 0, 1).reshape(B, T, ATT_Q)


def swiglu(x, wa, wb, wc):
    return (jax.nn.silu(x @ wa) * (x @ wb)) @ wc


def routed_experts(xt, eidx, wts, w1, w3, w2):
    T, D = xt.shape
    N = T * EXPERT_TOPK
    e_flat = eidx.reshape(N)
    tok_flat = jnp.repeat(jnp.arange(T, dtype=jnp.int32), EXPERT_TOPK)
    g_flat = wts.reshape(N)
    order = jnp.argsort(e_flat)
    e_sorted = e_flat[order]
    counts = jnp.zeros((N_EXPERTS,), jnp.int32).at[e_flat].add(1)
    padded = (counts + MOE_BLOCK - 1) // MOE_BLOCK * MOE_BLOCK
    pad_end = jnp.cumsum(padded)
    pad_start = pad_end - padded
    start = jnp.cumsum(counts) - counts
    dest = pad_start[e_sorted] + jnp.arange(N, dtype=jnp.int32) - start[e_sorted]
    n_blocks = -(-N // MOE_BLOCK) + N_EXPERTS
    rows = n_blocks * MOE_BLOCK
    row_tok = jnp.full((rows,), T, jnp.int32).at[dest].set(tok_flat[order])
    row_gate = jnp.zeros((rows,), jnp.float32).at[dest].set(g_flat[order])
    blk_exp = jnp.minimum(jnp.searchsorted(pad_end, jnp.arange(n_blocks, dtype=jnp.int32) * MOE_BLOCK, side='right'), N_EXPERTS - 1)
    x_pad = jnp.concatenate([xt, jnp.zeros((1, D), xt.dtype)], 0)

    def expert_block(args):
        tok_b, e = args
        return swiglu(x_pad[tok_b], w1[e], w3[e], w2[e])

    yb = lax.map(expert_block, (row_tok.reshape(n_blocks, MOE_BLOCK), blk_exp)).reshape(rows, D)
    yb = yb * row_gate[:, None].astype(yb.dtype)
    return jax.ops.segment_sum(yb, row_tok, num_segments=T + 1)[:T]


def moe(x, w_router, router_bias, w1, w3, w2, ws1, ws3, ws2):
    B, L, D = x.shape
    xt = x.reshape(B * L, D)
    T = xt.shape[0]
    scores = jax.nn.sigmoid((xt @ w_router).astype(jnp.float32))
    biased = scores + router_bias.astype(jnp.float32)
    grp_score = lax.top_k(biased.reshape(T, N_GROUPS, N_EXPERTS // N_GROUPS), 2)[0].sum(-1)
    _, gidx = lax.top_k(grp_score, TOPK_GROUPS)
    gmask = jnp.any(gidx[..., None] == jnp.arange(N_GROUPS)[None, None, :], axis=1)
    masked = jnp.where(jnp.repeat(gmask, N_EXPERTS // N_GROUPS, axis=-1), biased, -jnp.inf)
    _, eidx = lax.top_k(masked, EXPERT_TOPK)
    wts = jnp.take_along_axis(scores, eidx, -1)
    wts = wts / jnp.sum(wts, -1, keepdims=True) * ROUTED_SCALE
    y = routed_experts(xt, eidx, wts, w1, w3, w2) + swiglu(xt, ws1, ws3, ws2)
    return y.reshape(B, L, D)


def trunk_layer(x, pos, conv_buf, S0, k_past, v_past, ik_past, prm):
    B, L, _ = x.shape
    h = x @ prm['w_in']
    split_at = np.cumsum(IN_SIZES)[:-1].tolist()
    h_qkv, z, a_raw, b_raw, q, k, v, qi, ki, wi, ga, gb = jnp.split(h, split_at, -1)
    y_a, new_buf, S = gdn_branch(h_qkv, z, a_raw, b_raw, conv_buf, S0, prm['conv_w'], prm['a_log'], prm['dt_bias'], prm['gdn_norm_g'])
    q = partial_rope(q.reshape(B, L, ATT_HEADS, HEAD_DIM), pos)
    k = partial_rope(k.reshape(B, L, ATT_KV_HEADS, HEAD_DIM), pos)
    v = v.reshape(B, L, ATT_KV_HEADS, HEAD_DIM)
    qi = partial_rope(qi.reshape(B, L, IDX_HEADS, IDX_DIM), pos)
    ki = partial_rope(layer_norm(ki, prm['idx_k_ln_g'], prm['idx_k_ln_b'])[:, :, None, :], pos)[:, :, 0, :]
    wi = wi * (IDX_HEADS ** -0.5 * IDX_DIM ** -0.5)
    if k_past is None:
        k_all, v_all, ki_all = k, v, ki
    else:
        k_all = jnp.concatenate([k_past.astype(k.dtype), k], 1)
        v_all = jnp.concatenate([v_past.astype(v.dtype), v], 1)
        ki_all = jnp.concatenate([ik_past.astype(ki.dtype), ki], 1)
    y_b = dsa_branch(q, k_all, v_all, qi, ki_all, wi, pos)
    merged = jax.nn.sigmoid(ga) * (y_a @ prm['w_o_gdn']) + jax.nn.sigmoid(gb) * (y_b @ prm['w_o_dsa'])
    x = layer_norm(DN_ALPHA * x + merged @ prm['w_out'], prm['ln1_g'], prm['ln1_b'])
    f = moe(x, prm['w_router'], prm['router_bias'], prm['w1'], prm['w3'], prm['w2'], prm['ws1'], prm['ws3'], prm['ws2'])
    x = layer_norm(DN_ALPHA * x + f, prm['ln2_g'], prm['ln2_b'])
    return x, (k, v, ki, S, new_buf)


def setup_inputs(seed: int = 0) -> dict:
    key = jax.random.key(seed)
    ks = jax.random.split(key, 32)
    f32 = jnp.float32
    Ld = DEPTH

    def nrm(k, shape, scale):
        return jax.random.normal(k, shape, f32) * scale

    dt = jnp.exp(jax.random.uniform(ks[10], (Ld, GDN_HEADS), f32, math.log(1e-3), math.log(1e-1)))
    return {
        "x_prompt": nrm(ks[0], (BATCH, SEQ, D_MODEL), 1.0),
        "x_sample": nrm(ks[1], (DEC_BATCH, DEC_SEQ, D_MODEL), 1.0),
        "cache_k": nrm(ks[2], (Ld, DEC_BATCH, PAST_LEN, ATT_KV_HEADS, HEAD_DIM), 1.0),
        "cache_v": nrm(ks[3], (Ld, DEC_BATCH, PAST_LEN, ATT_KV_HEADS, HEAD_DIM), 1.0),
        "cache_idx_k": nrm(ks[4], (Ld, DEC_BATCH, PAST_LEN, IDX_DIM), 1.0),
        "state_gdn": nrm(ks[5], (Ld, DEC_BATCH, GDN_HEADS, GDN_DK, GDN_DV), 0.1),
        "state_conv": nrm(ks[6], (Ld, DEC_BATCH, CONV_W - 1, CONV_CH), 1.0),
        "w_in": nrm(ks[7], (Ld, D_MODEL, IN_COLS), D_MODEL ** -0.5),
        "conv_w": nrm(ks[8], (Ld, CONV_W, CONV_CH), CONV_W ** -0.5),
        "a_log": jnp.log(jax.random.uniform(ks[9], (Ld, GDN_HEADS), f32, 1.0, 16.0)),
        "dt_bias": dt + jnp.log(-jnp.expm1(-dt)),
        "gdn_norm_g": 1.0 + nrm(ks[11], (Ld, GDN_DV), 0.02),
        "w_o_gdn": nrm(ks[12], (Ld, GDN_V, D_MODEL), GDN_V ** -0.5 * DN_BETA),
        "idx_k_ln_g": 1.0 + nrm(ks[13], (Ld, IDX_DIM), 0.02),
        "idx_k_ln_b": nrm(ks[14], (Ld, IDX_DIM), 0.02),
        "w_o_dsa": nrm(ks[15], (Ld, ATT_Q, D_MODEL), ATT_Q ** -0.5 * DN_BETA),
        "w_out": nrm(ks[16], (Ld, D_MODEL, D_MODEL), D_MODEL ** -0.5 * DN_BETA),
        "ln1_g": 1.0 + nrm(ks[17], (Ld, D_MODEL), 0.02),
        "ln1_b": nrm(ks[18], (Ld, D_MODEL), 0.02),
        "w_router": nrm(ks[19], (Ld, D_MODEL, N_EXPERTS), D_MODEL ** -0.5),
        "router_bias": nrm(ks[20], (Ld, N_EXPERTS), 0.01),
        "w1": nrm(ks[21], (Ld, N_EXPERTS, D_MODEL, EXPERT_DIM), D_MODEL ** -0.5),
        "w3": nrm(ks[22], (Ld, N_EXPERTS, D_MODEL, EXPERT_DIM), D_MODEL ** -0.5),
        "w2": nrm(ks[23], (Ld, N_EXPERTS, EXPERT_DIM, D_MODEL), EXPERT_DIM ** -0.5 * DN_BETA),
        "ws1": nrm(ks[24], (Ld, D_MODEL, SHARED_DIM), D_MODEL ** -0.5),
        "ws3": nrm(ks[25], (Ld, D_MODEL, SHARED_DIM), D_MODEL ** -0.5),
        "ws2": nrm(ks[26], (Ld, SHARED_DIM, D_MODEL), SHARED_DIM ** -0.5 * DN_BETA),
        "ln2_g": 1.0 + nrm(ks[27], (Ld, D_MODEL), 0.02),
        "ln2_b": nrm(ks[28], (Ld, D_MODEL), 0.02),
    }


def reference(x_prompt, x_sample, cache_k, cache_v, cache_idx_k, state_gdn, state_conv,
              w_in, conv_w, a_log, dt_bias, gdn_norm_g, w_o_gdn, idx_k_ln_g, idx_k_ln_b,
              w_o_dsa, w_out, ln1_g, ln1_b, w_router, router_bias, w1, w3, w2,
              ws1, ws3, ws2, ln2_g, ln2_b):
    B, S_len, _ = x_prompt.shape
    T_len = x_sample.shape[1]
    past = cache_k.shape[2]
    pos_p = jnp.arange(S_len, dtype=jnp.int32)
    pos_s = past + jnp.arange(T_len, dtype=jnp.int32)
    xp, xs = x_prompt, x_sample
    st_p, st_s = [], []
    for l in range(DEPTH):
        prm = dict(w_in=w_in[l], conv_w=conv_w[l], a_log=a_log[l], dt_bias=dt_bias[l], gdn_norm_g=gdn_norm_g[l],
                   w_o_gdn=w_o_gdn[l], idx_k_ln_g=idx_k_ln_g[l], idx_k_ln_b=idx_k_ln_b[l], w_o_dsa=w_o_dsa[l],
                   w_out=w_out[l], ln1_g=ln1_g[l], ln1_b=ln1_b[l], w_router=w_router[l], router_bias=router_bias[l],
                   w1=w1[l], w3=w3[l], w2=w2[l], ws1=ws1[l], ws3=ws3[l], ws2=ws2[l], ln2_g=ln2_g[l], ln2_b=ln2_b[l])
        conv0 = jnp.zeros((B, CONV_W - 1, CONV_CH), xp.dtype)
        s0 = jnp.zeros((B, GDN_HEADS, GDN_DK, GDN_DV), jnp.float32)
        xp, sp = trunk_layer(xp, pos_p, conv0, s0, None, None, None, prm)
        xs, ss = trunk_layer(xs, pos_s, state_conv[l], state_gdn[l], cache_k[l], cache_v[l], cache_idx_k[l], prm)
        st_p.append(sp)
        st_s.append(ss)
    new_k_p = jnp.stack([s[0] for s in st_p], 0)
    new_v_p = jnp.stack([s[1] for s in st_p], 0)
    new_idx_k_p = jnp.stack([s[2] for s in st_p], 0)
    new_gdn_p = jnp.stack([s[3] for s in st_p], 0)
    new_conv_p = jnp.stack([s[4] for s in st_p], 0)
    new_k_s = jnp.stack([s[0] for s in st_s], 0)
    new_v_s = jnp.stack([s[1] for s in st_s], 0)
    new_idx_k_s = jnp.stack([s[2] for s in st_s], 0)
    new_gdn_s = jnp.stack([s[3] for s in st_s], 0)
    new_conv_s = jnp.stack([s[4] for s in st_s], 0)
    return (xp, xs, new_k_p, new_v_p, new_idx_k_p, new_gdn_p, new_conv_p,
            new_k_s, new_v_s, new_idx_k_s, new_gdn_s, new_conv_s)
```

```python
import functools
import math

import jax
import jax.numpy as jnp
import numpy as np
from jax import lax
from jax.experimental import pallas as pl
from jax.experimental.pallas import tpu as pltpu

F32 = jnp.float32
BF16 = jnp.bfloat16

D_MODEL = 1024
CHUNK = 64
GDN_HEADS = 8
GDN_DK = 128
GDN_DV = 128
CONV_W = 4
ATT_HEADS = 8
ATT_KV_HEADS = 2
HEAD_DIM = 128
IDX_HEADS = 16
IDX_DIM = 64
TOPK_MAX = 256
ROPE_THETA = 500000.0
N_EXPERTS = 256
EXPERT_TOPK = 8
N_GROUPS = 8
TOPK_GROUPS = 4
EXPERT_DIM = 256
SHARED_DIM = 256
ROUTED_SCALE = 2.5
MOE_BLOCK = 128
DEPTH = 1
DN_ALPHA = (2.0 * DEPTH) ** 0.25
LN_EPS = 1e-5
RMS_EPS = 1e-6

GDN_QK = GDN_HEADS * GDN_DK
GDN_V = GDN_HEADS * GDN_DV
CONV_CH = 2 * GDN_QK + GDN_V
ATT_Q = ATT_HEADS * HEAD_DIM
ATT_KV = ATT_KV_HEADS * HEAD_DIM
IDX_Q = IDX_HEADS * IDX_DIM

LANES = 128
SUBLANES = 8
VMEM_LIMIT = 56 * 1024 * 1024

COL_QKV = 0
COL_Z = COL_QKV + CONV_CH
COL_Q = COL_Z + GDN_V
COL_QI = COL_Q + ATT_Q
COL_GA = COL_QI + IDX_Q
COL_GB = COL_GA + D_MODEL
COL_K = COL_GB + D_MODEL
COL_V = COL_K + ATT_KV
COL_SMALL = COL_V + ATT_KV
SMALL_W = 512
SM_KI = 0
SM_A = IDX_DIM
SM_B = SM_A + GDN_HEADS
SM_WI = SM_B + GDN_HEADS
H_COLS = COL_SMALL + SMALL_W


def _cparams(sem):
    return pltpu.CompilerParams(dimension_semantics=sem, vmem_limit_bytes=VMEM_LIMIT)


def _dot(a, b):
    return jnp.dot(a, b, preferred_element_type=F32)


def _dot_bf(a, b):
    return jnp.dot(a.astype(BF16), b.astype(BF16), preferred_element_type=F32)


def _dot_hi(a, b):
    return jnp.dot(a, b, precision=lax.Precision.HIGHEST, preferred_element_type=F32)


def _dot_nt_hi(a, b):
    return lax.dot_general(a, b, (((1,), (1,)), ((), ())), precision=lax.Precision.HIGHEST,
                           preferred_element_type=F32)


def _dot_nt_bf(a, b):
    return lax.dot_general(a.astype(BF16), b.astype(BF16), (((1,), (1,)), ((), ())),
                           preferred_element_type=F32)


def _dot_tn_bf(a, b):
    return lax.dot_general(a.astype(BF16), b.astype(BF16), (((0,), (0,)), ((), ())),
                           preferred_element_type=F32)


def _sigmoid(x):
    return 1.0 / (1.0 + jnp.exp(-x))


def _silu(x):
    return x * _sigmoid(x)


def _repack_w_in(w_in):
    sizes = (CONV_CH, GDN_V, GDN_HEADS, GDN_HEADS, ATT_Q, ATT_KV, ATT_KV, IDX_Q, IDX_DIM, IDX_HEADS,
             D_MODEL, D_MODEL)
    offs = np.concatenate([[0], np.cumsum(sizes)])
    (p_qkv, p_z, p_a, p_b, p_q, p_k, p_v, p_qi, p_ki, p_wi, p_ga, p_gb) = [
        w_in[:, offs[i]:offs[i + 1]] for i in range(len(sizes))]
    pad = jnp.zeros((w_in.shape[0], SMALL_W - (SM_WI + IDX_HEADS)), w_in.dtype)
    return jnp.concatenate([p_qkv, p_z, p_q, p_qi, p_ga, p_gb, p_k, p_v, p_ki, p_a, p_b, p_wi, pad], axis=1)


def _proj_kernel(x_ref, w_ref, o_ref):
    o_ref[...] = _dot(x_ref[...].astype(BF16), w_ref[...])


def _project(x2d, w_bf, tm, tn):
    t, d = x2d.shape
    n = w_bf.shape[1]
    return pl.pallas_call(
        _proj_kernel,
        grid=(t // tm, n // tn),
        in_specs=[pl.BlockSpec((tm, d), lambda i, j: (i, 0)),
                  pl.BlockSpec((d, tn), lambda i, j: (0, j))],
        out_specs=pl.BlockSpec((tm, tn), lambda i, j: (i, j)),
        out_shape=jax.ShapeDtypeStruct((t, n), F32),
        compiler_params=_cparams(("parallel", "arbitrary")),
        name="in_proj",
    )(x2d, w_bf)


TAIL_ROWS = SUBLANES
INV_BASE = 8


def _unit_lower_inverse(m, c):
    ri = lax.broadcasted_iota(jnp.int32, (c, c), 0)
    ci = lax.broadcasted_iota(jnp.int32, (c, c), 1)
    eye = (ri == ci).astype(F32)
    blk = INV_BASE
    same = (ri // blk) == (ci // blk)
    n = jnp.where(same, -m, 0.0)
    x = eye + n
    span = 1
    while span * 2 < blk:
        n = _dot_hi(n, n)
        x = x + _dot_hi(x, n)
        span *= 2
    while blk < c:
        nxt = blk * 2
        e = jnp.where(((ri // nxt) == (ci // nxt)) & ((ri // blk) != (ci // blk)), m, 0.0)
        x = x - _dot_hi(_dot_hi(x, e), x)
        blk = nxt
    return x


def _gdn_kernel(hq_ref, z_ref, sm_ref, buf_ref, s0_ref, cw_ref, alog_ref, dtb_ref, ng_ref,
                y_ref, snew_ref, bufnew_ref, xp_ref, s_ref, *, chunk):
    c_idx = pl.program_id(1)
    n_c = pl.num_programs(1)
    C = chunk
    keep = CONV_W - 1

    @pl.when(c_idx == 0)
    def _():
        xp_ref[TAIL_ROWS - keep:TAIL_ROWS, :] = buf_ref[...]
        s_ref[...] = s0_ref[...]

    xp_ref[TAIL_ROWS:TAIL_ROWS + C, :] = hq_ref[...]

    acc = cw_ref[0:1, :] * xp_ref[TAIL_ROWS - keep:TAIL_ROWS - keep + C, :]
    for j in range(1, CONV_W):
        acc = acc + cw_ref[j:j + 1, :] * xp_ref[TAIL_ROWS - keep + j:TAIL_ROWS - keep + j + C, :]
    conv = _silu(acc)

    @pl.when(c_idx == n_c - 1)
    def _():
        bufnew_ref[...] = xp_ref[TAIL_ROWS + C - keep:TAIL_ROWS + C, :]

    xp_ref[TAIL_ROWS - keep:TAIL_ROWS, :] = xp_ref[TAIL_ROWS + C - keep:TAIL_ROWS + C, :]

    sm = sm_ref[...]
    a_raw = sm[:, SM_A:SM_A + GDN_HEADS]
    b_raw = sm[:, SM_B:SM_B + GDN_HEADS]
    xg = a_raw + dtb_ref[...]
    softplus = jnp.maximum(xg, 0.0) + jnp.log(1.0 + jnp.exp(-jnp.abs(xg)))
    g = -jnp.exp(alog_ref[...]) * softplus
    beta = _sigmoid(b_raw)

    ri = lax.broadcasted_iota(jnp.int32, (C, C), 0)
    ci = lax.broadcasted_iota(jnp.int32, (C, C), 1)
    incl = ri >= ci
    strict = ri > ci
    gc = _dot_hi(incl.astype(F32), g)
    lane0 = (lax.broadcasted_iota(jnp.int32, (C, LANES), 1) == 0).astype(F32)

    for h in range(GDN_HEADS):
        q = conv[:, h * GDN_DK:(h + 1) * GDN_DK]
        k = conv[:, GDN_QK + h * GDN_DK:GDN_QK + (h + 1) * GDN_DK]
        v = conv[:, 2 * GDN_QK + h * GDN_DV:2 * GDN_QK + (h + 1) * GDN_DV]
        q = q * lax.rsqrt(jnp.sum(q * q, -1, keepdims=True) + 1e-6) * (GDN_DK ** -0.5)
        k = k * lax.rsqrt(jnp.sum(k * k, -1, keepdims=True) + 1e-6)
        gch = gc[:, h:h + 1]
        bh = beta[:, h:h + 1]
        gc_row = _dot_nt_hi(lane0, jnp.broadcast_to(gch, (C, LANES)))
        decay = jnp.exp(jnp.where(incl, gch - gc_row, -jnp.inf))
        kb = k * bh
        m = jnp.where(strict, _dot_nt_bf(kb, k) * decay, 0.0)
        tinv = _unit_lower_inverse(m, C)
        eg = jnp.exp(gch)
        rhs = jnp.concatenate([v * bh, kb * eg], axis=-1)
        sol = _dot_bf(tinv, rhs)
        u = sol[:, :GDN_DV]
        w = sol[:, GDN_DV:]
        attn = _dot_nt_bf(q, k) * decay
        s = s_ref[h]
        v_new = u - _dot_bf(w, s)
        o = _dot_bf(q * eg, s) + _dot_bf(attn, v_new)
        glast = gch[C - 1:C, :]
        s_ref[h] = s * jnp.exp(glast) + _dot_tn_bf(k * jnp.exp(glast - gch), v_new)
        o = o * lax.rsqrt(jnp.mean(o * o, -1, keepdims=True) + RMS_EPS) * ng_ref[...]
        zh = z_ref[:, h * GDN_DV:(h + 1) * GDN_DV]
        y_ref[:, h * GDN_DV:(h + 1) * GDN_DV] = o * _silu(zh)

    @pl.when(c_idx == n_c - 1)
    def _():
        snew_ref[...] = s_ref[...]


def _gdn(h3, conv_buf, s0, conv_w, a_log, dt_bias, norm_g):
    b, l, _ = h3.shape
    chunk = min(CHUNK, l)
    assert l % chunk == 0 and chunk % SUBLANES == 0 and chunk >= CONV_W - 1
    kern = functools.partial(_gdn_kernel, chunk=chunk)
    keep = CONV_W - 1
    return pl.pallas_call(
        kern,
        grid=(b, l // chunk),
        in_specs=[
            pl.BlockSpec((None, chunk, CONV_CH), lambda i, c: (i, c, COL_QKV // CONV_CH)),
            pl.BlockSpec((None, chunk, GDN_V), lambda i, c: (i, c, COL_Z // GDN_V)),
            pl.BlockSpec((None, chunk, SMALL_W), lambda i, c: (i, c, COL_SMALL // SMALL_W)),
            pl.BlockSpec((None, keep, CONV_CH), lambda i, c: (i, 0, 0)),
            pl.BlockSpec((None, GDN_HEADS, GDN_DK, GDN_DV), lambda i, c: (i, 0, 0, 0)),
            pl.BlockSpec((CONV_W, CONV_CH), lambda i, c: (0, 0)),
            pl.BlockSpec((1, GDN_HEADS), lambda i, c: (0, 0)),
            pl.BlockSpec((1, GDN_HEADS), lambda i, c: (0, 0)),
            pl.BlockSpec((1, GDN_DV), lambda i, c: (0, 0)),
        ],
        out_specs=[
            pl.BlockSpec((None, chunk, GDN_V), lambda i, c: (i, c, 0)),
            pl.BlockSpec((None, GDN_HEADS, GDN_DK, GDN_DV), lambda i, c: (i, 0, 0, 0)),
            pl.BlockSpec((None, keep, CONV_CH), lambda i, c: (i, 0, 0)),
        ],
        out_shape=[
            jax.ShapeDtypeStruct((b, l, GDN_V), F32),
            jax.ShapeDtypeStruct((b, GDN_HEADS, GDN_DK, GDN_DV), F32),
            jax.ShapeDtypeStruct((b, keep, CONV_CH), F32),
        ],
        scratch_shapes=[
            pltpu.VMEM((TAIL_ROWS + chunk, CONV_CH), F32),
            pltpu.VMEM((GDN_HEADS, GDN_DK, GDN_DV), F32),
        ],
        compiler_params=_cparams(("parallel", "arbitrary")),
        name="gdn",
    )(h3, h3, h3, conv_buf, s0, conv_w, a_log.reshape(1, GDN_HEADS), dt_bias.reshape(1, GDN_HEADS),
      norm_g.reshape(1, GDN_DV))


def _rope_tables(pos, rot, period):
    half = rot // 2
    inv_freq = ROPE_THETA ** (-(2.0 / rot) * jnp.arange(half, dtype=F32))
    ang = pos.astype(F32)[:, None] * inv_freq[None, :]
    cos, sin = jnp.cos(ang), jnp.sin(ang)
    n = pos.shape[0]
    rest = period - rot
    c = jnp.concatenate([cos, cos, jnp.ones((n, rest), F32)], -1)
    sa = jnp.concatenate([-sin, jnp.zeros((n, half + rest), F32)], -1)
    sb = jnp.concatenate([jnp.zeros((n, half), F32), sin, jnp.zeros((n, rest), F32)], -1)
    reps = LANES // period
    return jnp.stack([jnp.tile(c, (1, reps)), jnp.tile(sa, (1, reps)), jnp.tile(sb, (1, reps))], 0)


def _rope128(x, tab_ref, half):
    return (x * tab_ref[0] + pltpu.roll(x, LANES - half, 1) * tab_ref[1]
            + pltpu.roll(x, half, 1) * tab_ref[2])


def _prep_kernel(q_ref, qi_ref, k_ref, sm_ref, tq_ref, ti_ref, lng_ref, lnb_ref,
                 qo_ref, qio_ref, ko_ref, kio_ref, wio_ref, kto_ref, kito_ref):
    hq = HEAD_DIM // 8
    hi = IDX_DIM // 8
    for h in range(ATT_HEADS):
        x = q_ref[:, h * HEAD_DIM:(h + 1) * HEAD_DIM]
        qo_ref[:, h * HEAD_DIM:(h + 1) * HEAD_DIM] = (_rope128(x, tq_ref, hq) * (HEAD_DIM ** -0.5)).astype(BF16)
    for h in range(ATT_KV_HEADS):
        x = _rope128(k_ref[:, h * HEAD_DIM:(h + 1) * HEAD_DIM], tq_ref, hq)
        ko_ref[:, h * HEAD_DIM:(h + 1) * HEAD_DIM] = x
        kto_ref[h * HEAD_DIM:(h + 1) * HEAD_DIM, :] = x.T.astype(BF16)
    for c in range(IDX_Q // LANES):
        x = qi_ref[:, c * LANES:(c + 1) * LANES]
        qio_ref[:, c * LANES:(c + 1) * LANES] = _rope128(x, ti_ref, hi).astype(BF16)
    sm = sm_ref[:, 0:LANES]
    lane = lax.broadcasted_iota(jnp.int32, sm.shape, 1)
    is_ki = lane < IDX_DIM
    mu = jnp.sum(jnp.where(is_ki, sm, 0.0), -1, keepdims=True) * (1.0 / IDX_DIM)
    xc = jnp.where(is_ki, sm - mu, 0.0)
    var = jnp.sum(xc * xc, -1, keepdims=True) * (1.0 / IDX_DIM)
    ki = xc * lax.rsqrt(var + LN_EPS) * lng_ref[...] + lnb_ref[...]
    ki = _rope128(ki, ti_ref, hi)
    kio_ref[...] = ki[:, 0:IDX_DIM]
    kito_ref[...] = ki.T[0:IDX_DIM, :].astype(BF16)
    wio_ref[...] = sm[:, SM_WI:SM_WI + IDX_HEADS] * (IDX_HEADS ** -0.5 * IDX_DIM ** -0.5)


def _prep(h3, pos, ln_g, ln_b, tm):
    b, lp, _ = h3.shape
    tab_q = _rope_tables(pos, HEAD_DIM // 4, HEAD_DIM)
    tab_i = _rope_tables(pos, IDX_DIM // 4, IDX_DIM)
    lng = jnp.concatenate([ln_g, jnp.zeros((LANES - IDX_DIM,), F32)]).reshape(1, LANES)
    lnb = jnp.concatenate([ln_b, jnp.zeros((LANES - IDX_DIM,), F32)]).reshape(1, LANES)
    return pl.pallas_call(
        _prep_kernel,
        grid=(b, lp // tm),
        in_specs=[
            pl.BlockSpec((None, tm, ATT_Q), lambda i, t: (i, t, COL_Q // ATT_Q)),
            pl.BlockSpec((None, tm, IDX_Q), lambda i, t: (i, t, COL_QI // IDX_Q)),
            pl.BlockSpec((None, tm, ATT_KV), lambda i, t: (i, t, COL_K // ATT_KV)),
            pl.BlockSpec((None, tm, SMALL_W), lambda i, t: (i, t, COL_SMALL // SMALL_W)),
            pl.BlockSpec((3, tm, LANES), lambda i, t: (0, t, 0)),
            pl.BlockSpec((3, tm, LANES), lambda i, t: (0, t, 0)),
            pl.BlockSpec((1, LANES), lambda i, t: (0, 0)),
            pl.BlockSpec((1, LANES), lambda i, t: (0, 0)),
        ],
        out_specs=[
            pl.BlockSpec((None, tm, ATT_Q), lambda i, t: (i, t, 0)),
            pl.BlockSpec((None, tm, IDX_Q), lambda i, t: (i, t, 0)),
            pl.BlockSpec((None, tm, ATT_KV), lambda i, t: (i, t, 0)),
            pl.BlockSpec((None, tm, IDX_DIM), lambda i, t: (i, t, 0)),
            pl.BlockSpec((None, tm, IDX_HEADS), lambda i, t: (i, t, 0)),
            pl.BlockSpec((None, ATT_KV, tm), lambda i, t: (i, 0, t)),
            pl.BlockSpec((None, IDX_DIM, tm), lambda i, t: (i, 0, t)),
        ],
        out_shape=[
            jax.ShapeDtypeStruct((b, lp, ATT_Q), BF16),
            jax.ShapeDtypeStruct((b, lp, IDX_Q), BF16),
            jax.ShapeDtypeStruct((b, lp, ATT_KV), F32),
            jax.ShapeDtypeStruct((b, lp, IDX_DIM), F32),
            jax.ShapeDtypeStruct((b, lp, IDX_HEADS), F32),
            jax.ShapeDtypeStruct((b, ATT_KV, lp), BF16),
            jax.ShapeDtypeStruct((b, IDX_DIM, lp), BF16),
        ],
        compiler_params=_cparams(("parallel", "parallel")),
        name="dsa_prep",
    )(h3, h3, h3, h3, tab_q, tab_i, lng, lnb)


INT_MIN = -2 ** 31
INT_MAX = 2 ** 31 - 1
NEG_INF_KEY = INT_MIN + 0x7FFFFF
NEG_BIG = -1e30
REP = ATT_HEADS // ATT_KV_HEADS


def _dsa_kernel(q_ref, qi_ref, wi_ref, kit_ref, kt_ref, v_ref, o_ref,
                key_ref, qs_ref, m_ref, l_ref, acc_ref, cm_ref,
                *, tq, kb, l_true, pos0, topk, idx_bits):
    q0 = pos0 + pl.program_id(1) * tq
    rowpos = q0 + lax.broadcasted_iota(jnp.int32, (tq, 1), 0)
    lim = jnp.minimum((rowpos // CHUNK + 1) * CHUNK, l_true)
    lim_max = jnp.minimum(((q0 + tq - 1) // CHUNK + 1) * CHUNK, l_true)
    nkb = (lim_max + kb - 1) // kb
    lane_k = lax.broadcasted_iota(jnp.int32, (1, kb), 1)
    wi = wi_ref[...]

    def score_blk(j, carry):
        off = pl.multiple_of(j * kb, kb)
        kit = kit_ref[:, pl.ds(off, kb)]
        acc = jnp.zeros((tq, kb), F32)
        for h in range(IDX_HEADS):
            s = _dot(qi_ref[:, h * IDX_DIM:(h + 1) * IDX_DIM], kit)
            acc = acc + wi[:, h:h + 1] * jnp.maximum(s, 0.0)
        score = jnp.where(off + lane_k < lim, acc, -jnp.inf)
        bits = pltpu.bitcast(score, jnp.int32)
        key_ref[:, pl.ds(off, kb)] = bits ^ ((bits >> 31) & INT_MAX)
        return carry

    lax.fori_loop(0, nkb, score_blk, 0)

    def count(pred_fn):
        def blk(j, c):
            off = pl.multiple_of(j * kb, kb)
            part = jnp.where(pred_fn(key_ref[:, pl.ds(off, kb)], off), 1.0, 0.0)
            for s in range(kb // LANES):
                c = c + part[:, s * LANES:(s + 1) * LANES]
            return c
        c = lax.fori_loop(0, nkb, blk, jnp.zeros((tq, LANES), F32))
        return jnp.sum(c, -1, keepdims=True)

    def bit_step(t, cur):
        cand_u = cur | lax.shift_left(jnp.int32(1), 31 - t)
        cand_s = cand_u ^ INT_MIN
        cnt = count(lambda kk, off: kk >= cand_s)
        return jnp.where(cnt >= topk, cand_u, cur)

    thr = lax.fori_loop(0, 32, bit_step, jnp.zeros((tq, 1), jnp.int32)) ^ INT_MIN
    n_gt = count(lambda kk, off: kk > thr)
    n_ge = count(lambda kk, off: kk >= thr)
    need = topk - n_gt

    cm_ref[...] = jnp.full((tq, 1), INT_MAX, jnp.int32)

    @pl.when(jnp.max(jnp.where((n_ge > topk) & (thr > NEG_INF_KEY), 1, 0)) > 0)
    def _():
        def idx_step(t, cm):
            cand = cm | lax.shift_left(jnp.int32(1), idx_bits - 1 - t)
            before = count(lambda kk, off: (kk == thr) & (off + lane_k < cand))
            return jnp.where(before < need, cand, cm)
        cm_ref[...] = lax.fori_loop(0, idx_bits, idx_step, jnp.zeros((tq, 1), jnp.int32))

    cm = cm_ref[...]

    for g in range(ATT_KV_HEADS):
        for r in range(REP):
            hh = g * REP + r
            qs_ref[g, r * tq:(r + 1) * tq, :] = q_ref[:, hh * HEAD_DIM:(hh + 1) * HEAD_DIM]
    m_ref[...] = jnp.full(m_ref.shape, NEG_BIG, F32)
    l_ref[...] = jnp.zeros(l_ref.shape, F32)
    acc_ref[...] = jnp.zeros(acc_ref.shape, F32)

    def attn_blk(j, carry):
        off = pl.multiple_of(j * kb, kb)
        kk = key_ref[:, pl.ds(off, kb)]
        kpos = off + lane_k
        sel = ((kk > thr) | ((kk == thr) & (kpos <= cm))) & (kpos < lim)
        bias = jnp.where(sel, 0.0, NEG_BIG)
        for g in range(ATT_KV_HEADS):
            logits = _dot(qs_ref[g], kt_ref[g * HEAD_DIM:(g + 1) * HEAD_DIM, pl.ds(off, kb)])
            logits = (logits.reshape(REP, tq, kb) + bias[None]).reshape(REP * tq, kb)
            m_old = m_ref[g]
            m_new = jnp.maximum(m_old, jnp.max(logits, -1, keepdims=True))
            alpha = jnp.exp(m_old - m_new)
            p = jnp.exp(logits - m_new)
            l_ref[g] = alpha * l_ref[g] + jnp.sum(p, -1, keepdims=True)
            acc_ref[g] = alpha * acc_ref[g] + _dot(p.astype(BF16), v_ref[pl.ds(off, kb), g * HEAD_DIM:(g + 1) * HEAD_DIM])
            m_ref[g] = m_new
        return carry

    lax.fori_loop(0, nkb, attn_blk, 0)

    for g in range(ATT_KV_HEADS):
        o = acc_ref[g] / l_ref[g]
        for r in range(REP):
            hh = g * REP + r
            o_ref[:, hh * HEAD_DIM:(hh + 1) * HEAD_DIM] = o[r * tq:(r + 1) * tq, :]


def _dsa(q, qi, wi, kit, kt, v, tq, kb, l_true, pos0):
    b, t, _ = q.shape
    lk = kt.shape[2]
    topk = min(TOPK_MAX, l_true // 4)
    kern = functools.partial(_dsa_kernel, tq=tq, kb=kb, l_true=l_true, pos0=pos0, topk=topk,
                             idx_bits=int(lk).bit_length())
    return pl.pallas_call(
        kern,
        grid=(b, t // tq),
        in_specs=[
            pl.BlockSpec((None, tq, ATT_Q), lambda i, t_: (i, t_, 0)),
            pl.BlockSpec((None, tq, IDX_Q), lambda i, t_: (i, t_, 0)),
            pl.BlockSpec((None, tq, IDX_HEADS), lambda i, t_: (i, t_, 0)),
            pl.BlockSpec((None, IDX_DIM, lk), lambda i, t_: (i, 0, 0)),
            pl.BlockSpec((None, ATT_KV, lk), lambda i, t_: (i, 0, 0)),
            pl.BlockSpec((None, lk, ATT_KV), lambda i, t_: (i, 0, 0)),
        ],
        out_specs=pl.BlockSpec((None, tq, ATT_Q), lambda i, t_: (i, t_, 0)),
        out_shape=jax.ShapeDtypeStruct((b, t, ATT_Q), F32),
        scratch_shapes=[
            pltpu.VMEM((tq, lk), jnp.int32),
            pltpu.VMEM((ATT_KV_HEADS, REP * tq, HEAD_DIM), BF16),
            pltpu.VMEM((ATT_KV_HEADS, REP * tq, 1), F32),
            pltpu.VMEM((ATT_KV_HEADS, REP * tq, 1), F32),
            pltpu.VMEM((ATT_KV_HEADS, REP * tq, HEAD_DIM), F32),
            pltpu.VMEM((tq, 1), jnp.int32),
        ],
        compiler_params=_cparams(("parallel", "arbitrary")),
        name="dsa",
    )(q, qi, wi, kit, kt, v)


def _layer_norm(x, g, b):
    mu = jnp.mean(x, -1, keepdims=True)
    xc = x - mu
    var = jnp.mean(xc * xc, -1, keepdims=True)
    return xc * lax.rsqrt(var + LN_EPS) * g + b


def _merge_kernel(ya_ref, yb_ref, ga_ref, gb_ref, x_ref, wa_ref, wb_ref, wo_ref, g_ref, b_ref, o_ref):
    pa = _dot(ya_ref[...].astype(BF16), wa_ref[...])
    pb = _dot(yb_ref[...].astype(BF16), wb_ref[...])
    merged = _sigmoid(ga_ref[...]) * pa + _sigmoid(gb_ref[...]) * pb
    y = DN_ALPHA * x_ref[...] + _dot(merged.astype(BF16), wo_ref[...])
    o_ref[...] = _layer_norm(y, g_ref[...], b_ref[...])


def _merge(ya, yb, h2, x2, wa, wb, wo, g, b, tm):
    t = x2.shape[0]
    row = lambda c: pl.BlockSpec((tm, D_MODEL), lambda i: (i, c))
    full = lambda shp: pl.BlockSpec(shp, lambda i: (0, 0))
    return pl.pallas_call(
        _merge_kernel,
        grid=(t // tm,),
        in_specs=[row(0), row(0), row(COL_GA // D_MODEL), row(COL_GB // D_MODEL), row(0),
                  full((GDN_V, D_MODEL)), full((ATT_Q, D_MODEL)), full((D_MODEL, D_MODEL)),
                  full((1, D_MODEL)), full((1, D_MODEL))],
        out_specs=row(0),
        out_shape=jax.ShapeDtypeStruct((t, D_MODEL), F32),
        compiler_params=_cparams(("parallel",)),
        name="merge_ln1",
    )(ya, yb, h2, h2, x2, wa, wb, wo, g.reshape(1, D_MODEL), b.reshape(1, D_MODEL))


GROUP_SIZE = N_EXPERTS // N_GROUPS


def _first_max(cur, rows, n_rows):
    m = jnp.max(cur, axis=0, keepdims=True)
    idx = jnp.min(jnp.where(cur == m, rows, n_rows), axis=0, keepdims=True)
    return m, idx


def _router_kernel(x_ref, wr_ref, rb_ref, ws1_ref, ws3_ref, ws2_ref,
                   eidx_ref, wts_ref, base_ref, xb_ref):
    x = x_ref[...]
    tm = x.shape[0]
    logits = _dot_nt_hi(wr_ref[...], x)
    scores = _sigmoid(logits)
    biased = scores + rb_ref[...]
    neg = -jnp.inf

    rows_g = lax.broadcasted_iota(jnp.int32, (GROUP_SIZE, tm), 0)
    gs = []
    for g in range(N_GROUPS):
        blk = biased[g * GROUP_SIZE:(g + 1) * GROUP_SIZE, :]
        m1, i1 = _first_max(blk, rows_g, GROUP_SIZE)
        m2 = jnp.max(jnp.where(rows_g == i1, neg, blk), axis=0, keepdims=True)
        gs.append(m1 + m2)
    cur = jnp.concatenate(gs, axis=0)
    rows_n = lax.broadcasted_iota(jnp.int32, (N_GROUPS, tm), 0)
    gsel = jnp.zeros((N_GROUPS, tm), F32)
    for _ in range(TOPK_GROUPS):
        _, ig = _first_max(cur, rows_n, N_GROUPS)
        hit = rows_n == ig
        gsel = jnp.where(hit, 1.0, gsel)
        cur = jnp.where(hit, neg, cur)
    gexp = jnp.concatenate([jnp.broadcast_to(gsel[g:g + 1, :], (GROUP_SIZE, tm)) for g in range(N_GROUPS)], axis=0)
    cur = jnp.where(gexp > 0.0, biased, neg)

    rows_e = lax.broadcasted_iota(jnp.int32, (N_EXPERTS, tm), 0)
    es, ws = [], []
    for _ in range(EXPERT_TOPK):
        _, ie = _first_max(cur, rows_e, N_EXPERTS)
        hit = rows_e == ie
        es.append(ie)
        ws.append(jnp.sum(jnp.where(hit, scores, 0.0), axis=0, keepdims=True))
        cur = jnp.where(hit, neg, cur)
    w = jnp.concatenate(ws, axis=0)
    eidx_ref[...] = jnp.concatenate(es, axis=0)
    wts_ref[...] = w / jnp.sum(w, axis=0, keepdims=True) * ROUTED_SCALE

    xb = x.astype(BF16)
    xb_ref[...] = xb
    hs = _silu(_dot(xb, ws1_ref[...])) * _dot(xb, ws3_ref[...])
    base_ref[...] = DN_ALPHA * x + _dot(hs.astype(BF16), ws2_ref[...])


def _router(x1, wr_t, rbias, ws1, ws3, ws2, tm):
    t = x1.shape[0]
    full = lambda shp: pl.BlockSpec(shp, lambda i: (0, 0))
    return pl.pallas_call(
        _router_kernel,
        grid=(t // tm,),
        in_specs=[pl.BlockSpec((tm, D_MODEL), lambda i: (i, 0)),
                  full((N_EXPERTS, D_MODEL)), full((N_EXPERTS, 1)),
                  full((D_MODEL, SHARED_DIM)), full((D_MODEL, SHARED_DIM)), full((SHARED_DIM, D_MODEL))],
        out_specs=[pl.BlockSpec((EXPERT_TOPK, tm), lambda i: (0, i)),
                   pl.BlockSpec((EXPERT_TOPK, tm), lambda i: (0, i)),
                   pl.BlockSpec((tm, D_MODEL), lambda i: (i, 0)),
                   pl.BlockSpec((tm, D_MODEL), lambda i: (i, 0))],
        out_shape=[jax.ShapeDtypeStruct((EXPERT_TOPK, t), jnp.int32),
                   jax.ShapeDtypeStruct((EXPERT_TOPK, t), F32),
                   jax.ShapeDtypeStruct((t, D_MODEL), F32),
                   jax.ShapeDtypeStruct((t, D_MODEL), BF16)],
        compiler_params=_cparams(("parallel",)),
        name="router_shared",
    )(x1, wr_t, rbias.reshape(N_EXPERTS, 1), ws1, ws3, ws2)


MOE_ROWS = 256


def _dispatch(eidx_t, t):
    n = t * EXPERT_TOPK
    e_flat = eidx_t.T.reshape(n)
    tok_flat = jnp.repeat(jnp.arange(t, dtype=jnp.int32), EXPERT_TOPK)
    order = jnp.argsort(e_flat)
    e_sorted = e_flat[order]
    counts = jnp.zeros((N_EXPERTS,), jnp.int32).at[e_flat].add(1)
    padded = (counts + MOE_ROWS - 1) // MOE_ROWS * MOE_ROWS
    pad_end = jnp.cumsum(padded)
    pad_start = pad_end - padded
    start = jnp.cumsum(counts) - counts
    dest = pad_start[e_sorted] + jnp.arange(n, dtype=jnp.int32) - start[e_sorted]
    n_blocks = -(-n // MOE_ROWS) + N_EXPERTS
    rows = n_blocks * MOE_ROWS
    row_tok = jnp.full((rows,), t, jnp.int32).at[dest].set(tok_flat[order])
    pos = jnp.zeros((n,), jnp.int32).at[order].set(dest).reshape(t, EXPERT_TOPK)
    blk_exp = jnp.minimum(jnp.searchsorted(pad_end, jnp.arange(n_blocks, dtype=jnp.int32) * MOE_ROWS, side='right'),
                          N_EXPERTS - 1).astype(jnp.int32)
    n_used = (pad_end[-1:] // MOE_ROWS).astype(jnp.int32)
    return row_tok, pos, blk_exp, n_used


def _expert_kernel(be_ref, nu_ref, x_ref, w1_ref, w3_ref, w2_ref, o_ref, w1b_ref, w3b_ref, w2b_ref):
    i = pl.program_id(0)
    prev = be_ref[jnp.maximum(i - 1, 0)]

    @pl.when((i == 0) | (be_ref[i] != prev))
    def _():
        w1b_ref[...] = w1_ref[...].astype(BF16)
        w3b_ref[...] = w3_ref[...].astype(BF16)
        w2b_ref[...] = w2_ref[...].astype(BF16)

    @pl.when(i < nu_ref[0])
    def _():
        x = x_ref[...]
        hmid = _silu(_dot(x, w1b_ref[...])) * _dot(x, w3b_ref[...])
        o_ref[...] = _dot(hmid.astype(BF16), w2b_ref[...])

    @pl.when(i >= nu_ref[0])
    def _():
        o_ref[...] = jnp.zeros(o_ref.shape, F32)


def _experts(xg, blk_exp, n_used, w1, w3, w2):
    rows = xg.shape[0]
    n_blocks = rows // MOE_ROWS
    grid_spec = pltpu.PrefetchScalarGridSpec(
        num_scalar_prefetch=2,
        grid=(n_blocks,),
        in_specs=[
            pl.BlockSpec((MOE_ROWS, D_MODEL), lambda i, be, nu: (i, 0)),
            pl.BlockSpec((None, D_MODEL, EXPERT_DIM), lambda i, be, nu: (be[i], 0, 0)),
            pl.BlockSpec((None, D_MODEL, EXPERT_DIM), lambda i, be, nu: (be[i], 0, 0)),
            pl.BlockSpec((None, EXPERT_DIM, D_MODEL), lambda i, be, nu: (be[i], 0, 0)),
        ],
        out_specs=pl.BlockSpec((MOE_ROWS, D_MODEL), lambda i, be, nu: (i, 0)),
        scratch_shapes=[pltpu.VMEM((D_MODEL, EXPERT_DIM), BF16), pltpu.VMEM((D_MODEL, EXPERT_DIM), BF16),
                        pltpu.VMEM((EXPERT_DIM, D_MODEL), BF16)],
    )
    return pl.pallas_call(
        _expert_kernel,
        grid_spec=grid_spec,
        out_shape=jax.ShapeDtypeStruct((rows, D_MODEL), F32),
        compiler_params=_cparams(("arbitrary",)),
        name="experts",
    )(blk_exp, n_used, xg, w1, w3, w2)


def _combine_kernel(yg_ref, w_ref, base_ref, g_ref, b_ref, o_ref):
    acc = base_ref[...]
    w = w_ref[...]
    for j in range(EXPERT_TOPK):
        acc = acc + w[:, j:j + 1] * yg_ref[:, j, :]
    o_ref[...] = _layer_norm(acc, g_ref[...], b_ref[...])


def _combine(yg, wts, base, g, b, tm):
    t = base.shape[0]
    return pl.pallas_call(
        _combine_kernel,
        grid=(t // tm,),
        in_specs=[pl.BlockSpec((tm, EXPERT_TOPK, D_MODEL), lambda i: (i, 0, 0)),
                  pl.BlockSpec((tm, EXPERT_TOPK), lambda i: (i, 0)),
                  pl.BlockSpec((tm, D_MODEL), lambda i: (i, 0)),
                  pl.BlockSpec((1, D_MODEL), lambda i: (0, 0)),
                  pl.BlockSpec((1, D_MODEL), lambda i: (0, 0))],
        out_specs=pl.BlockSpec((tm, D_MODEL), lambda i: (i, 0)),
        out_shape=jax.ShapeDtypeStruct((t, D_MODEL), F32),
        compiler_params=_cparams(("parallel",)),
        name="combine_ln2",
    )(yg, wts, base, g.reshape(1, D_MODEL), b.reshape(1, D_MODEL))


def _moe(x1, wr_t, rbias, w1, w3, w2, ws1, ws3, ws2, g2, b2, tm):
    t = x1.shape[0]
    eidx_t, wts_t, base, xb = _router(x1, wr_t, rbias, ws1, ws3, ws2, tm)
    row_tok, pos, blk_exp, n_used = _dispatch(eidx_t, t)
    xb_pad = jnp.concatenate([xb, jnp.zeros((1, D_MODEL), BF16)], 0)
    yb = _experts(xb_pad[row_tok], blk_exp, n_used, w1, w3, w2)
    return _combine(yb[pos], wts_t.T, base, g2, b2, tm)


def _round_up(n, m):
    return -(-n // m) * m


def _pick_tile(n, pref):
    t = min(n, pref)
    assert n % t == 0
    return t


DSA_KEY_BLOCK = 512
DSA_Q_TILE = 128


def _trunk(x, pos0, conv_buf, s0, k_past, v_past, ik_past, prm):
    b, l, _ = x.shape
    t = b * l
    h2 = _project(x.reshape(t, D_MODEL), prm["w_in"], _pick_tile(t, 1024), 1024)
    h3 = h2.reshape(b, l, H_COLS)

    ya, s_new, buf_new = _gdn(h3, conv_buf, s0, prm["conv_w"], prm["a_log"], prm["dt_bias"], prm["gdn_norm_g"])

    pos = pos0 + jnp.arange(l, dtype=jnp.int32)
    q, qi, k_new, ki_new, wi, kt, kit = _prep(h3, pos, prm["idx_k_ln_g"], prm["idx_k_ln_b"], _pick_tile(l, 512))
    v_new = h3[:, :, COL_V:COL_V + ATT_KV]
    if k_past is None:
        l_all = l
        kt_all, kit_all, v_all = kt, kit, v_new.astype(BF16)
    else:
        past = k_past.shape[1]
        l_all = past + l
        kt_all = jnp.concatenate([jnp.swapaxes(k_past.reshape(b, past, ATT_KV), 1, 2).astype(BF16), kt], 2)
        kit_all = jnp.concatenate([jnp.swapaxes(ik_past, 1, 2).astype(BF16), kit], 2)
        v_all = jnp.concatenate([v_past.reshape(b, past, ATT_KV), v_new], 1).astype(BF16)
    lk = _round_up(l_all, DSA_KEY_BLOCK)
    if lk != l_all:
        kt_all = jnp.pad(kt_all, ((0, 0), (0, 0), (0, lk - l_all)))
        kit_all = jnp.pad(kit_all, ((0, 0), (0, 0), (0, lk - l_all)))
        v_all = jnp.pad(v_all, ((0, 0), (0, lk - l_all), (0, 0)))
    yb = _dsa(q, qi, wi, kit_all, kt_all, v_all, _pick_tile(l, DSA_Q_TILE), DSA_KEY_BLOCK, l_all, pos0)

    tm = _pick_tile(t, 256)
    x1 = _merge(ya.reshape(t, GDN_V), yb.reshape(t, ATT_Q), h2, x.reshape(t, D_MODEL),
                prm["w_o_gdn"], prm["w_o_dsa"], prm["w_out"], prm["ln1_g"], prm["ln1_b"], tm)
    y = _moe(x1, prm["w_router_t"], prm["router_bias"], prm["w1"], prm["w3"], prm["w2"],
             prm["ws1"], prm["ws3"], prm["ws2"], prm["ln2_g"], prm["ln2_b"], tm)
    state = (k_new.reshape(b, l, ATT_KV_HEADS, HEAD_DIM), v_new.reshape(b, l, ATT_KV_HEADS, HEAD_DIM),
             ki_new, s_new, buf_new)
    return y.reshape(b, l, D_MODEL), state


def kernel(x_prompt, x_sample, cache_k, cache_v, cache_idx_k, state_gdn, state_conv, w_in, conv_w, a_log, dt_bias, gdn_norm_g, w_o_gdn, idx_k_ln_g, idx_k_ln_b, w_o_dsa, w_out, ln1_g, ln1_b, w_router, router_bias, w1, w3, w2, ws1, ws3, ws2, ln2_g, ln2_b):
    assert w_in.shape[0] == DEPTH == 1
    b, s_len, _ = x_prompt.shape
    past = cache_k.shape[2]
    prm = dict(
        w_in=_repack_w_in(w_in[0]).astype(BF16), conv_w=conv_w[0], a_log=a_log[0], dt_bias=dt_bias[0],
        gdn_norm_g=gdn_norm_g[0], w_o_gdn=w_o_gdn[0].astype(BF16), idx_k_ln_g=idx_k_ln_g[0],
        idx_k_ln_b=idx_k_ln_b[0], w_o_dsa=w_o_dsa[0].astype(BF16), w_out=w_out[0].astype(BF16),
        ln1_g=ln1_g[0], ln1_b=ln1_b[0], w_router_t=w_router[0].T, router_bias=router_bias[0],
        w1=w1[0], w3=w3[0], w2=w2[0], ws1=ws1[0].astype(BF16), ws3=ws3[0].astype(BF16),
        ws2=ws2[0].astype(BF16), ln2_g=ln2_g[0], ln2_b=ln2_b[0])
    conv0 = jnp.zeros((b, CONV_W - 1, CONV_CH), F32)
    s0 = jnp.zeros((b, GDN_HEADS, GDN_DK, GDN_DV), F32)
    yp, sp = _trunk(x_prompt, 0, conv0, s0, None, None, None, prm)
    ys, ss = _trunk(x_sample, past, state_conv[0], state_gdn[0], cache_k[0], cache_v[0], cache_idx_k[0], prm)
    return (yp, ys) + tuple(a[None] for a in sp) + tuple(a[None] for a in ss)
```

```python
import functools

import jax
import jax.numpy as jnp
import numpy as np
from jax import lax
from jax.experimental import pallas as pl
from jax.experimental.pallas import tpu as pltpu

F32 = jnp.float32
BF16 = jnp.bfloat16

D_MODEL = 1024
CHUNK = 64
GDN_HEADS = 8
GDN_DK = 128
GDN_DV = 128
CONV_W = 4
ATT_HEADS = 8
ATT_KV_HEADS = 2
HEAD_DIM = 128
IDX_HEADS = 16
IDX_DIM = 64
TOPK_MAX = 256
ROPE_THETA = 500000.0
N_EXPERTS = 256
EXPERT_TOPK = 8
N_GROUPS = 8
TOPK_GROUPS = 4
EXPERT_DIM = 256
SHARED_DIM = 256
ROUTED_SCALE = 2.5
DEPTH = 1
DN_ALPHA = (2.0 * DEPTH) ** 0.25
LN_EPS = 1e-5
RMS_EPS = 1e-6

GDN_QK = GDN_HEADS * GDN_DK
GDN_V = GDN_HEADS * GDN_DV
CONV_CH = 2 * GDN_QK + GDN_V
ATT_Q = ATT_HEADS * HEAD_DIM
ATT_KV = ATT_KV_HEADS * HEAD_DIM
IDX_Q = IDX_HEADS * IDX_DIM
REP = ATT_HEADS // ATT_KV_HEADS

LANES = 128
SUBLANES = 8
VMEM_LIMIT = 56 * 1024 * 1024

PROJ_TM = 1024
PROJ_TN = 1024
PREP_TM = 512
DSA_Q_TILE = 128
DSA_KEY_BLOCK = 512
TOKEN_TM = 256
MOE_ROWS = 256

COL_QKV = 0
COL_Z = COL_QKV + CONV_CH
COL_Q = COL_Z + GDN_V
COL_QI = COL_Q + ATT_Q
COL_GA = COL_QI + IDX_Q
COL_GB = COL_GA + D_MODEL
COL_K = COL_GB + D_MODEL
COL_V = COL_K + ATT_KV
COL_SMALL = COL_V + ATT_KV
SMALL_W = 512
SM_A = IDX_DIM
SM_B = SM_A + GDN_HEADS
SM_WI = SM_B + GDN_HEADS
H_COLS = COL_SMALL + SMALL_W


def _cparams(sem):
    return pltpu.CompilerParams(dimension_semantics=sem, vmem_limit_bytes=VMEM_LIMIT)


def _dot(a, b):
    return jnp.dot(a, b, preferred_element_type=F32)


def _dot_bf(a, b):
    return jnp.dot(a.astype(BF16), b.astype(BF16), preferred_element_type=F32)


def _dot_hi(a, b):
    return jnp.dot(a, b, precision=lax.Precision.HIGHEST, preferred_element_type=F32)


def _dot_nt_hi(a, b):
    return lax.dot_general(a, b, (((1,), (1,)), ((), ())), precision=lax.Precision.HIGHEST,
                           preferred_element_type=F32)


def _dot_nt_bf(a, b):
    return lax.dot_general(a.astype(BF16), b.astype(BF16), (((1,), (1,)), ((), ())),
                           preferred_element_type=F32)


def _dot_tn_bf(a, b):
    return lax.dot_general(a.astype(BF16), b.astype(BF16), (((0,), (0,)), ((), ())),
                           preferred_element_type=F32)


def _sigmoid(x):
    return 1.0 / (1.0 + jnp.exp(-x))


def _silu(x):
    return x * _sigmoid(x)


def _round_up(n, m):
    return -(-n // m) * m


def _pick_tile(n, pref):
    t = min(n, pref)
    assert n % t == 0
    return t


def _repack_w_in(w_in):
    sizes = (CONV_CH, GDN_V, GDN_HEADS, GDN_HEADS, ATT_Q, ATT_KV, ATT_KV, IDX_Q, IDX_DIM, IDX_HEADS,
             D_MODEL, D_MODEL)
    offs = np.concatenate([[0], np.cumsum(sizes)])
    (p_qkv, p_z, p_a, p_b, p_q, p_k, p_v, p_qi, p_ki, p_wi, p_ga, p_gb) = [
        w_in[:, offs[i]:offs[i + 1]] for i in range(len(sizes))]
    pad = jnp.zeros((w_in.shape[0], SMALL_W - (SM_WI + IDX_HEADS)), w_in.dtype)
    return jnp.concatenate([p_qkv, p_z, p_q, p_qi, p_ga, p_gb, p_k, p_v, p_ki, p_a, p_b, p_wi, pad], axis=1)


def _proj_kernel(x_ref, w_ref, o_ref):
    o_ref[...] = _dot(x_ref[...].astype(BF16), w_ref[...])


def _project(x2d, w_bf):
    t, d = x2d.shape
    n = w_bf.shape[1]
    tm, tn = _pick_tile(t, PROJ_TM), _pick_tile(n, PROJ_TN)
    return pl.pallas_call(
        _proj_kernel,
        grid=(t // tm, n // tn),
        in_specs=[pl.BlockSpec((tm, d), lambda i, j: (i, 0)),
                  pl.BlockSpec((d, tn), lambda i, j: (0, j))],
        out_specs=pl.BlockSpec((tm, tn), lambda i, j: (i, j)),
        out_shape=jax.ShapeDtypeStruct((t, n), F32),
        compiler_params=_cparams(("parallel", "arbitrary")),
        name="in_proj",
    )(x2d, w_bf)


TAIL_ROWS = SUBLANES
INV_BASE = 8


def _split(a):
    hi = a.astype(BF16)
    return hi, (a - hi.astype(F32)).astype(BF16)


def _dot3(a, b):
    return _dot(a[0], b[0]) + (_dot(a[0], b[1]) + _dot(a[1], b[0]))


def _unit_lower_inverse(ms, c):
    ri = lax.broadcasted_iota(jnp.int32, (c, c), 0)
    ci = lax.broadcasted_iota(jnp.int32, (c, c), 1)
    eye = (ri == ci).astype(F32)
    blk = INV_BASE
    same = (ri // blk) == (ci // blk)
    ns = [jnp.where(same, -m, 0.0) for m in ms]
    xs = [eye + n for n in ns]
    span = 1
    while span * 2 < blk:
        nsp = [_split(n) for n in ns]
        ns = [_dot3(n, n) for n in nsp]
        nsp = [_split(n) for n in ns]
        xs = [x + _dot3(_split(x), n) for x, n in zip(xs, nsp)]
        span *= 2
    while blk < c:
        nxt = blk * 2
        emask = ((ri // nxt) == (ci // nxt)) & ((ri // blk) != (ci // blk))
        xsp = [_split(x) for x in xs]
        ts = [_dot3(x, _split(jnp.where(emask, m, 0.0))) for x, m in zip(xsp, ms)]
        xs = [x - _dot3(_split(t), xp) for x, t, xp in zip(xs, ts, xsp)]
        blk = nxt
    return xs


def _gdn_kernel(hq_ref, z_ref, sm_ref, buf_ref, s0_ref, cw_ref, alog_ref, dtb_ref, ng_ref,
                y_ref, snew_ref, bufnew_ref, xp_ref, s_ref, *, chunk):
    c_idx = pl.program_id(1)
    n_c = pl.num_programs(1)
    C = chunk
    keep = CONV_W - 1

    @pl.when(c_idx == 0)
    def _():
        xp_ref[TAIL_ROWS - keep:TAIL_ROWS, :] = buf_ref[...]
        s_ref[...] = s0_ref[...]

    xp_ref[TAIL_ROWS:TAIL_ROWS + C, :] = hq_ref[...]

    acc = cw_ref[0:1, :] * xp_ref[TAIL_ROWS - keep:TAIL_ROWS - keep + C, :]
    for j in range(1, CONV_W):
        acc = acc + cw_ref[j:j + 1, :] * xp_ref[TAIL_ROWS - keep + j:TAIL_ROWS - keep + j + C, :]
    conv = _silu(acc)

    @pl.when(c_idx == n_c - 1)
    def _():
        bufnew_ref[...] = xp_ref[TAIL_ROWS + C - keep:TAIL_ROWS + C, :]

    xp_ref[TAIL_ROWS - keep:TAIL_ROWS, :] = xp_ref[TAIL_ROWS + C - keep:TAIL_ROWS + C, :]

    sm = sm_ref[:, 0:LANES]
    xg = sm + dtb_ref[...]
    softplus = jnp.maximum(xg, 0.0) + jnp.log(1.0 + jnp.exp(-jnp.abs(xg)))
    g = -jnp.exp(alog_ref[...]) * softplus
    beta = _sigmoid(sm)

    ri = lax.broadcasted_iota(jnp.int32, (C, C), 0)
    ci = lax.broadcasted_iota(jnp.int32, (C, C), 1)
    incl = ri >= ci
    strict = ri > ci
    gc = _dot_hi(incl.astype(F32), g)
    gc_t = gc.T

    heads = range(GDN_HEADS)
    qs, ks, vs, gcs, bhs, egs, decays = [], [], [], [], [], [], []
    for h in heads:
        q = conv[:, h * GDN_DK:(h + 1) * GDN_DK]
        k = conv[:, GDN_QK + h * GDN_DK:GDN_QK + (h + 1) * GDN_DK]
        qs.append(q * lax.rsqrt(jnp.sum(q * q, -1, keepdims=True) + 1e-6) * (GDN_DK ** -0.5))
        ks.append(k * lax.rsqrt(jnp.sum(k * k, -1, keepdims=True) + 1e-6))
        vs.append(conv[:, 2 * GDN_QK + h * GDN_DV:2 * GDN_QK + (h + 1) * GDN_DV])
        gch = gc[:, SM_A + h:SM_A + h + 1]
        gcs.append(gch)
        bhs.append(beta[:, SM_B + h:SM_B + h + 1])
        egs.append(jnp.exp(gch))
        decays.append(jnp.exp(jnp.where(incl, gch - gc_t[SM_A + h:SM_A + h + 1, :], -jnp.inf)))
    kbs = [k * bh for k, bh in zip(ks, bhs)]
    ms = [jnp.where(strict, _dot_nt_bf(kb, k) * d, 0.0) for kb, k, d in zip(kbs, ks, decays)]
    attns = [_dot_nt_bf(q, k) * d for q, k, d in zip(qs, ks, decays)]
    tinvs = _unit_lower_inverse(ms, C)
    sols = [_dot_bf(t, jnp.concatenate([v * bh, kb * eg], axis=-1))
            for t, v, bh, kb, eg in zip(tinvs, vs, bhs, kbs, egs)]
    ss = [s_ref[h] for h in heads]
    v_news = [sol[:, :GDN_DV] - _dot_bf(sol[:, GDN_DV:], s) for sol, s in zip(sols, ss)]
    os_ = [_dot_bf(q * eg, s) + _dot_bf(a, vn) for q, eg, s, a, vn in zip(qs, egs, ss, attns, v_news)]
    for h in heads:
        glast = gcs[h][C - 1:C, :]
        s_ref[h] = ss[h] * jnp.exp(glast) + _dot_tn_bf(ks[h] * jnp.exp(glast - gcs[h]), v_news[h])
    for h in heads:
        o = os_[h]
        o = o * lax.rsqrt(jnp.mean(o * o, -1, keepdims=True) + RMS_EPS) * ng_ref[...]
        zh = z_ref[:, h * GDN_DV:(h + 1) * GDN_DV]
        y_ref[:, h * GDN_DV:(h + 1) * GDN_DV] = o * _silu(zh)

    @pl.when(c_idx == n_c - 1)
    def _():
        snew_ref[...] = s_ref[...]


def _gdn(h3, l, conv_buf, s0, conv_w, a_log, dt_bias, norm_g):
    b = h3.shape[0]
    chunk = min(CHUNK, l)
    assert l % chunk == 0 and chunk % SUBLANES == 0 and chunk >= CONV_W - 1
    kern = functools.partial(_gdn_kernel, chunk=chunk)
    keep = CONV_W - 1
    lane_row = lambda vec, at: jnp.zeros((1, LANES), F32).at[0, at:at + vec.shape[0]].set(vec)
    return pl.pallas_call(
        kern,
        grid=(b, l // chunk),
        in_specs=[
            pl.BlockSpec((None, chunk, CONV_CH), lambda i, c: (i, c, COL_QKV // CONV_CH)),
            pl.BlockSpec((None, chunk, GDN_V), lambda i, c: (i, c, COL_Z // GDN_V)),
            pl.BlockSpec((None, chunk, SMALL_W), lambda i, c: (i, c, COL_SMALL // SMALL_W)),
            pl.BlockSpec((None, keep, CONV_CH), lambda i, c: (i, 0, 0)),
            pl.BlockSpec((None, GDN_HEADS, GDN_DK, GDN_DV), lambda i, c: (i, 0, 0, 0)),
            pl.BlockSpec((CONV_W, CONV_CH), lambda i, c: (0, 0)),
            pl.BlockSpec((1, LANES), lambda i, c: (0, 0)),
            pl.BlockSpec((1, LANES), lambda i, c: (0, 0)),
            pl.BlockSpec((1, GDN_DV), lambda i, c: (0, 0)),
        ],
        out_specs=[
            pl.BlockSpec((None, chunk, GDN_V), lambda i, c: (i, c, 0)),
            pl.BlockSpec((None, GDN_HEADS, GDN_DK, GDN_DV), lambda i, c: (i, 0, 0, 0)),
            pl.BlockSpec((None, keep, CONV_CH), lambda i, c: (i, 0, 0)),
        ],
        out_shape=[
            jax.ShapeDtypeStruct((b, l, GDN_V), F32),
            jax.ShapeDtypeStruct((b, GDN_HEADS, GDN_DK, GDN_DV), F32),
            jax.ShapeDtypeStruct((b, keep, CONV_CH), F32),
        ],
        scratch_shapes=[
            pltpu.VMEM((TAIL_ROWS + chunk, CONV_CH), F32),
            pltpu.VMEM((GDN_HEADS, GDN_DK, GDN_DV), F32),
        ],
        compiler_params=_cparams(("parallel", "arbitrary")),
        name="gdn",
    )(h3, h3, h3, conv_buf, s0, conv_w, lane_row(a_log, SM_A), lane_row(dt_bias, SM_A),
      norm_g.reshape(1, GDN_DV))


def _rope_tables(pos, rot, period):
    half = rot // 2
    inv_freq = ROPE_THETA ** (-(2.0 / rot) * jnp.arange(half, dtype=F32))
    ang = pos.astype(F32)[:, None] * inv_freq[None, :]
    cos, sin = jnp.cos(ang), jnp.sin(ang)
    n = pos.shape[0]
    rest = period - rot
    c = jnp.concatenate([cos, cos, jnp.ones((n, rest), F32)], -1)
    sa = jnp.concatenate([-sin, jnp.zeros((n, half + rest), F32)], -1)
    sb = jnp.concatenate([jnp.zeros((n, half), F32), sin, jnp.zeros((n, rest), F32)], -1)
    reps = LANES // period
    return jnp.stack([jnp.tile(c, (1, reps)), jnp.tile(sa, (1, reps)), jnp.tile(sb, (1, reps))], 0)


def _rope128(x, tab_ref, half):
    return (x * tab_ref[0] + pltpu.roll(x, LANES - half, 1) * tab_ref[1]
            + pltpu.roll(x, half, 1) * tab_ref[2])


def _prep_kernel(q_ref, qi_ref, k_ref, v_ref, sm_ref, tq_ref, ti_ref, lng_ref, lnb_ref,
                 qt_ref, qit_ref, wit_ref, ko_ref, kb_ref, kio_ref, kib_ref, vo_ref, vt_ref):
    hq = HEAD_DIM // 8
    hi = IDX_DIM // 8
    for h in range(ATT_HEADS):
        x = _rope128(q_ref[:, h * HEAD_DIM:(h + 1) * HEAD_DIM], tq_ref, hq) * (HEAD_DIM ** -0.5)
        qt_ref[h * HEAD_DIM:(h + 1) * HEAD_DIM, :] = x.T.astype(BF16)
    for h in range(ATT_KV_HEADS):
        x = _rope128(k_ref[:, h * HEAD_DIM:(h + 1) * HEAD_DIM], tq_ref, hq)
        ko_ref[:, h * HEAD_DIM:(h + 1) * HEAD_DIM] = x
        kb_ref[:, h * HEAD_DIM:(h + 1) * HEAD_DIM] = x.astype(BF16)
        v = v_ref[:, h * HEAD_DIM:(h + 1) * HEAD_DIM]
        vo_ref[:, h * HEAD_DIM:(h + 1) * HEAD_DIM] = v
        vt_ref[h * HEAD_DIM:(h + 1) * HEAD_DIM, :] = v.T.astype(BF16)
    for c in range(IDX_Q // LANES):
        x = _rope128(qi_ref[:, c * LANES:(c + 1) * LANES], ti_ref, hi)
        qit_ref[c * LANES:(c + 1) * LANES, :] = x.T.astype(BF16)
    sm = sm_ref[:, 0:LANES]
    lane = lax.broadcasted_iota(jnp.int32, sm.shape, 1)
    is_ki = lane < IDX_DIM
    mu = jnp.sum(jnp.where(is_ki, sm, 0.0), -1, keepdims=True) * (1.0 / IDX_DIM)
    xc = jnp.where(is_ki, sm - mu, 0.0)
    var = jnp.sum(xc * xc, -1, keepdims=True) * (1.0 / IDX_DIM)
    ki = xc * lax.rsqrt(var + LN_EPS) * lng_ref[...] + lnb_ref[...]
    ki = _rope128(ki, ti_ref, hi)[:, 0:IDX_DIM]
    kio_ref[...] = ki
    kib_ref[...] = ki.astype(BF16)
    wit_ref[...] = sm.T[SM_WI:SM_WI + IDX_HEADS, :] * (IDX_HEADS ** -0.5 * IDX_DIM ** -0.5)


def _prep(h3, pos, ln_g, ln_b):
    b, l, _ = h3.shape
    tm = _pick_tile(l, PREP_TM)
    tab_q = _rope_tables(pos, HEAD_DIM // 4, HEAD_DIM)
    tab_i = _rope_tables(pos, IDX_DIM // 4, IDX_DIM)
    lng = jnp.concatenate([ln_g, jnp.zeros((LANES - IDX_DIM,), F32)]).reshape(1, LANES)
    lnb = jnp.concatenate([ln_b, jnp.zeros((LANES - IDX_DIM,), F32)]).reshape(1, LANES)
    rows = lambda w, col: pl.BlockSpec((None, tm, w), lambda i, t: (i, t, col))
    cols = lambda w: pl.BlockSpec((None, w, tm), lambda i, t: (i, 0, t))
    return pl.pallas_call(
        _prep_kernel,
        grid=(b, l // tm),
        in_specs=[
            rows(ATT_Q, COL_Q // ATT_Q), rows(IDX_Q, COL_QI // IDX_Q), rows(ATT_KV, COL_K // ATT_KV),
            rows(ATT_KV, COL_V // ATT_KV), rows(SMALL_W, COL_SMALL // SMALL_W),
            pl.BlockSpec((3, tm, LANES), lambda i, t: (0, t, 0)),
            pl.BlockSpec((3, tm, LANES), lambda i, t: (0, t, 0)),
            pl.BlockSpec((1, LANES), lambda i, t: (0, 0)),
            pl.BlockSpec((1, LANES), lambda i, t: (0, 0)),
        ],
        out_specs=[cols(ATT_Q), cols(IDX_Q), cols(IDX_HEADS), rows(ATT_KV, 0), rows(ATT_KV, 0),
                   rows(IDX_DIM, 0), rows(IDX_DIM, 0), rows(ATT_KV, 0), cols(ATT_KV)],
        out_shape=[
            jax.ShapeDtypeStruct((b, ATT_Q, l), BF16),
            jax.ShapeDtypeStruct((b, IDX_Q, l), BF16),
            jax.ShapeDtypeStruct((b, IDX_HEADS, l), F32),
            jax.ShapeDtypeStruct((b, l, ATT_KV), F32),
            jax.ShapeDtypeStruct((b, l, ATT_KV), BF16),
            jax.ShapeDtypeStruct((b, l, IDX_DIM), F32),
            jax.ShapeDtypeStruct((b, l, IDX_DIM), BF16),
            jax.ShapeDtypeStruct((b, l, ATT_KV), F32),
            jax.ShapeDtypeStruct((b, ATT_KV, l), BF16),
        ],
        compiler_params=_cparams(("parallel", "parallel")),
        name="dsa_prep",
    )(h3, h3, h3, h3, h3, tab_q, tab_i, lng, lnb)


INT_MIN = -2 ** 31
INT_MAX = 2 ** 31 - 1
NEG_INF_KEY = INT_MIN + 0x7FFFFF
NEG_BIG = -1e30
HEAD_PAIRS = IDX_HEADS // 2
COUNT_CHAINS = 8


def _dsa_kernel(qt_ref, qit_ref, wit_ref, ki_ref, k_ref, vt_ref, o_ref,
                key_ref, qs_ref, qip_ref, m_ref, l_ref, acc_ref, cm_ref,
                *, tq, kb, l_true, pos0, topk, idx_bits):
    q0 = pos0 + pl.program_id(1) * tq
    qpos = q0 + lax.broadcasted_iota(jnp.int32, (1, tq), 1)
    lim = jnp.minimum((qpos // CHUNK + 1) * CHUNK, l_true)
    lim_max = jnp.minimum(((q0 + tq - 1) // CHUNK + 1) * CHUNK, l_true)
    nkb = (lim_max + kb - 1) // kb
    key_iota = lax.broadcasted_iota(jnp.int32, (kb, tq), 0)
    wit = wit_ref[...]

    for p in range(HEAD_PAIRS):
        for u in range(2):
            hh = 2 * p + u
            qip_ref[p, :, u * tq:(u + 1) * tq] = qit_ref[hh * IDX_DIM:(hh + 1) * IDX_DIM, :]
    for g in range(ATT_KV_HEADS):
        for r in range(REP):
            hh = g * REP + r
            qs_ref[g, :, r * tq:(r + 1) * tq] = qt_ref[hh * HEAD_DIM:(hh + 1) * HEAD_DIM, :]

    def score_blk(j, carry):
        off = pl.multiple_of(j * kb, kb)
        ki = ki_ref[pl.ds(off, kb), :]
        acc = jnp.zeros((kb, tq), F32)
        for p in range(HEAD_PAIRS):
            s2 = _dot(ki, qip_ref[p])
            acc = (acc + wit[2 * p:2 * p + 1, :] * jnp.maximum(s2[:, :tq], 0.0)
                   + wit[2 * p + 1:2 * p + 2, :] * jnp.maximum(s2[:, tq:], 0.0))
        score = jnp.where(off + key_iota < lim, acc, -jnp.inf)
        bits = pltpu.bitcast(score, jnp.int32)
        key_ref[pl.ds(off, kb), :] = bits ^ ((bits >> 31) & INT_MAX)
        return carry

    lax.fori_loop(0, nkb, score_blk, 0)

    def count(pred_fn):
        def blk(j, c):
            off = pl.multiple_of(j * kb, kb)
            part = jnp.where(pred_fn(key_ref[pl.ds(off, kb), :], off), 1.0, 0.0)
            return c + jnp.sum(part.reshape(kb // (COUNT_CHAINS * SUBLANES), COUNT_CHAINS * SUBLANES, tq), axis=0)
        c = lax.fori_loop(0, nkb, blk, jnp.zeros((COUNT_CHAINS * SUBLANES, tq), F32))
        return jnp.sum(c, axis=0, keepdims=True)

    def bit_step(t, cur):
        cand_u = cur | lax.shift_left(jnp.int32(1), 31 - t)
        cand_s = cand_u ^ INT_MIN
        cnt = count(lambda kk, off: kk >= cand_s)
        return jnp.where(cnt >= topk, cand_u, cur)

    thr = lax.fori_loop(0, 32, bit_step, jnp.zeros((1, tq), jnp.int32)) ^ INT_MIN
    n_gt = count(lambda kk, off: kk > thr)
    n_ge = count(lambda kk, off: kk >= thr)
    need = topk - n_gt

    cm_ref[...] = jnp.full((1, tq), INT_MAX, jnp.int32)

    @pl.when(jnp.max(jnp.where((n_ge > topk) & (thr > NEG_INF_KEY), 1, 0)) > 0)
    def _():
        def idx_step(t, cm):
            cand = cm | lax.shift_left(jnp.int32(1), idx_bits - 1 - t)
            before = count(lambda kk, off: (kk == thr) & (off + key_iota < cand))
            return jnp.where(before < need, cand, cm)
        cm_ref[...] = lax.fori_loop(0, idx_bits, idx_step, jnp.zeros((1, tq), jnp.int32))

    cm = cm_ref[...]

    m_ref[...] = jnp.full(m_ref.shape, NEG_BIG, F32)
    l_ref[...] = jnp.zeros(l_ref.shape, F32)
    acc_ref[...] = jnp.zeros(acc_ref.shape, F32)
    groups = range(ATT_KV_HEADS)

    def attn_blk(j, carry):
        off = pl.multiple_of(j * kb, kb)
        kk = key_ref[pl.ds(off, kb), :]
        kpos = off + key_iota
        sel = ((kk > thr) | ((kk == thr) & (kpos <= cm))) & (kpos < lim)
        bias = jnp.where(sel, 0.0, NEG_BIG)
        bias = jnp.concatenate([bias] * REP, axis=1)
        logits = [_dot(k_ref[pl.ds(off, kb), g * HEAD_DIM:(g + 1) * HEAD_DIM], qs_ref[g]) + bias for g in groups]
        m_old = [m_ref[g] for g in groups]
        m_new = [jnp.maximum(m_old[g], jnp.max(logits[g], axis=0, keepdims=True)) for g in groups]
        ps = [jnp.exp(logits[g] - m_new[g]) for g in groups]
        alphas = [jnp.exp(m_old[g] - m_new[g]) for g in groups]
        pvs = [_dot(vt_ref[g * HEAD_DIM:(g + 1) * HEAD_DIM, pl.ds(off, kb)], ps[g].astype(BF16)) for g in groups]
        for g in groups:
            l_ref[g] = alphas[g] * l_ref[g] + jnp.sum(ps[g], axis=0, keepdims=True)
            acc_ref[g] = alphas[g] * acc_ref[g] + pvs[g]
            m_ref[g] = m_new[g]
        return carry

    lax.fori_loop(0, nkb, attn_blk, 0)

    for g in groups:
        o_t = acc_ref[g] / l_ref[g]
        for r in range(REP):
            hh = g * REP + r
            o_ref[:, hh * HEAD_DIM:(hh + 1) * HEAD_DIM] = o_t[:, r * tq:(r + 1) * tq].T


def _dsa(qt, qit, wit, ki, k, vt, l_true, pos0):
    b, _, t = qt.shape
    lk = k.shape[1]
    tq, kb = DSA_Q_TILE, DSA_KEY_BLOCK
    assert t % tq == 0 and lk % kb == 0 and tq == HEAD_DIM
    topk = min(TOPK_MAX, l_true // 4)
    kern = functools.partial(_dsa_kernel, tq=tq, kb=kb, l_true=l_true, pos0=pos0, topk=topk,
                             idx_bits=int(lk).bit_length())
    return pl.pallas_call(
        kern,
        grid=(b, t // tq),
        in_specs=[
            pl.BlockSpec((None, ATT_Q, tq), lambda i, t_: (i, 0, t_)),
            pl.BlockSpec((None, IDX_Q, tq), lambda i, t_: (i, 0, t_)),
            pl.BlockSpec((None, IDX_HEADS, tq), lambda i, t_: (i, 0, t_)),
            pl.BlockSpec((None, lk, IDX_DIM), lambda i, t_: (i, 0, 0)),
            pl.BlockSpec((None, lk, ATT_KV), lambda i, t_: (i, 0, 0)),
            pl.BlockSpec((None, ATT_KV, lk), lambda i, t_: (i, 0, 0)),
        ],
        out_specs=pl.BlockSpec((None, tq, ATT_Q), lambda i, t_: (i, t_, 0)),
        out_shape=jax.ShapeDtypeStruct((b, t, ATT_Q), F32),
        scratch_shapes=[
            pltpu.VMEM((lk, tq), jnp.int32),
            pltpu.VMEM((ATT_KV_HEADS, HEAD_DIM, REP * tq), BF16),
            pltpu.VMEM((HEAD_PAIRS, IDX_DIM, 2 * tq), BF16),
            pltpu.VMEM((ATT_KV_HEADS, 1, REP * tq), F32),
            pltpu.VMEM((ATT_KV_HEADS, 1, REP * tq), F32),
            pltpu.VMEM((ATT_KV_HEADS, HEAD_DIM, REP * tq), F32),
            pltpu.VMEM((1, tq), jnp.int32),
        ],
        compiler_params=_cparams(("parallel", "arbitrary")),
        name="dsa",
    )(qt, qit, wit, ki, k, vt)


def _layer_norm(x, g, b):
    mu = jnp.mean(x, -1, keepdims=True)
    xc = x - mu
    var = jnp.mean(xc * xc, -1, keepdims=True)
    return xc * lax.rsqrt(var + LN_EPS) * g + b


def _merge_kernel(ya_ref, yb_ref, ga_ref, gb_ref, x_ref, wa_ref, wb_ref, wo_ref, g_ref, b_ref, o_ref):
    pa = _dot(ya_ref[...].astype(BF16), wa_ref[...])
    pb = _dot(yb_ref[...].astype(BF16), wb_ref[...])
    merged = _sigmoid(ga_ref[...]) * pa + _sigmoid(gb_ref[...]) * pb
    y = DN_ALPHA * x_ref[...] + _dot(merged.astype(BF16), wo_ref[...])
    o_ref[...] = _layer_norm(y, g_ref[...], b_ref[...])


def _merge(ya, yb, h3, x, wa, wb, wo, g, b):
    bsz, l, _ = x.shape
    tm = _pick_tile(l, TOKEN_TM)
    row = lambda c: pl.BlockSpec((None, tm, D_MODEL), lambda i, t: (i, t, c))
    full = lambda shp: pl.BlockSpec(shp, lambda i, t: (0, 0))
    return pl.pallas_call(
        _merge_kernel,
        grid=(bsz, l // tm),
        in_specs=[row(0), row(0), row(COL_GA // D_MODEL), row(COL_GB // D_MODEL), row(0),
                  full((GDN_V, D_MODEL)), full((ATT_Q, D_MODEL)), full((D_MODEL, D_MODEL)),
                  full((1, D_MODEL)), full((1, D_MODEL))],
        out_specs=row(0),
        out_shape=jax.ShapeDtypeStruct((bsz, l, D_MODEL), F32),
        compiler_params=_cparams(("parallel", "parallel")),
        name="merge_ln1",
    )(ya, yb, h3, h3, x, wa, wb, wo, g.reshape(1, D_MODEL), b.reshape(1, D_MODEL))


GROUP_SIZE = N_EXPERTS // N_GROUPS


def _first_max(cur, rows, n_rows):
    m = jnp.max(cur, axis=0, keepdims=True)
    idx = jnp.min(jnp.where(cur == m, rows, n_rows), axis=0, keepdims=True)
    return m, idx


def _router_kernel(x_ref, wr_ref, rb_ref, ws1_ref, ws3_ref, ws2_ref,
                   eidx_ref, wts_ref, base_ref, xb_ref):
    x = x_ref[...]
    tm = x.shape[0]
    logits = _dot_nt_hi(wr_ref[...], x)
    scores = _sigmoid(logits)
    biased = scores + rb_ref[...]
    neg = -jnp.inf

    rows_g = lax.broadcasted_iota(jnp.int32, (GROUP_SIZE, tm), 0)
    gs = []
    for g in range(N_GROUPS):
        blk = biased[g * GROUP_SIZE:(g + 1) * GROUP_SIZE, :]
        m1, i1 = _first_max(blk, rows_g, GROUP_SIZE)
        m2 = jnp.max(jnp.where(rows_g == i1, neg, blk), axis=0, keepdims=True)
        gs.append(m1 + m2)
    cur = jnp.concatenate(gs, axis=0)
    rows_n = lax.broadcasted_iota(jnp.int32, (N_GROUPS, tm), 0)
    gsel = jnp.zeros((N_GROUPS, tm), F32)
    for _ in range(TOPK_GROUPS):
        _, ig = _first_max(cur, rows_n, N_GROUPS)
        hit = rows_n == ig
        gsel = jnp.where(hit, 1.0, gsel)
        cur = jnp.where(hit, neg, cur)
    gexp = jnp.concatenate([jnp.broadcast_to(gsel[g:g + 1, :], (GROUP_SIZE, tm)) for g in range(N_GROUPS)], axis=0)
    cur = jnp.where(gexp > 0.0, biased, neg)

    rows_e = lax.broadcasted_iota(jnp.int32, (N_EXPERTS, tm), 0)
    es, ws = [], []
    for _ in range(EXPERT_TOPK):
        _, ie = _first_max(cur, rows_e, N_EXPERTS)
        hit = rows_e == ie
        es.append(ie)
        ws.append(jnp.sum(jnp.where(hit, scores, 0.0), axis=0, keepdims=True))
        cur = jnp.where(hit, neg, cur)
    w = jnp.concatenate(ws, axis=0)
    eidx_ref[...] = jnp.concatenate(es, axis=0)
    wts_ref[...] = w / jnp.sum(w, axis=0, keepdims=True) * ROUTED_SCALE

    xb = x.astype(BF16)
    xb_ref[...] = xb
    hs = _silu(_dot(xb, ws1_ref[...])) * _dot(xb, ws3_ref[...])
    base_ref[...] = DN_ALPHA * x + _dot(hs.astype(BF16), ws2_ref[...])


def _router(x1, wr_t, rbias, ws1, ws3, ws2, tm):
    t = x1.shape[0]
    full = lambda shp: pl.BlockSpec(shp, lambda i: (0, 0))
    return pl.pallas_call(
        _router_kernel,
        grid=(t // tm,),
        in_specs=[pl.BlockSpec((tm, D_MODEL), lambda i: (i, 0)),
                  full((N_EXPERTS, D_MODEL)), full((N_EXPERTS, 1)),
                  full((D_MODEL, SHARED_DIM)), full((D_MODEL, SHARED_DIM)), full((SHARED_DIM, D_MODEL))],
        out_specs=[pl.BlockSpec((EXPERT_TOPK, tm), lambda i: (0, i)),
                   pl.BlockSpec((EXPERT_TOPK, tm), lambda i: (0, i)),
                   pl.BlockSpec((tm, D_MODEL), lambda i: (i, 0)),
                   pl.BlockSpec((tm, D_MODEL), lambda i: (i, 0))],
        out_shape=[jax.ShapeDtypeStruct((EXPERT_TOPK, t), jnp.int32),
                   jax.ShapeDtypeStruct((EXPERT_TOPK, t), F32),
                   jax.ShapeDtypeStruct((t, D_MODEL), F32),
                   jax.ShapeDtypeStruct((t, D_MODEL), BF16)],
        compiler_params=_cparams(("parallel",)),
        name="router_shared",
    )(x1, wr_t, rbias.reshape(N_EXPERTS, 1), ws1, ws3, ws2)


def _dispatch(eidx_t, t):
    n = t * EXPERT_TOPK
    e_flat = eidx_t.T.reshape(n)
    tok_flat = jnp.repeat(jnp.arange(t, dtype=jnp.int32), EXPERT_TOPK)
    order = jnp.argsort(e_flat)
    e_sorted = e_flat[order]
    counts = jnp.zeros((N_EXPERTS,), jnp.int32).at[e_flat].add(1)
    padded = (counts + MOE_ROWS - 1) // MOE_ROWS * MOE_ROWS
    pad_end = jnp.cumsum(padded)
    pad_start = pad_end - padded
    start = jnp.cumsum(counts) - counts
    dest = pad_start[e_sorted] + jnp.arange(n, dtype=jnp.int32) - start[e_sorted]
    n_blocks = -(-n // MOE_ROWS) + N_EXPERTS
    rows = n_blocks * MOE_ROWS
    row_tok = jnp.full((rows,), t, jnp.int32).at[dest].set(tok_flat[order])
    pos = jnp.zeros((n,), jnp.int32).at[order].set(dest).reshape(t, EXPERT_TOPK)
    blk_exp = jnp.minimum(jnp.searchsorted(pad_end, jnp.arange(n_blocks, dtype=jnp.int32) * MOE_ROWS, side='right'),
                          N_EXPERTS - 1).astype(jnp.int32)
    n_used = (pad_end[-1:] // MOE_ROWS).astype(jnp.int32)
    return row_tok, pos, blk_exp, n_used


def _expert_kernel(be_ref, nu_ref, x_ref, w1_ref, w3_ref, w2_ref, o_ref, w1b_ref, w3b_ref, w2b_ref):
    i = pl.program_id(0)
    prev = be_ref[jnp.maximum(i - 1, 0)]

    @pl.when((i == 0) | (be_ref[i] != prev))
    def _():
        w1b_ref[...] = w1_ref[...].astype(BF16)
        w3b_ref[...] = w3_ref[...].astype(BF16)
        w2b_ref[...] = w2_ref[...].astype(BF16)

    @pl.when(i < nu_ref[0])
    def _():
        x = x_ref[...]
        hmid = _silu(_dot(x, w1b_ref[...])) * _dot(x, w3b_ref[...])
        o_ref[...] = _dot(hmid.astype(BF16), w2b_ref[...])

    @pl.when(i >= nu_ref[0])
    def _():
        o_ref[...] = jnp.zeros(o_ref.shape, F32)


def _experts(xg, blk_exp, n_used, w1, w3, w2):
    rows = xg.shape[0]
    n_blocks = rows // MOE_ROWS
    grid_spec = pltpu.PrefetchScalarGridSpec(
        num_scalar_prefetch=2,
        grid=(n_blocks,),
        in_specs=[
            pl.BlockSpec((MOE_ROWS, D_MODEL), lambda i, be, nu: (i, 0)),
            pl.BlockSpec((None, D_MODEL, EXPERT_DIM), lambda i, be, nu: (be[i], 0, 0)),
            pl.BlockSpec((None, D_MODEL, EXPERT_DIM), lambda i, be, nu: (be[i], 0, 0)),
            pl.BlockSpec((None, EXPERT_DIM, D_MODEL), lambda i, be, nu: (be[i], 0, 0)),
        ],
        out_specs=pl.BlockSpec((MOE_ROWS, D_MODEL), lambda i, be, nu: (i, 0)),
        scratch_shapes=[pltpu.VMEM((D_MODEL, EXPERT_DIM), BF16), pltpu.VMEM((D_MODEL, EXPERT_DIM), BF16),
                        pltpu.VMEM((EXPERT_DIM, D_MODEL), BF16)],
    )
    return pl.pallas_call(
        _expert_kernel,
        grid_spec=grid_spec,
        out_shape=jax.ShapeDtypeStruct((rows, D_MODEL), F32),
        compiler_params=_cparams(("arbitrary",)),
        name="experts",
    )(blk_exp, n_used, xg, w1, w3, w2)


def _combine_kernel(yg_ref, w_ref, base_ref, g_ref, b_ref, o_ref):
    acc = base_ref[...]
    w = w_ref[...]
    for j in range(EXPERT_TOPK):
        acc = acc + w[:, j:j + 1] * yg_ref[:, j, :]
    o_ref[...] = _layer_norm(acc, g_ref[...], b_ref[...])


def _combine(yg, wts, base, g, b, tm):
    t = base.shape[0]
    return pl.pallas_call(
        _combine_kernel,
        grid=(t // tm,),
        in_specs=[pl.BlockSpec((tm, EXPERT_TOPK, D_MODEL), lambda i: (i, 0, 0)),
                  pl.BlockSpec((tm, EXPERT_TOPK), lambda i: (i, 0)),
                  pl.BlockSpec((tm, D_MODEL), lambda i: (i, 0)),
                  pl.BlockSpec((1, D_MODEL), lambda i: (0, 0)),
                  pl.BlockSpec((1, D_MODEL), lambda i: (0, 0))],
        out_specs=pl.BlockSpec((tm, D_MODEL), lambda i: (i, 0)),
        out_shape=jax.ShapeDtypeStruct((t, D_MODEL), F32),
        compiler_params=_cparams(("parallel",)),
        name="combine_ln2",
    )(yg, wts, base, g.reshape(1, D_MODEL), b.reshape(1, D_MODEL))


def _moe(x1, prm):
    t = x1.shape[0]
    tm = _pick_tile(t, TOKEN_TM)
    eidx_t, wts_t, base, xb = _router(x1, prm["w_router_t"], prm["router_bias"], prm["ws1"], prm["ws3"],
                                      prm["ws2"], tm)
    row_tok, pos, blk_exp, n_used = _dispatch(eidx_t, t)
    xb_pad = jnp.concatenate([xb, jnp.zeros((1, D_MODEL), BF16)], 0)
    yb = _experts(xb_pad[row_tok], blk_exp, n_used, prm["w1"], prm["w3"], prm["w2"])
    return _combine(yb[pos], wts_t.T, base, prm["ln2_g"], prm["ln2_b"], tm)


def _mixer(x, pos0, conv_buf, s0, k_past, v_past, ik_past, prm):
    b, l, _ = x.shape
    lp = _round_up(l, DSA_Q_TILE)
    xp = x if lp == l else jnp.pad(x, ((0, 0), (0, lp - l), (0, 0)))
    h3 = _project(xp.reshape(b * lp, D_MODEL), prm["w_in"]).reshape(b, lp, H_COLS)

    ya, s_new, buf_new = _gdn(h3, l, conv_buf, s0, prm["conv_w"], prm["a_log"], prm["dt_bias"], prm["gdn_norm_g"])

    pos = pos0 + jnp.arange(lp, dtype=jnp.int32)
    qt, qit, wit, k_new, k_bf, ki_new, ki_bf, v_new, vt = _prep(h3, pos, prm["idx_k_ln_g"], prm["idx_k_ln_b"])
    if k_past is None:
        l_all = l
        k_all, ki_all, vt_all = k_bf, ki_bf, vt
    else:
        past = k_past.shape[1]
        l_all = past + l
        k_all = jnp.concatenate([k_past.reshape(b, past, ATT_KV).astype(BF16), k_bf[:, :l]], 1)
        ki_all = jnp.concatenate([ik_past.astype(BF16), ki_bf[:, :l]], 1)
        vt_all = jnp.concatenate([jnp.swapaxes(v_past.reshape(b, past, ATT_KV), 1, 2).astype(BF16), vt[:, :, :l]], 2)
    lk = _round_up(l_all, DSA_KEY_BLOCK)
    if lk != l_all:
        k_all = jnp.pad(k_all, ((0, 0), (0, lk - l_all), (0, 0)))
        ki_all = jnp.pad(ki_all, ((0, 0), (0, lk - l_all), (0, 0)))
        vt_all = jnp.pad(vt_all, ((0, 0), (0, 0), (0, lk - l_all)))
    yb = _dsa(qt, qit, wit, ki_all, k_all, vt_all, l_all, pos0)

    x1 = _merge(ya, yb, h3, x, prm["w_o_gdn"], prm["w_o_dsa"], prm["w_out"], prm["ln1_g"], prm["ln1_b"])
    state = (k_new[:, :l].reshape(b, l, ATT_KV_HEADS, HEAD_DIM), v_new[:, :l].reshape(b, l, ATT_KV_HEADS, HEAD_DIM),
             ki_new[:, :l], s_new, buf_new)
    return x1, state


def kernel(x_prompt, x_sample, cache_k, cache_v, cache_idx_k, state_gdn, state_conv, w_in, conv_w, a_log, dt_bias, gdn_norm_g, w_o_gdn, idx_k_ln_g, idx_k_ln_b, w_o_dsa, w_out, ln1_g, ln1_b, w_router, router_bias, w1, w3, w2, ws1, ws3, ws2, ln2_g, ln2_b):
    assert w_in.shape[0] == DEPTH == 1
    bp, lp_, _ = x_prompt.shape
    bs, ls_, _ = x_sample.shape
    past = cache_k.shape[2]
    prm = dict(
        w_in=_repack_w_in(w_in[0]).astype(BF16), conv_w=conv_w[0], a_log=a_log[0], dt_bias=dt_bias[0],
        gdn_norm_g=gdn_norm_g[0], w_o_gdn=w_o_gdn[0].astype(BF16), idx_k_ln_g=idx_k_ln_g[0],
        idx_k_ln_b=idx_k_ln_b[0], w_o_dsa=w_o_dsa[0].astype(BF16), w_out=w_out[0].astype(BF16),
        ln1_g=ln1_g[0], ln1_b=ln1_b[0], w_router_t=w_router[0].T, router_bias=router_bias[0],
        w1=w1[0], w3=w3[0], w2=w2[0], ws1=ws1[0].astype(BF16), ws3=ws3[0].astype(BF16),
        ws2=ws2[0].astype(BF16), ln2_g=ln2_g[0], ln2_b=ln2_b[0])
    conv0 = jnp.zeros((bp, CONV_W - 1, CONV_CH), F32)
    s0 = jnp.zeros((bp, GDN_HEADS, GDN_DK, GDN_DV), F32)
    x1p, sp = _mixer(x_prompt, 0, conv0, s0, None, None, None, prm)
    x1s, ss = _mixer(x_sample, past, state_conv[0], state_gdn[0], cache_k[0], cache_v[0], cache_idx_k[0], prm)
    tp, ts = bp * lp_, bs * ls_
    y = _moe(jnp.concatenate([x1p.reshape(tp, D_MODEL), x1s.reshape(ts, D_MODEL)], 0), prm)
    yp = y[:tp].reshape(bp, lp_, D_MODEL)
    ys = y[tp:].reshape(bs, ls_, D_MODEL)
    return (yp, ys) + tuple(a[None] for a in sp) + tuple(a[None] for a in ss)
```

```python
import functools

import jax
import jax.numpy as jnp
import numpy as np
from jax import lax
from jax.experimental import pallas as pl
from jax.experimental.pallas import tpu as pltpu

F32 = jnp.float32
BF16 = jnp.bfloat16

D_MODEL = 1024
CHUNK = 64
GDN_HEADS = 8
GDN_DK = 128
GDN_DV = 128
CONV_W = 4
ATT_HEADS = 8
ATT_KV_HEADS = 2
HEAD_DIM = 128
IDX_HEADS = 16
IDX_DIM = 64
TOPK_MAX = 256
ROPE_THETA = 500000.0
N_EXPERTS = 256
EXPERT_TOPK = 8
N_GROUPS = 8
TOPK_GROUPS = 4
EXPERT_DIM = 256
SHARED_DIM = 256
ROUTED_SCALE = 2.5
DEPTH = 1
DN_ALPHA = (2.0 * DEPTH) ** 0.25
LN_EPS = 1e-5
RMS_EPS = 1e-6

GDN_QK = GDN_HEADS * GDN_DK
GDN_V = GDN_HEADS * GDN_DV
CONV_CH = 2 * GDN_QK + GDN_V
ATT_Q = ATT_HEADS * HEAD_DIM
ATT_KV = ATT_KV_HEADS * HEAD_DIM
IDX_Q = IDX_HEADS * IDX_DIM
REP = ATT_HEADS // ATT_KV_HEADS

LANES = 128
SUBLANES = 8
VMEM_LIMIT = 56 * 1024 * 1024

PROJ_TM = 1024
PROJ_TN = 1024
PREP_TM = 512
DSA_Q_TILE = 128
DSA_KEY_BLOCK = 512
TOKEN_TM = 256
COMBINE_TM = 128
MOE_ROWS = 256

COL_QKV = 0
COL_Z = COL_QKV + CONV_CH
COL_Q = COL_Z + GDN_V
COL_QI = COL_Q + ATT_Q
COL_GA = COL_QI + IDX_Q
COL_GB = COL_GA + D_MODEL
COL_K = COL_GB + D_MODEL
COL_V = COL_K + ATT_KV
COL_SMALL = COL_V + ATT_KV
SMALL_W = 512
SM_A = IDX_DIM
SM_B = SM_A + GDN_HEADS
SM_WI = SM_B + GDN_HEADS
H_COLS = COL_SMALL + SMALL_W


def _cparams(sem):
    return pltpu.CompilerParams(dimension_semantics=sem, vmem_limit_bytes=VMEM_LIMIT)


def _dot(a, b):
    return jnp.dot(a, b, preferred_element_type=F32)


def _dot_bf(a, b):
    return jnp.dot(a.astype(BF16), b.astype(BF16), preferred_element_type=F32)


def _dot_hi(a, b):
    return jnp.dot(a, b, precision=lax.Precision.HIGHEST, preferred_element_type=F32)


def _dot_nt_hi(a, b):
    return lax.dot_general(a, b, (((1,), (1,)), ((), ())), precision=lax.Precision.HIGHEST,
                           preferred_element_type=F32)


def _dot_nt_bf(a, b):
    return lax.dot_general(a.astype(BF16), b.astype(BF16), (((1,), (1,)), ((), ())),
                           preferred_element_type=F32)


def _dot_tn_bf(a, b):
    return lax.dot_general(a.astype(BF16), b.astype(BF16), (((0,), (0,)), ((), ())),
                           preferred_element_type=F32)


def _sigmoid(x):
    return 1.0 / (1.0 + jnp.exp(-x))


def _silu(x):
    return x * _sigmoid(x)


def _round_up(n, m):
    return -(-n // m) * m


def _pick_tile(n, pref):
    t = min(n, pref)
    assert n % t == 0
    return t


def _repack_w_in(w_in):
    sizes = (CONV_CH, GDN_V, GDN_HEADS, GDN_HEADS, ATT_Q, ATT_KV, ATT_KV, IDX_Q, IDX_DIM, IDX_HEADS,
             D_MODEL, D_MODEL)
    offs = np.concatenate([[0], np.cumsum(sizes)])
    (p_qkv, p_z, p_a, p_b, p_q, p_k, p_v, p_qi, p_ki, p_wi, p_ga, p_gb) = [
        w_in[:, offs[i]:offs[i + 1]] for i in range(len(sizes))]
    pad = jnp.zeros((w_in.shape[0], SMALL_W - (SM_WI + IDX_HEADS)), w_in.dtype)
    return jnp.concatenate([p_qkv, p_z, p_q, p_qi, p_ga, p_gb, p_k, p_v, p_ki, p_a, p_b, p_wi, pad], axis=1)


def _proj_kernel(x_ref, w_ref, o_ref):
    o_ref[...] = _dot(x_ref[...].astype(BF16), w_ref[...])


def _project(x2d, w_bf):
    t, d = x2d.shape
    n = w_bf.shape[1]
    tm, tn = _pick_tile(t, PROJ_TM), _pick_tile(n, PROJ_TN)
    return pl.pallas_call(
        _proj_kernel,
        grid=(t // tm, n // tn),
        in_specs=[pl.BlockSpec((tm, d), lambda i, j: (i, 0)),
                  pl.BlockSpec((d, tn), lambda i, j: (0, j))],
        out_specs=pl.BlockSpec((tm, tn), lambda i, j: (i, j)),
        out_shape=jax.ShapeDtypeStruct((t, n), F32),
        compiler_params=_cparams(("parallel", "arbitrary")),
        name="in_proj",
    )(x2d, w_bf)


TAIL_ROWS = SUBLANES
INV_BASE = 8


def _split(a):
    hi = a.astype(BF16)
    return hi, (a - hi.astype(F32)).astype(BF16)


def _dot3(a, b):
    return _dot(a[0], b[0]) + (_dot(a[0], b[1]) + _dot(a[1], b[0]))


def _unit_lower_inverse(ms, c):
    ri = lax.broadcasted_iota(jnp.int32, (c, c), 0)
    ci = lax.broadcasted_iota(jnp.int32, (c, c), 1)
    eye = (ri == ci).astype(F32)
    blk = INV_BASE
    same = (ri // blk) == (ci // blk)
    ns = [jnp.where(same, -m, 0.0) for m in ms]
    xs = [eye + n for n in ns]
    span = 1
    while span * 2 < blk:
        nsp = [_split(n) for n in ns]
        ns = [_dot3(n, n) for n in nsp]
        nsp = [_split(n) for n in ns]
        xs = [x + _dot3(_split(x), n) for x, n in zip(xs, nsp)]
        span *= 2
    while blk < c:
        nxt = blk * 2
        emask = ((ri // nxt) == (ci // nxt)) & ((ri // blk) != (ci // blk))
        xsp = [_split(x) for x in xs]
        ts = [_dot3(x, _split(jnp.where(emask, m, 0.0))) for x, m in zip(xsp, ms)]
        xs = [x - _dot3(_split(t), xp) for x, t, xp in zip(xs, ts, xsp)]
        blk = nxt
    return xs


def _gdn_kernel(hq_ref, z_ref, sm_ref, buf_ref, s0_ref, cw_ref, alog_ref, dtb_ref, ng_ref,
                y_ref, snew_ref, bufnew_ref, xp_ref, s_ref, *, chunk):
    c_idx = pl.program_id(1)
    n_c = pl.num_programs(1)
    C = chunk
    keep = CONV_W - 1

    @pl.when(c_idx == 0)
    def _():
        xp_ref[TAIL_ROWS - keep:TAIL_ROWS, :] = buf_ref[...]
        s_ref[...] = s0_ref[...]

    xp_ref[TAIL_ROWS:TAIL_ROWS + C, :] = hq_ref[...]

    acc = cw_ref[0:1, :] * xp_ref[TAIL_ROWS - keep:TAIL_ROWS - keep + C, :]
    for j in range(1, CONV_W):
        acc = acc + cw_ref[j:j + 1, :] * xp_ref[TAIL_ROWS - keep + j:TAIL_ROWS - keep + j + C, :]
    conv = _silu(acc)

    @pl.when(c_idx == n_c - 1)
    def _():
        bufnew_ref[...] = xp_ref[TAIL_ROWS + C - keep:TAIL_ROWS + C, :]

    xp_ref[TAIL_ROWS - keep:TAIL_ROWS, :] = xp_ref[TAIL_ROWS + C - keep:TAIL_ROWS + C, :]

    sm = sm_ref[:, 0:LANES]
    xg = sm + dtb_ref[...]
    softplus = jnp.maximum(xg, 0.0) + jnp.log(1.0 + jnp.exp(-jnp.abs(xg)))
    g = -jnp.exp(alog_ref[...]) * softplus
    beta = _sigmoid(sm)

    ri = lax.broadcasted_iota(jnp.int32, (C, C), 0)
    ci = lax.broadcasted_iota(jnp.int32, (C, C), 1)
    incl = ri >= ci
    strict = ri > ci
    gc = _dot_hi(incl.astype(F32), g)
    gc_t = gc.T

    heads = range(GDN_HEADS)
    qs, ks, vs, gcs, bhs, egs, decays = [], [], [], [], [], [], []
    for h in heads:
        q = conv[:, h * GDN_DK:(h + 1) * GDN_DK]
        k = conv[:, GDN_QK + h * GDN_DK:GDN_QK + (h + 1) * GDN_DK]
        qs.append(q * lax.rsqrt(jnp.sum(q * q, -1, keepdims=True) + 1e-6) * (GDN_DK ** -0.5))
        ks.append(k * lax.rsqrt(jnp.sum(k * k, -1, keepdims=True) + 1e-6))
        vs.append(conv[:, 2 * GDN_QK + h * GDN_DV:2 * GDN_QK + (h + 1) * GDN_DV])
        gch = gc[:, SM_A + h:SM_A + h + 1]
        gcs.append(gch)
        bhs.append(beta[:, SM_B + h:SM_B + h + 1])
        egs.append(jnp.exp(gch))
        decays.append(jnp.exp(jnp.where(incl, gch - gc_t[SM_A + h:SM_A + h + 1, :], -jnp.inf)))
    kbs = [k * bh for k, bh in zip(ks, bhs)]
    ms = [jnp.where(strict, _dot_nt_bf(kb, k) * d, 0.0) for kb, k, d in zip(kbs, ks, decays)]
    attns = [_dot_nt_bf(q, k) * d for q, k, d in zip(qs, ks, decays)]
    tinvs = _unit_lower_inverse(ms, C)
    sols = [_dot_bf(t, jnp.concatenate([v * bh, kb * eg], axis=-1))
            for t, v, bh, kb, eg in zip(tinvs, vs, bhs, kbs, egs)]
    ss = [s_ref[h] for h in heads]
    v_news = [sol[:, :GDN_DV] - _dot_bf(sol[:, GDN_DV:], s) for sol, s in zip(sols, ss)]
    os_ = [_dot_bf(q * eg, s) + _dot_bf(a, vn) for q, eg, s, a, vn in zip(qs, egs, ss, attns, v_news)]
    for h in heads:
        glast = gcs[h][C - 1:C, :]
        s_ref[h] = ss[h] * jnp.exp(glast) + _dot_tn_bf(ks[h] * jnp.exp(glast - gcs[h]), v_news[h])
    for h in heads:
        o = os_[h]
        o = o * lax.rsqrt(jnp.mean(o * o, -1, keepdims=True) + RMS_EPS) * ng_ref[...]
        zh = z_ref[:, h * GDN_DV:(h + 1) * GDN_DV]
        y_ref[:, h * GDN_DV:(h + 1) * GDN_DV] = o * _silu(zh)

    @pl.when(c_idx == n_c - 1)
    def _():
        snew_ref[...] = s_ref[...]


def _gdn(h3, l, conv_buf, s0, conv_w, a_log, dt_bias, norm_g):
    b = h3.shape[0]
    chunk = min(CHUNK, l)
    assert l % chunk == 0 and chunk % SUBLANES == 0 and chunk >= CONV_W - 1
    kern = functools.partial(_gdn_kernel, chunk=chunk)
    keep = CONV_W - 1
    lane_row = lambda vec, at: jnp.zeros((1, LANES), F32).at[0, at:at + vec.shape[0]].set(vec)
    return pl.pallas_call(
        kern,
        grid=(b, l // chunk),
        in_specs=[
            pl.BlockSpec((None, chunk, CONV_CH), lambda i, c: (i, c, COL_QKV // CONV_CH)),
            pl.BlockSpec((None, chunk, GDN_V), lambda i, c: (i, c, COL_Z // GDN_V)),
            pl.BlockSpec((None, chunk, SMALL_W), lambda i, c: (i, c, COL_SMALL // SMALL_W)),
            pl.BlockSpec((None, keep, CONV_CH), lambda i, c: (i, 0, 0)),
            pl.BlockSpec((None, GDN_HEADS, GDN_DK, GDN_DV), lambda i, c: (i, 0, 0, 0)),
            pl.BlockSpec((CONV_W, CONV_CH), lambda i, c: (0, 0)),
            pl.BlockSpec((1, LANES), lambda i, c: (0, 0)),
            pl.BlockSpec((1, LANES), lambda i, c: (0, 0)),
            pl.BlockSpec((1, GDN_DV), lambda i, c: (0, 0)),
        ],
        out_specs=[
            pl.BlockSpec((None, chunk, GDN_V), lambda i, c: (i, c, 0)),
            pl.BlockSpec((None, GDN_HEADS, GDN_DK, GDN_DV), lambda i, c: (i, 0, 0, 0)),
            pl.BlockSpec((None, keep, CONV_CH), lambda i, c: (i, 0, 0)),
        ],
        out_shape=[
            jax.ShapeDtypeStruct((b, l, GDN_V), F32),
            jax.ShapeDtypeStruct((b, GDN_HEADS, GDN_DK, GDN_DV), F32),
            jax.ShapeDtypeStruct((b, keep, CONV_CH), F32),
        ],
        scratch_shapes=[
            pltpu.VMEM((TAIL_ROWS + chunk, CONV_CH), F32),
            pltpu.VMEM((GDN_HEADS, GDN_DK, GDN_DV), F32),
        ],
        compiler_params=_cparams(("parallel", "arbitrary")),
        name="gdn",
    )(h3, h3, h3, conv_buf, s0, conv_w, lane_row(a_log, SM_A), lane_row(dt_bias, SM_A),
      norm_g.reshape(1, GDN_DV))


def _rope_tables(pos, rot, period):
    half = rot // 2
    inv_freq = ROPE_THETA ** (-(2.0 / rot) * jnp.arange(half, dtype=F32))
    ang = pos.astype(F32)[:, None] * inv_freq[None, :]
    cos, sin = jnp.cos(ang), jnp.sin(ang)
    n = pos.shape[0]
    rest = period - rot
    c = jnp.concatenate([cos, cos, jnp.ones((n, rest), F32)], -1)
    sa = jnp.concatenate([-sin, jnp.zeros((n, half + rest), F32)], -1)
    sb = jnp.concatenate([jnp.zeros((n, half), F32), sin, jnp.zeros((n, rest), F32)], -1)
    reps = LANES // period
    return jnp.stack([jnp.tile(c, (1, reps)), jnp.tile(sa, (1, reps)), jnp.tile(sb, (1, reps))], 0)


def _rope128(x, tab_ref, half):
    return (x * tab_ref[0] + pltpu.roll(x, LANES - half, 1) * tab_ref[1]
            + pltpu.roll(x, half, 1) * tab_ref[2])


def _prep_kernel(q_ref, qi_ref, k_ref, v_ref, sm_ref, tq_ref, ti_ref, lng_ref, lnb_ref,
                 qt_ref, qit_ref, wit_ref, ko_ref, kb_ref, kio_ref, kib_ref, vo_ref, vt_ref):
    hq = HEAD_DIM // 8
    hi = IDX_DIM // 8
    for h in range(ATT_HEADS):
        x = _rope128(q_ref[:, h * HEAD_DIM:(h + 1) * HEAD_DIM], tq_ref, hq) * (HEAD_DIM ** -0.5)
        qt_ref[h * HEAD_DIM:(h + 1) * HEAD_DIM, :] = x.T.astype(BF16)
    for h in range(ATT_KV_HEADS):
        x = _rope128(k_ref[:, h * HEAD_DIM:(h + 1) * HEAD_DIM], tq_ref, hq)
        ko_ref[:, h * HEAD_DIM:(h + 1) * HEAD_DIM] = x
        kb_ref[:, h * HEAD_DIM:(h + 1) * HEAD_DIM] = x.astype(BF16)
        v = v_ref[:, h * HEAD_DIM:(h + 1) * HEAD_DIM]
        vo_ref[:, h * HEAD_DIM:(h + 1) * HEAD_DIM] = v
        vt_ref[h * HEAD_DIM:(h + 1) * HEAD_DIM, :] = v.T.astype(BF16)
    for c in range(IDX_Q // LANES):
        x = _rope128(qi_ref[:, c * LANES:(c + 1) * LANES], ti_ref, hi)
        qit_ref[c * LANES:(c + 1) * LANES, :] = x.T.astype(BF16)
    sm = sm_ref[:, 0:LANES]
    lane = lax.broadcasted_iota(jnp.int32, sm.shape, 1)
    is_ki = lane < IDX_DIM
    mu = jnp.sum(jnp.where(is_ki, sm, 0.0), -1, keepdims=True) * (1.0 / IDX_DIM)
    xc = jnp.where(is_ki, sm - mu, 0.0)
    var = jnp.sum(xc * xc, -1, keepdims=True) * (1.0 / IDX_DIM)
    ki = xc * lax.rsqrt(var + LN_EPS) * lng_ref[...] + lnb_ref[...]
    ki = _rope128(ki, ti_ref, hi)[:, 0:IDX_DIM]
    kio_ref[...] = ki
    kib_ref[...] = ki.astype(BF16)
    wit_ref[...] = sm.T[SM_WI:SM_WI + IDX_HEADS, :] * (IDX_HEADS ** -0.5 * IDX_DIM ** -0.5)


def _prep(h3, pos, ln_g, ln_b):
    b, l, _ = h3.shape
    tm = _pick_tile(l, PREP_TM)
    tab_q = _rope_tables(pos, HEAD_DIM // 4, HEAD_DIM)
    tab_i = _rope_tables(pos, IDX_DIM // 4, IDX_DIM)
    lng = jnp.concatenate([ln_g, jnp.zeros((LANES - IDX_DIM,), F32)]).reshape(1, LANES)
    lnb = jnp.concatenate([ln_b, jnp.zeros((LANES - IDX_DIM,), F32)]).reshape(1, LANES)
    rows = lambda w, col: pl.BlockSpec((None, tm, w), lambda i, t: (i, t, col))
    cols = lambda w: pl.BlockSpec((None, w, tm), lambda i, t: (i, 0, t))
    return pl.pallas_call(
        _prep_kernel,
        grid=(b, l // tm),
        in_specs=[
            rows(ATT_Q, COL_Q // ATT_Q), rows(IDX_Q, COL_QI // IDX_Q), rows(ATT_KV, COL_K // ATT_KV),
            rows(ATT_KV, COL_V // ATT_KV), rows(SMALL_W, COL_SMALL // SMALL_W),
            pl.BlockSpec((3, tm, LANES), lambda i, t: (0, t, 0)),
            pl.BlockSpec((3, tm, LANES), lambda i, t: (0, t, 0)),
            pl.BlockSpec((1, LANES), lambda i, t: (0, 0)),
            pl.BlockSpec((1, LANES), lambda i, t: (0, 0)),
        ],
        out_specs=[cols(ATT_Q), cols(IDX_Q), cols(IDX_HEADS), rows(ATT_KV, 0), rows(ATT_KV, 0),
                   rows(IDX_DIM, 0), rows(IDX_DIM, 0), rows(ATT_KV, 0), cols(ATT_KV)],
        out_shape=[
            jax.ShapeDtypeStruct((b, ATT_Q, l), BF16),
            jax.ShapeDtypeStruct((b, IDX_Q, l), BF16),
            jax.ShapeDtypeStruct((b, IDX_HEADS, l), F32),
            jax.ShapeDtypeStruct((b, l, ATT_KV), F32),
            jax.ShapeDtypeStruct((b, l, ATT_KV), BF16),
            jax.ShapeDtypeStruct((b, l, IDX_DIM), F32),
            jax.ShapeDtypeStruct((b, l, IDX_DIM), BF16),
            jax.ShapeDtypeStruct((b, l, ATT_KV), F32),
            jax.ShapeDtypeStruct((b, ATT_KV, l), BF16),
        ],
        compiler_params=_cparams(("parallel", "parallel")),
        name="dsa_prep",
    )(h3, h3, h3, h3, h3, tab_q, tab_i, lng, lnb)


INT_MIN = -2 ** 31
INT_MAX = 2 ** 31 - 1
NEG_INF_KEY = INT_MIN + 0x7FFFFF
NEG_BIG = -1e30
HEAD_PAIRS = IDX_HEADS // 2
COUNT_CHAINS = 8


def _dsa_kernel(qt_ref, qit_ref, wit_ref, ki_ref, k_ref, vt_ref, o_ref,
                key_ref, qs_ref, qip_ref, m_ref, l_ref, acc_ref, cm_ref,
                *, tq, kb, l_true, pos0, topk, idx_bits):
    q0 = pos0 + pl.program_id(1) * tq
    qpos = q0 + lax.broadcasted_iota(jnp.int32, (1, tq), 1)
    lim = jnp.minimum((qpos // CHUNK + 1) * CHUNK, l_true)
    lim_max = jnp.minimum(((q0 + tq - 1) // CHUNK + 1) * CHUNK, l_true)
    nkb = (lim_max + kb - 1) // kb
    key_iota = lax.broadcasted_iota(jnp.int32, (kb, tq), 0)
    wit = wit_ref[...]

    for p in range(HEAD_PAIRS):
        for u in range(2):
            hh = 2 * p + u
            qip_ref[p, :, u * tq:(u + 1) * tq] = qit_ref[hh * IDX_DIM:(hh + 1) * IDX_DIM, :]
    for g in range(ATT_KV_HEADS):
        for r in range(REP):
            hh = g * REP + r
            qs_ref[g, :, r * tq:(r + 1) * tq] = qt_ref[hh * HEAD_DIM:(hh + 1) * HEAD_DIM, :]

    def score_blk(j, carry):
        off = pl.multiple_of(j * kb, kb)
        ki = ki_ref[pl.ds(off, kb), :]
        acc = jnp.zeros((kb, tq), F32)
        for p in range(HEAD_PAIRS):
            s2 = _dot(ki, qip_ref[p])
            acc = (acc + wit[2 * p:2 * p + 1, :] * jnp.maximum(s2[:, :tq], 0.0)
                   + wit[2 * p + 1:2 * p + 2, :] * jnp.maximum(s2[:, tq:], 0.0))
        score = jnp.where(off + key_iota < lim, acc, -jnp.inf)
        bits = pltpu.bitcast(score, jnp.int32)
        key_ref[pl.ds(off, kb), :] = bits ^ ((bits >> 31) & INT_MAX)
        return carry

    lax.fori_loop(0, nkb, score_blk, 0)

    def count(pred_fn):
        def blk(j, c):
            off = pl.multiple_of(j * kb, kb)
            part = jnp.where(pred_fn(key_ref[pl.ds(off, kb), :], off), 1.0, 0.0)
            return c + jnp.sum(part.reshape(kb // (COUNT_CHAINS * SUBLANES), COUNT_CHAINS * SUBLANES, tq), axis=0)
        c = lax.fori_loop(0, nkb, blk, jnp.zeros((COUNT_CHAINS * SUBLANES, tq), F32))
        return jnp.sum(c, axis=0, keepdims=True)

    def bit_step(t, cur):
        cand_u = cur | lax.shift_left(jnp.int32(1), 31 - t)
        cand_s = cand_u ^ INT_MIN
        cnt = count(lambda kk, off: kk >= cand_s)
        return jnp.where(cnt >= topk, cand_u, cur)

    thr = lax.fori_loop(0, 32, bit_step, jnp.zeros((1, tq), jnp.int32)) ^ INT_MIN
    n_gt = count(lambda kk, off: kk > thr)
    n_ge = count(lambda kk, off: kk >= thr)
    need = topk - n_gt

    cm_ref[...] = jnp.full((1, tq), INT_MAX, jnp.int32)

    @pl.when(jnp.max(jnp.where((n_ge > topk) & (thr > NEG_INF_KEY), 1, 0)) > 0)
    def _():
        def idx_step(t, cm):
            cand = cm | lax.shift_left(jnp.int32(1), idx_bits - 1 - t)
            before = count(lambda kk, off: (kk == thr) & (off + key_iota < cand))
            return jnp.where(before < need, cand, cm)
        cm_ref[...] = lax.fori_loop(0, idx_bits, idx_step, jnp.zeros((1, tq), jnp.int32))

    cm = cm_ref[...]

    m_ref[...] = jnp.full(m_ref.shape, NEG_BIG, F32)
    l_ref[...] = jnp.zeros(l_ref.shape, F32)
    acc_ref[...] = jnp.zeros(acc_ref.shape, F32)
    groups = range(ATT_KV_HEADS)

    def attn_blk(j, carry):
        off = pl.multiple_of(j * kb, kb)
        kk = key_ref[pl.ds(off, kb), :]
        kpos = off + key_iota
        sel = ((kk > thr) | ((kk == thr) & (kpos <= cm))) & (kpos < lim)
        bias = jnp.where(sel, 0.0, NEG_BIG)
        bias = jnp.concatenate([bias] * REP, axis=1)
        logits = [_dot(k_ref[pl.ds(off, kb), g * HEAD_DIM:(g + 1) * HEAD_DIM], qs_ref[g]) + bias for g in groups]
        m_old = [m_ref[g] for g in groups]
        m_new = [jnp.maximum(m_old[g], jnp.max(logits[g], axis=0, keepdims=True)) for g in groups]
        ps = [jnp.exp(logits[g] - m_new[g]) for g in groups]
        alphas = [jnp.exp(m_old[g] - m_new[g]) for g in groups]
        pvs = [_dot(vt_ref[g * HEAD_DIM:(g + 1) * HEAD_DIM, pl.ds(off, kb)], ps[g].astype(BF16)) for g in groups]
        for g in groups:
            l_ref[g] = alphas[g] * l_ref[g] + jnp.sum(ps[g], axis=0, keepdims=True)
            acc_ref[g] = alphas[g] * acc_ref[g] + pvs[g]
            m_ref[g] = m_new[g]
        return carry

    lax.fori_loop(0, nkb, attn_blk, 0)

    for g in groups:
        o_t = acc_ref[g] / l_ref[g]
        for r in range(REP):
            hh = g * REP + r
            o_ref[:, hh * HEAD_DIM:(hh + 1) * HEAD_DIM] = o_t[:, r * tq:(r + 1) * tq].T


def _dsa(qt, qit, wit, ki, k, vt, l_true, pos0):
    b, _, t = qt.shape
    lk = k.shape[1]
    tq, kb = DSA_Q_TILE, DSA_KEY_BLOCK
    assert t % tq == 0 and lk % kb == 0 and tq == HEAD_DIM
    topk = min(TOPK_MAX, l_true // 4)
    kern = functools.partial(_dsa_kernel, tq=tq, kb=kb, l_true=l_true, pos0=pos0, topk=topk,
                             idx_bits=int(lk).bit_length())
    return pl.pallas_call(
        kern,
        grid=(b, t // tq),
        in_specs=[
            pl.BlockSpec((None, ATT_Q, tq), lambda i, t_: (i, 0, t_)),
            pl.BlockSpec((None, IDX_Q, tq), lambda i, t_: (i, 0, t_)),
            pl.BlockSpec((None, IDX_HEADS, tq), lambda i, t_: (i, 0, t_)),
            pl.BlockSpec((None, lk, IDX_DIM), lambda i, t_: (i, 0, 0)),
            pl.BlockSpec((None, lk, ATT_KV), lambda i, t_: (i, 0, 0)),
            pl.BlockSpec((None, ATT_KV, lk), lambda i, t_: (i, 0, 0)),
        ],
        out_specs=pl.BlockSpec((None, tq, ATT_Q), lambda i, t_: (i, t_, 0)),
        out_shape=jax.ShapeDtypeStruct((b, t, ATT_Q), F32),
        scratch_shapes=[
            pltpu.VMEM((lk, tq), jnp.int32),
            pltpu.VMEM((ATT_KV_HEADS, HEAD_DIM, REP * tq), BF16),
            pltpu.VMEM((HEAD_PAIRS, IDX_DIM, 2 * tq), BF16),
            pltpu.VMEM((ATT_KV_HEADS, 1, REP * tq), F32),
            pltpu.VMEM((ATT_KV_HEADS, 1, REP * tq), F32),
            pltpu.VMEM((ATT_KV_HEADS, HEAD_DIM, REP * tq), F32),
            pltpu.VMEM((1, tq), jnp.int32),
        ],
        compiler_params=_cparams(("parallel", "arbitrary")),
        name="dsa",
    )(qt, qit, wit, ki, k, vt)


def _layer_norm(x, g, b):
    mu = jnp.mean(x, -1, keepdims=True)
    xc = x - mu
    var = jnp.mean(xc * xc, -1, keepdims=True)
    return xc * lax.rsqrt(var + LN_EPS) * g + b


def _merge_kernel(ya_ref, yb_ref, ga_ref, gb_ref, x_ref, wa_ref, wb_ref, wo_ref, g_ref, b_ref, o_ref):
    pa = _dot(ya_ref[...].astype(BF16), wa_ref[...])
    pb = _dot(yb_ref[...].astype(BF16), wb_ref[...])
    merged = _sigmoid(ga_ref[...]) * pa + _sigmoid(gb_ref[...]) * pb
    y = DN_ALPHA * x_ref[...] + _dot(merged.astype(BF16), wo_ref[...])
    o_ref[...] = _layer_norm(y, g_ref[...], b_ref[...])


def _merge(ya, yb, h3, x, wa, wb, wo, g, b):
    bsz, l, _ = x.shape
    tm = _pick_tile(l, TOKEN_TM)
    row = lambda c: pl.BlockSpec((None, tm, D_MODEL), lambda i, t: (i, t, c))
    full = lambda shp: pl.BlockSpec(shp, lambda i, t: (0, 0))
    return pl.pallas_call(
        _merge_kernel,
        grid=(bsz, l // tm),
        in_specs=[row(0), row(0), row(COL_GA // D_MODEL), row(COL_GB // D_MODEL), row(0),
                  full((GDN_V, D_MODEL)), full((ATT_Q, D_MODEL)), full((D_MODEL, D_MODEL)),
                  full((1, D_MODEL)), full((1, D_MODEL))],
        out_specs=row(0),
        out_shape=jax.ShapeDtypeStruct((bsz, l, D_MODEL), F32),
        compiler_params=_cparams(("parallel", "parallel")),
        name="merge_ln1",
    )(ya, yb, h3, h3, x, wa, wb, wo, g.reshape(1, D_MODEL), b.reshape(1, D_MODEL))


GROUP_SIZE = N_EXPERTS // N_GROUPS


def _first_max(cur, rows, n_rows):
    m = jnp.max(cur, axis=0, keepdims=True)
    idx = jnp.min(jnp.where(cur == m, rows, n_rows), axis=0, keepdims=True)
    return m, idx


def _router_kernel(x_ref, wr_ref, rb_ref, ws1_ref, ws3_ref, ws2_ref,
                   eidx_ref, wts_ref, rank_ref, cnt_ref, base_ref, run_ref):
    @pl.when(pl.program_id(0) == 0)
    def _():
        run_ref[...] = jnp.zeros(run_ref.shape, F32)

    x = x_ref[...]
    tm = x.shape[0]
    logits = _dot_nt_hi(wr_ref[...], x)
    scores = _sigmoid(logits)
    biased = scores + rb_ref[...]
    neg = -jnp.inf

    rows_g = lax.broadcasted_iota(jnp.int32, (GROUP_SIZE, tm), 0)
    gs = []
    for g in range(N_GROUPS):
        blk = biased[g * GROUP_SIZE:(g + 1) * GROUP_SIZE, :]
        m1, i1 = _first_max(blk, rows_g, GROUP_SIZE)
        m2 = jnp.max(jnp.where(rows_g == i1, neg, blk), axis=0, keepdims=True)
        gs.append(m1 + m2)
    cur = jnp.concatenate(gs, axis=0)
    rows_n = lax.broadcasted_iota(jnp.int32, (N_GROUPS, tm), 0)
    gsel = jnp.zeros((N_GROUPS, tm), F32)
    for _ in range(TOPK_GROUPS):
        _, ig = _first_max(cur, rows_n, N_GROUPS)
        hit = rows_n == ig
        gsel = jnp.where(hit, 1.0, gsel)
        cur = jnp.where(hit, neg, cur)
    gexp = jnp.concatenate([jnp.broadcast_to(gsel[g:g + 1, :], (GROUP_SIZE, tm)) for g in range(N_GROUPS)], axis=0)
    cur = jnp.where(gexp > 0.0, biased, neg)

    rows_e = lax.broadcasted_iota(jnp.int32, (N_EXPERTS, tm), 0)
    es, ws, hits = [], [], []
    for _ in range(EXPERT_TOPK):
        _, ie = _first_max(cur, rows_e, N_EXPERTS)
        hit = rows_e == ie
        es.append(ie)
        hits.append(hit)
        ws.append(jnp.sum(jnp.where(hit, scores, 0.0), axis=0, keepdims=True))
        cur = jnp.where(hit, neg, cur)
    w = jnp.concatenate(ws, axis=0)
    eidx_ref[...] = jnp.concatenate(es, axis=0)
    wts_ref[...] = w / jnp.sum(w, axis=0, keepdims=True) * ROUTED_SCALE

    assigned = jnp.zeros((N_EXPERTS, tm), F32)
    for hit in hits:
        assigned = jnp.where(hit, 1.0, assigned)
    earlier = (lax.broadcasted_iota(jnp.int32, (tm, tm), 0) < lax.broadcasted_iota(jnp.int32, (tm, tm), 1))
    before = run_ref[...] + _dot(assigned.astype(BF16), earlier.astype(BF16))
    rank_ref[...] = jnp.concatenate(
        [jnp.sum(jnp.where(hit, before, 0.0), axis=0, keepdims=True) for hit in hits], axis=0).astype(jnp.int32)
    run_ref[...] = run_ref[...] + jnp.sum(assigned, axis=1, keepdims=True)
    cnt_ref[...] = run_ref[...].astype(jnp.int32)

    xb = x.astype(BF16)
    hs = _silu(_dot(xb, ws1_ref[...])) * _dot(xb, ws3_ref[...])
    base_ref[...] = DN_ALPHA * x + _dot(hs.astype(BF16), ws2_ref[...])


def _router(x1, wr_t, rbias, ws1, ws3, ws2, tm):
    t = x1.shape[0]
    full = lambda shp: pl.BlockSpec(shp, lambda i: (0, 0))
    return pl.pallas_call(
        _router_kernel,
        grid=(t // tm,),
        in_specs=[pl.BlockSpec((tm, D_MODEL), lambda i: (i, 0)),
                  full((N_EXPERTS, D_MODEL)), full((N_EXPERTS, 1)),
                  full((D_MODEL, SHARED_DIM)), full((D_MODEL, SHARED_DIM)), full((SHARED_DIM, D_MODEL))],
        out_specs=[pl.BlockSpec((EXPERT_TOPK, tm), lambda i: (0, i)),
                   pl.BlockSpec((EXPERT_TOPK, tm), lambda i: (0, i)),
                   pl.BlockSpec((EXPERT_TOPK, tm), lambda i: (0, i)),
                   full((N_EXPERTS, 1)),
                   pl.BlockSpec((tm, D_MODEL), lambda i: (i, 0))],
        out_shape=[jax.ShapeDtypeStruct((EXPERT_TOPK, t), jnp.int32),
                   jax.ShapeDtypeStruct((EXPERT_TOPK, t), F32),
                   jax.ShapeDtypeStruct((EXPERT_TOPK, t), jnp.int32),
                   jax.ShapeDtypeStruct((N_EXPERTS, 1), jnp.int32),
                   jax.ShapeDtypeStruct((t, D_MODEL), F32)],
        scratch_shapes=[pltpu.VMEM((N_EXPERTS, 1), F32)],
        compiler_params=_cparams(("arbitrary",)),
        name="router_shared",
    )(x1, wr_t, rbias.reshape(N_EXPERTS, 1), ws1, ws3, ws2)


def _block_plan(counts, n_blocks):
    padded = (counts + MOE_ROWS - 1) // MOE_ROWS * MOE_ROWS
    pad_end = jnp.cumsum(padded)
    blk_first = jnp.arange(n_blocks, dtype=jnp.int32) * MOE_ROWS
    blk_exp = jnp.minimum(jnp.sum((pad_end[None, :] <= blk_first[:, None]).astype(jnp.int32), axis=1), N_EXPERTS - 1)
    return pad_end - padded, blk_exp, pad_end[-1:] // MOE_ROWS


def _pos_kernel(e_ref, r_ref, ps_ref, pos_ref):
    tm = e_ref.shape[1]
    rows_e = lax.broadcasted_iota(jnp.int32, (N_EXPERTS, tm), 0)
    first = [jnp.sum(jnp.where(rows_e == e_ref[j:j + 1, :], ps_ref[...], 0), axis=0, keepdims=True)
             for j in range(EXPERT_TOPK)]
    pos_ref[...] = jnp.concatenate(first, axis=0) + r_ref[...]


def _positions(eidx_t, rank_t, pad_start, tm):
    t = eidx_t.shape[1]
    blk = pl.BlockSpec((EXPERT_TOPK, tm), lambda i: (0, i))
    return pl.pallas_call(
        _pos_kernel,
        grid=(t // tm,),
        in_specs=[blk, blk, pl.BlockSpec((N_EXPERTS, 1), lambda i: (0, 0))],
        out_specs=blk,
        out_shape=jax.ShapeDtypeStruct((EXPERT_TOPK, t), jnp.int32),
        compiler_params=_cparams(("parallel",)),
        name="moe_positions",
    )(eidx_t, rank_t, pad_start.reshape(N_EXPERTS, 1))


def _scatter_kernel(pos_ref, x_ref, rows_in_ref, rows_ref, sem):
    del rows_in_ref
    tm = x_ref.shape[0]

    def row_copy(t, j):
        return pltpu.make_async_copy(x_ref.at[pl.ds(t, 1), :], rows_ref.at[pl.ds(pos_ref[j, t], 1), :], sem)

    def issue(t, carry):
        for j in range(EXPERT_TOPK):
            row_copy(t, j).start()
        return carry

    lax.fori_loop(0, tm, issue, 0)
    for j in range(EXPERT_TOPK):
        pltpu.make_async_copy(x_ref, rows_ref.at[pl.ds(0, tm), :], sem).wait()


def _scatter_rows(pos_t, x1, n_rows, tm):
    t = x1.shape[0]
    return pl.pallas_call(
        _scatter_kernel,
        grid=(t // tm,),
        in_specs=[pl.BlockSpec((EXPERT_TOPK, tm), lambda i: (0, i), memory_space=pltpu.SMEM),
                  pl.BlockSpec((tm, D_MODEL), lambda i: (i, 0)),
                  pl.BlockSpec(memory_space=pl.ANY)],
        out_specs=pl.BlockSpec(memory_space=pl.ANY),
        out_shape=jax.ShapeDtypeStruct((n_rows, D_MODEL), F32),
        scratch_shapes=[pltpu.SemaphoreType.DMA(())],
        input_output_aliases={2: 0},
        compiler_params=_cparams(("arbitrary",)),
        name="moe_scatter",
    )(pos_t, x1, jnp.zeros((n_rows, D_MODEL), F32))


def _expert_kernel(be_ref, nu_ref, x_ref, w1_ref, w3_ref, w2_ref, o_ref, w1b_ref, w3b_ref, w2b_ref):
    i = pl.program_id(0)
    prev = be_ref[jnp.maximum(i - 1, 0)]

    @pl.when((i == 0) | (be_ref[i] != prev))
    def _():
        w1b_ref[...] = w1_ref[...].astype(BF16)
        w3b_ref[...] = w3_ref[...].astype(BF16)
        w2b_ref[...] = w2_ref[...].astype(BF16)

    @pl.when(i < nu_ref[0])
    def _():
        x = x_ref[...].astype(BF16)
        hmid = _silu(_dot(x, w1b_ref[...])) * _dot(x, w3b_ref[...])
        o_ref[...] = _dot(hmid.astype(BF16), w2b_ref[...])

    @pl.when(i >= nu_ref[0])
    def _():
        o_ref[...] = jnp.zeros(o_ref.shape, F32)


def _experts(xg, blk_exp, n_used, w1, w3, w2):
    rows = xg.shape[0]
    n_blocks = rows // MOE_ROWS
    grid_spec = pltpu.PrefetchScalarGridSpec(
        num_scalar_prefetch=2,
        grid=(n_blocks,),
        in_specs=[
            pl.BlockSpec((MOE_ROWS, D_MODEL), lambda i, be, nu: (i, 0)),
            pl.BlockSpec((None, D_MODEL, EXPERT_DIM), lambda i, be, nu: (be[i], 0, 0)),
            pl.BlockSpec((None, D_MODEL, EXPERT_DIM), lambda i, be, nu: (be[i], 0, 0)),
            pl.BlockSpec((None, EXPERT_DIM, D_MODEL), lambda i, be, nu: (be[i], 0, 0)),
        ],
        out_specs=pl.BlockSpec((MOE_ROWS, D_MODEL), lambda i, be, nu: (i, 0)),
        scratch_shapes=[pltpu.VMEM((D_MODEL, EXPERT_DIM), BF16), pltpu.VMEM((D_MODEL, EXPERT_DIM), BF16),
                        pltpu.VMEM((EXPERT_DIM, D_MODEL), BF16)],
    )
    return pl.pallas_call(
        _expert_kernel,
        grid_spec=grid_spec,
        out_shape=jax.ShapeDtypeStruct((rows, D_MODEL), F32),
        compiler_params=_cparams(("arbitrary",)),
        name="experts",
    )(blk_exp, n_used, xg, w1, w3, w2)


def _combine_kernel(pos_ref, y_ref, w_ref, base_ref, g_ref, b_ref, o_ref, gbuf_ref, sem):
    tm = base_ref.shape[0]

    def row_copy(t, j):
        return pltpu.make_async_copy(y_ref.at[pl.ds(pos_ref[j, t], 1), :], gbuf_ref.at[j, pl.ds(t, 1), :], sem)

    def issue(t, carry):
        for j in range(EXPERT_TOPK):
            row_copy(t, j).start()
        return carry

    lax.fori_loop(0, tm, issue, 0)
    for j in range(EXPERT_TOPK):
        pltpu.make_async_copy(y_ref.at[pl.ds(0, tm), :], gbuf_ref.at[j], sem).wait()

    acc = base_ref[...]
    w = w_ref[...]
    for j in range(EXPERT_TOPK):
        acc = acc + w[:, j:j + 1] * gbuf_ref[j]
    o_ref[...] = _layer_norm(acc, g_ref[...], b_ref[...])


def _combine(pos_t, y_rows, wts, base, g, b, tm):
    t = base.shape[0]
    return pl.pallas_call(
        _combine_kernel,
        grid=(t // tm,),
        in_specs=[pl.BlockSpec((EXPERT_TOPK, tm), lambda i: (0, i), memory_space=pltpu.SMEM),
                  pl.BlockSpec(memory_space=pl.ANY),
                  pl.BlockSpec((tm, EXPERT_TOPK), lambda i: (i, 0)),
                  pl.BlockSpec((tm, D_MODEL), lambda i: (i, 0)),
                  pl.BlockSpec((1, D_MODEL), lambda i: (0, 0)),
                  pl.BlockSpec((1, D_MODEL), lambda i: (0, 0))],
        out_specs=pl.BlockSpec((tm, D_MODEL), lambda i: (i, 0)),
        out_shape=jax.ShapeDtypeStruct((t, D_MODEL), F32),
        scratch_shapes=[pltpu.VMEM((EXPERT_TOPK, tm, D_MODEL), F32), pltpu.SemaphoreType.DMA(())],
        compiler_params=_cparams(("arbitrary",)),
        name="combine_ln2",
    )(pos_t, y_rows, wts, base, g.reshape(1, D_MODEL), b.reshape(1, D_MODEL))


def _moe(x1, prm):
    t = x1.shape[0]
    tm = _pick_tile(t, TOKEN_TM)
    eidx_t, wts_t, rank_t, counts, base = _router(x1, prm["w_router_t"], prm["router_bias"], prm["ws1"],
                                                  prm["ws3"], prm["ws2"], tm)
    n_blocks = -(-t * EXPERT_TOPK // MOE_ROWS) + N_EXPERTS
    pad_start, blk_exp, n_used = _block_plan(counts.reshape(N_EXPERTS), n_blocks)
    pos_t = _positions(eidx_t, rank_t, pad_start, tm)
    x_rows = _scatter_rows(pos_t, x1, n_blocks * MOE_ROWS, tm)
    y_rows = _experts(x_rows, blk_exp, n_used, prm["w1"], prm["w3"], prm["w2"])
    return _combine(pos_t, y_rows, wts_t.T, base, prm["ln2_g"], prm["ln2_b"], _pick_tile(t, COMBINE_TM))


def _mixer(x, pos0, conv_buf, s0, k_past, v_past, ik_past, prm):
    b, l, _ = x.shape
    lp = _round_up(l, DSA_Q_TILE)
    xp = x if lp == l else jnp.pad(x, ((0, 0), (0, lp - l), (0, 0)))
    h3 = _project(xp.reshape(b * lp, D_MODEL), prm["w_in"]).reshape(b, lp, H_COLS)

    ya, s_new, buf_new = _gdn(h3, l, conv_buf, s0, prm["conv_w"], prm["a_log"], prm["dt_bias"], prm["gdn_norm_g"])

    pos = pos0 + jnp.arange(lp, dtype=jnp.int32)
    qt, qit, wit, k_new, k_bf, ki_new, ki_bf, v_new, vt = _prep(h3, pos, prm["idx_k_ln_g"], prm["idx_k_ln_b"])
    if k_past is None:
        l_all = l
        k_all, ki_all, vt_all = k_bf, ki_bf, vt
    else:
        past = k_past.shape[1]
        l_all = past + l
        k_all = jnp.concatenate([k_past.reshape(b, past, ATT_KV).astype(BF16), k_bf[:, :l]], 1)
        ki_all = jnp.concatenate([ik_past.astype(BF16), ki_bf[:, :l]], 1)
        vt_all = jnp.concatenate([jnp.swapaxes(v_past.reshape(b, past, ATT_KV), 1, 2).astype(BF16), vt[:, :, :l]], 2)
    lk = _round_up(l_all, DSA_KEY_BLOCK)
    if lk != l_all:
        k_all = jnp.pad(k_all, ((0, 0), (0, lk - l_all), (0, 0)))
        ki_all = jnp.pad(ki_all, ((0, 0), (0, lk - l_all), (0, 0)))
        vt_all = jnp.pad(vt_all, ((0, 0), (0, 0), (0, lk - l_all)))
    yb = _dsa(qt, qit, wit, ki_all, k_all, vt_all, l_all, pos0)

    x1 = _merge(ya, yb, h3, x, prm["w_o_gdn"], prm["w_o_dsa"], prm["w_out"], prm["ln1_g"], prm["ln1_b"])
    state = (k_new[:, :l].reshape(b, l, ATT_KV_HEADS, HEAD_DIM), v_new[:, :l].reshape(b, l, ATT_KV_HEADS, HEAD_DIM),
             ki_new[:, :l], s_new, buf_new)
    return x1, state


def kernel(x_prompt, x_sample, cache_k, cache_v, cache_idx_k, state_gdn, state_conv, w_in, conv_w, a_log, dt_bias, gdn_norm_g, w_o_gdn, idx_k_ln_g, idx_k_ln_b, w_o_dsa, w_out, ln1_g, ln1_b, w_router, router_bias, w1, w3, w2, ws1, ws3, ws2, ln2_g, ln2_b):
    assert w_in.shape[0] == DEPTH == 1
    bp, lp_, _ = x_prompt.shape
    bs, ls_, _ = x_sample.shape
    past = cache_k.shape[2]
    prm = dict(
        w_in=_repack_w_in(w_in[0]).astype(BF16), conv_w=conv_w[0], a_log=a_log[0], dt_bias=dt_bias[0],
        gdn_norm_g=gdn_norm_g[0], w_o_gdn=w_o_gdn[0].astype(BF16), idx_k_ln_g=idx_k_ln_g[0],
        idx_k_ln_b=idx_k_ln_b[0], w_o_dsa=w_o_dsa[0].astype(BF16), w_out=w_out[0].astype(BF16),
        ln1_g=ln1_g[0], ln1_b=ln1_b[0], w_router_t=w_router[0].T, router_bias=router_bias[0],
        w1=w1[0], w3=w3[0], w2=w2[0], ws1=ws1[0].astype(BF16), ws3=ws3[0].astype(BF16),
        ws2=ws2[0].astype(BF16), ln2_g=ln2_g[0], ln2_b=ln2_b[0])
    conv0 = jnp.zeros((bp, CONV_W - 1, CONV_CH), F32)
    s0 = jnp.zeros((bp, GDN_HEADS, GDN_DK, GDN_DV), F32)
    x1p, sp = _mixer(x_prompt, 0, conv0, s0, None, None, None, prm)
    x1s, ss = _mixer(x_sample, past, state_conv[0], state_gdn[0], cache_k[0], cache_v[0], cache_idx_k[0], prm)
    tp, ts = bp * lp_, bs * ls_
    y = _moe(jnp.concatenate([x1p.reshape(tp, D_MODEL), x1s.reshape(ts, D_MODEL)], 0), prm)
    yp = y[:tp].reshape(bp, lp_, D_MODEL)
    ys = y[tp:].reshape(bs, ls_, D_MODEL)
    return (yp, ys) + tuple(a[None] for a in sp) + tuple(a[None] for a in ss)
```

```python
import functools

import jax
import jax.numpy as jnp
import numpy as np
from jax import lax
from jax.experimental import pallas as pl
from jax.experimental.pallas import tpu as pltpu

F32 = jnp.float32
BF16 = jnp.bfloat16

D_MODEL = 1024
CHUNK = 64
GDN_HEADS = 8
GDN_DK = 128
GDN_DV = 128
CONV_W = 4
ATT_HEADS = 8
ATT_KV_HEADS = 2
HEAD_DIM = 128
IDX_HEADS = 16
IDX_DIM = 64
TOPK_MAX = 256
ROPE_THETA = 500000.0
N_EXPERTS = 256
EXPERT_TOPK = 8
N_GROUPS = 8
TOPK_GROUPS = 4
EXPERT_DIM = 256
SHARED_DIM = 256
ROUTED_SCALE = 2.5
DEPTH = 1
DN_ALPHA = (2.0 * DEPTH) ** 0.25
LN_EPS = 1e-5
RMS_EPS = 1e-6

GDN_QK = GDN_HEADS * GDN_DK
GDN_V = GDN_HEADS * GDN_DV
CONV_CH = 2 * GDN_QK + GDN_V
ATT_Q = ATT_HEADS * HEAD_DIM
ATT_KV = ATT_KV_HEADS * HEAD_DIM
IDX_Q = IDX_HEADS * IDX_DIM
REP = ATT_HEADS // ATT_KV_HEADS

LANES = 128
SUBLANES = 8
VMEM_LIMIT = 56 * 1024 * 1024

PROJ_TM = 1024
PROJ_TN = 1024
PREP_TM = 512
DSA_Q_TILE = 128
DSA_KEY_BLOCK = 512
TOKEN_TM = 256
COMBINE_TM = 128
GDN_BATCH_TILE = 2
MOE_ROWS = 256

COL_QKV = 0
COL_Z = COL_QKV + CONV_CH
COL_Q = COL_Z + GDN_V
COL_QI = COL_Q + ATT_Q
COL_GA = COL_QI + IDX_Q
COL_GB = COL_GA + D_MODEL
COL_K = COL_GB + D_MODEL
COL_V = COL_K + ATT_KV
COL_SMALL = COL_V + ATT_KV
SMALL_W = 512
SM_A = IDX_DIM
SM_B = SM_A + GDN_HEADS
SM_WI = SM_B + GDN_HEADS
H_COLS = COL_SMALL + SMALL_W


def _cparams(sem):
    return pltpu.CompilerParams(dimension_semantics=sem, vmem_limit_bytes=VMEM_LIMIT)


def _dot(a, b):
    return jnp.dot(a, b, preferred_element_type=F32)


def _dot_bf(a, b):
    return jnp.dot(a.astype(BF16), b.astype(BF16), preferred_element_type=F32)


def _dot_hi(a, b):
    return jnp.dot(a, b, precision=lax.Precision.HIGHEST, preferred_element_type=F32)


def _dot_nt_hi(a, b):
    return lax.dot_general(a, b, (((1,), (1,)), ((), ())), precision=lax.Precision.HIGHEST,
                           preferred_element_type=F32)


def _dot_nt_bf(a, b):
    return lax.dot_general(a.astype(BF16), b.astype(BF16), (((1,), (1,)), ((), ())),
                           preferred_element_type=F32)


def _dot_tn_bf(a, b):
    return lax.dot_general(a.astype(BF16), b.astype(BF16), (((0,), (0,)), ((), ())),
                           preferred_element_type=F32)


def _sigmoid(x):
    return 1.0 / (1.0 + jnp.exp(-x))


def _silu(x):
    return x * _sigmoid(x)


def _round_up(n, m):
    return -(-n // m) * m


def _pick_tile(n, pref):
    t = min(n, pref)
    assert n % t == 0
    return t


def _repack_w_in(w_in):
    sizes = (CONV_CH, GDN_V, GDN_HEADS, GDN_HEADS, ATT_Q, ATT_KV, ATT_KV, IDX_Q, IDX_DIM, IDX_HEADS,
             D_MODEL, D_MODEL)
    offs = np.concatenate([[0], np.cumsum(sizes)])
    (p_qkv, p_z, p_a, p_b, p_q, p_k, p_v, p_qi, p_ki, p_wi, p_ga, p_gb) = [
        w_in[:, offs[i]:offs[i + 1]] for i in range(len(sizes))]
    pad = jnp.zeros((w_in.shape[0], SMALL_W - (SM_WI + IDX_HEADS)), w_in.dtype)
    return jnp.concatenate([p_qkv, p_z, p_q, p_qi, p_ga, p_gb, p_k, p_v, p_ki, p_a, p_b, p_wi, pad], axis=1)


def _proj_kernel(x_ref, w_ref, o_ref):
    o_ref[...] = _dot(x_ref[...].astype(BF16), w_ref[...])


def _project(x2d, w_bf):
    t, d = x2d.shape
    n = w_bf.shape[1]
    tm, tn = _pick_tile(t, PROJ_TM), _pick_tile(n, PROJ_TN)
    return pl.pallas_call(
        _proj_kernel,
        grid=(t // tm, n // tn),
        in_specs=[pl.BlockSpec((tm, d), lambda i, j: (i, 0)),
                  pl.BlockSpec((d, tn), lambda i, j: (0, j))],
        out_specs=pl.BlockSpec((tm, tn), lambda i, j: (i, j)),
        out_shape=jax.ShapeDtypeStruct((t, n), F32),
        compiler_params=_cparams(("parallel", "arbitrary")),
        name="in_proj",
    )(x2d, w_bf)


TAIL_ROWS = SUBLANES
INV_BASE = 8


def _split(a):
    hi = a.astype(BF16)
    return hi, (a - hi.astype(F32)).astype(BF16)


def _dot3(a, b):
    return _dot(a[0], b[0]) + (_dot(a[0], b[1]) + _dot(a[1], b[0]))


def _unit_lower_inverse(ms, c):
    ri = lax.broadcasted_iota(jnp.int32, (c, c), 0)
    ci = lax.broadcasted_iota(jnp.int32, (c, c), 1)
    eye = (ri == ci).astype(F32)
    blk = INV_BASE
    same = (ri // blk) == (ci // blk)
    ns = [jnp.where(same, -m, 0.0) for m in ms]
    xs = [eye + n for n in ns]
    span = 1
    while span * 2 < blk:
        nsp = [_split(n) for n in ns]
        ns = [_dot3(n, n) for n in nsp]
        nsp = [_split(n) for n in ns]
        xs = [x + _dot3(_split(x), n) for x, n in zip(xs, nsp)]
        span *= 2
    while blk < c:
        nxt = blk * 2
        emask = ((ri // nxt) == (ci // nxt)) & ((ri // blk) != (ci // blk))
        xsp = [_split(x) for x in xs]
        ts = [_dot3(x, _split(jnp.where(emask, m, 0.0))) for x, m in zip(xsp, ms)]
        xs = [x - _dot3(_split(t), xp) for x, t, xp in zip(xs, ts, xsp)]
        blk = nxt
    return xs


def _gdn_kernel(hq_ref, z_ref, sm_ref, buf_ref, s0_ref, cw_ref, alog_ref, dtb_ref, ng_ref,
                y_ref, snew_ref, bufnew_ref, xp_ref, s_ref, *, chunk):
    c_idx = pl.program_id(1)
    n_c = pl.num_programs(1)
    C = chunk
    keep = CONV_W - 1

    @pl.when(c_idx == 0)
    def _():
        xp_ref[:, TAIL_ROWS - keep:TAIL_ROWS, :] = buf_ref[...]
        s_ref[...] = s0_ref[...]

    ri = lax.broadcasted_iota(jnp.int32, (C, C), 0)
    ci = lax.broadcasted_iota(jnp.int32, (C, C), 1)
    incl = ri >= ci
    strict = ri > ci

    pairs = [(bi, h) for bi in range(hq_ref.shape[0]) for h in range(GDN_HEADS)]
    qs, ks, vs, gcs, bhs, egs, decays = [], [], [], [], [], [], []
    for bi in range(hq_ref.shape[0]):
        xp_ref[bi, TAIL_ROWS:TAIL_ROWS + C, :] = hq_ref[bi]

        acc = cw_ref[0:1, :] * xp_ref[bi, TAIL_ROWS - keep:TAIL_ROWS - keep + C, :]
        for j in range(1, CONV_W):
            acc = acc + cw_ref[j:j + 1, :] * xp_ref[bi, TAIL_ROWS - keep + j:TAIL_ROWS - keep + j + C, :]
        conv = _silu(acc)

        @pl.when(c_idx == n_c - 1)
        def _():
            bufnew_ref[bi] = xp_ref[bi, TAIL_ROWS + C - keep:TAIL_ROWS + C, :]

        xp_ref[bi, TAIL_ROWS - keep:TAIL_ROWS, :] = xp_ref[bi, TAIL_ROWS + C - keep:TAIL_ROWS + C, :]

        sm = sm_ref[bi, :, 0:LANES]
        xg = sm + dtb_ref[...]
        softplus = jnp.maximum(xg, 0.0) + jnp.log(1.0 + jnp.exp(-jnp.abs(xg)))
        g = -jnp.exp(alog_ref[...]) * softplus
        beta = _sigmoid(sm)
        gc = _dot_hi(incl.astype(F32), g)
        gc_t = gc.T

        for h in range(GDN_HEADS):
            q = conv[:, h * GDN_DK:(h + 1) * GDN_DK]
            k = conv[:, GDN_QK + h * GDN_DK:GDN_QK + (h + 1) * GDN_DK]
            qs.append(q * lax.rsqrt(jnp.sum(q * q, -1, keepdims=True) + 1e-6) * (GDN_DK ** -0.5))
            ks.append(k * lax.rsqrt(jnp.sum(k * k, -1, keepdims=True) + 1e-6))
            vs.append(conv[:, 2 * GDN_QK + h * GDN_DV:2 * GDN_QK + (h + 1) * GDN_DV])
            gch = gc[:, SM_A + h:SM_A + h + 1]
            gcs.append(gch)
            bhs.append(beta[:, SM_B + h:SM_B + h + 1])
            egs.append(jnp.exp(gch))
            decays.append(jnp.exp(jnp.where(incl, gch - gc_t[SM_A + h:SM_A + h + 1, :], -jnp.inf)))
    kbs = [k * bh for k, bh in zip(ks, bhs)]
    ms = [jnp.where(strict, _dot_nt_bf(kb, k) * d, 0.0) for kb, k, d in zip(kbs, ks, decays)]
    attns = [_dot_nt_bf(q, k) * d for q, k, d in zip(qs, ks, decays)]
    tinvs = _unit_lower_inverse(ms, C)
    sols = [_dot_bf(t, jnp.concatenate([v * bh, kb * eg], axis=-1))
            for t, v, bh, kb, eg in zip(tinvs, vs, bhs, kbs, egs)]
    ss = [s_ref[bi, h] for bi, h in pairs]
    v_news = [sol[:, :GDN_DV] - _dot_bf(sol[:, GDN_DV:], s) for sol, s in zip(sols, ss)]
    os_ = [_dot_bf(q * eg, s) + _dot_bf(a, vn) for q, eg, s, a, vn in zip(qs, egs, ss, attns, v_news)]
    for n, (bi, h) in enumerate(pairs):
        glast = gcs[n][C - 1:C, :]
        s_ref[bi, h] = ss[n] * jnp.exp(glast) + _dot_tn_bf(ks[n] * jnp.exp(glast - gcs[n]), v_news[n])
    for n, (bi, h) in enumerate(pairs):
        o = os_[n]
        o = o * lax.rsqrt(jnp.mean(o * o, -1, keepdims=True) + RMS_EPS) * ng_ref[...]
        zh = z_ref[bi, :, h * GDN_DV:(h + 1) * GDN_DV]
        y_ref[bi, :, h * GDN_DV:(h + 1) * GDN_DV] = o * _silu(zh)

    @pl.when(c_idx == n_c - 1)
    def _():
        snew_ref[...] = s_ref[...]


def _gdn(h3, l, conv_buf, s0, conv_w, a_log, dt_bias, norm_g):
    b = h3.shape[0]
    chunk = min(CHUNK, l)
    assert l % chunk == 0 and chunk % SUBLANES == 0 and chunk >= CONV_W - 1
    kern = functools.partial(_gdn_kernel, chunk=chunk)
    keep = CONV_W - 1
    lane_row = lambda vec, at: jnp.zeros((1, LANES), F32).at[0, at:at + vec.shape[0]].set(vec)
    gb = _pick_tile(b, GDN_BATCH_TILE)
    return pl.pallas_call(
        kern,
        grid=(b // gb, l // chunk),
        in_specs=[
            pl.BlockSpec((gb, chunk, CONV_CH), lambda i, c: (i, c, COL_QKV // CONV_CH)),
            pl.BlockSpec((gb, chunk, GDN_V), lambda i, c: (i, c, COL_Z // GDN_V)),
            pl.BlockSpec((gb, chunk, SMALL_W), lambda i, c: (i, c, COL_SMALL // SMALL_W)),
            pl.BlockSpec((gb, keep, CONV_CH), lambda i, c: (i, 0, 0)),
            pl.BlockSpec((gb, GDN_HEADS, GDN_DK, GDN_DV), lambda i, c: (i, 0, 0, 0)),
            pl.BlockSpec((CONV_W, CONV_CH), lambda i, c: (0, 0)),
            pl.BlockSpec((1, LANES), lambda i, c: (0, 0)),
            pl.BlockSpec((1, LANES), lambda i, c: (0, 0)),
            pl.BlockSpec((1, GDN_DV), lambda i, c: (0, 0)),
        ],
        out_specs=[
            pl.BlockSpec((gb, chunk, GDN_V), lambda i, c: (i, c, 0)),
            pl.BlockSpec((gb, GDN_HEADS, GDN_DK, GDN_DV), lambda i, c: (i, 0, 0, 0)),
            pl.BlockSpec((gb, keep, CONV_CH), lambda i, c: (i, 0, 0)),
        ],
        out_shape=[
            jax.ShapeDtypeStruct((b, l, GDN_V), F32),
            jax.ShapeDtypeStruct((b, GDN_HEADS, GDN_DK, GDN_DV), F32),
            jax.ShapeDtypeStruct((b, keep, CONV_CH), F32),
        ],
        scratch_shapes=[
            pltpu.VMEM((gb, TAIL_ROWS + chunk, CONV_CH), F32),
            pltpu.VMEM((gb, GDN_HEADS, GDN_DK, GDN_DV), F32),
        ],
        compiler_params=_cparams(("parallel", "arbitrary")),
        name="gdn",
    )(h3, h3, h3, conv_buf, s0, conv_w, lane_row(a_log, SM_A), lane_row(dt_bias, SM_A),
      norm_g.reshape(1, GDN_DV))


def _rope_tables(pos, rot, period):
    half = rot // 2
    inv_freq = ROPE_THETA ** (-(2.0 / rot) * jnp.arange(half, dtype=F32))
    ang = pos.astype(F32)[:, None] * inv_freq[None, :]
    cos, sin = jnp.cos(ang), jnp.sin(ang)
    n = pos.shape[0]
    rest = period - rot
    c = jnp.concatenate([cos, cos, jnp.ones((n, rest), F32)], -1)
    sa = jnp.concatenate([-sin, jnp.zeros((n, half + rest), F32)], -1)
    sb = jnp.concatenate([jnp.zeros((n, half), F32), sin, jnp.zeros((n, rest), F32)], -1)
    reps = LANES // period
    return jnp.stack([jnp.tile(c, (1, reps)), jnp.tile(sa, (1, reps)), jnp.tile(sb, (1, reps))], 0)


Q_SCALE_LOG2 = HEAD_DIM ** -0.5 * float(np.log2(np.e))


def _rope128(x, tab_ref, half):
    return (x * tab_ref[0] + pltpu.roll(x, LANES - half, 1) * tab_ref[1]
            + pltpu.roll(x, half, 1) * tab_ref[2])


def _prep_kernel(q_ref, qi_ref, k_ref, v_ref, sm_ref, tq_ref, ti_ref, lng_ref, lnb_ref,
                 qt_ref, qit_ref, wit_ref, ko_ref, kb_ref, kio_ref, kib_ref, vo_ref, vt_ref):
    hq = HEAD_DIM // 8
    hi = IDX_DIM // 8
    for h in range(ATT_HEADS):
        x = _rope128(q_ref[:, h * HEAD_DIM:(h + 1) * HEAD_DIM], tq_ref, hq) * Q_SCALE_LOG2
        qt_ref[h * HEAD_DIM:(h + 1) * HEAD_DIM, :] = x.T.astype(BF16)
    for h in range(ATT_KV_HEADS):
        x = _rope128(k_ref[:, h * HEAD_DIM:(h + 1) * HEAD_DIM], tq_ref, hq)
        ko_ref[:, h * HEAD_DIM:(h + 1) * HEAD_DIM] = x
        kb_ref[:, h * HEAD_DIM:(h + 1) * HEAD_DIM] = x.astype(BF16)
        v = v_ref[:, h * HEAD_DIM:(h + 1) * HEAD_DIM]
        vo_ref[:, h * HEAD_DIM:(h + 1) * HEAD_DIM] = v
        vt_ref[h * HEAD_DIM:(h + 1) * HEAD_DIM, :] = v.T.astype(BF16)
    for c in range(IDX_Q // LANES):
        x = _rope128(qi_ref[:, c * LANES:(c + 1) * LANES], ti_ref, hi)
        qit_ref[c * LANES:(c + 1) * LANES, :] = x.T.astype(BF16)
    sm = sm_ref[:, 0:LANES]
    lane = lax.broadcasted_iota(jnp.int32, sm.shape, 1)
    is_ki = lane < IDX_DIM
    mu = jnp.sum(jnp.where(is_ki, sm, 0.0), -1, keepdims=True) * (1.0 / IDX_DIM)
    xc = jnp.where(is_ki, sm - mu, 0.0)
    var = jnp.sum(xc * xc, -1, keepdims=True) * (1.0 / IDX_DIM)
    ki = xc * lax.rsqrt(var + LN_EPS) * lng_ref[...] + lnb_ref[...]
    ki = _rope128(ki, ti_ref, hi)[:, 0:IDX_DIM]
    kio_ref[...] = ki
    kib_ref[...] = ki.astype(BF16)
    wit_ref[...] = sm.T[SM_WI:SM_WI + IDX_HEADS, :] * (IDX_HEADS ** -0.5 * IDX_DIM ** -0.5)


def _prep(h3, pos, ln_g, ln_b):
    b, l, _ = h3.shape
    tm = _pick_tile(l, PREP_TM)
    tab_q = _rope_tables(pos, HEAD_DIM // 4, HEAD_DIM)
    tab_i = _rope_tables(pos, IDX_DIM // 4, IDX_DIM)
    lng = jnp.concatenate([ln_g, jnp.zeros((LANES - IDX_DIM,), F32)]).reshape(1, LANES)
    lnb = jnp.concatenate([ln_b, jnp.zeros((LANES - IDX_DIM,), F32)]).reshape(1, LANES)
    rows = lambda w, col: pl.BlockSpec((None, tm, w), lambda i, t: (i, t, col))
    cols = lambda w: pl.BlockSpec((None, w, tm), lambda i, t: (i, 0, t))
    return pl.pallas_call(
        _prep_kernel,
        grid=(b, l // tm),
        in_specs=[
            rows(ATT_Q, COL_Q // ATT_Q), rows(IDX_Q, COL_QI // IDX_Q), rows(ATT_KV, COL_K // ATT_KV),
            rows(ATT_KV, COL_V // ATT_KV), rows(SMALL_W, COL_SMALL // SMALL_W),
            pl.BlockSpec((3, tm, LANES), lambda i, t: (0, t, 0)),
            pl.BlockSpec((3, tm, LANES), lambda i, t: (0, t, 0)),
            pl.BlockSpec((1, LANES), lambda i, t: (0, 0)),
            pl.BlockSpec((1, LANES), lambda i, t: (0, 0)),
        ],
        out_specs=[cols(ATT_Q), cols(IDX_Q), cols(IDX_HEADS), rows(ATT_KV, 0), rows(ATT_KV, 0),
                   rows(IDX_DIM, 0), rows(IDX_DIM, 0), rows(ATT_KV, 0), cols(ATT_KV)],
        out_shape=[
            jax.ShapeDtypeStruct((b, ATT_Q, l), BF16),
            jax.ShapeDtypeStruct((b, IDX_Q, l), BF16),
            jax.ShapeDtypeStruct((b, IDX_HEADS, l), F32),
            jax.ShapeDtypeStruct((b, l, ATT_KV), F32),
            jax.ShapeDtypeStruct((b, l, ATT_KV), BF16),
            jax.ShapeDtypeStruct((b, l, IDX_DIM), F32),
            jax.ShapeDtypeStruct((b, l, IDX_DIM), BF16),
            jax.ShapeDtypeStruct((b, l, ATT_KV), F32),
            jax.ShapeDtypeStruct((b, ATT_KV, l), BF16),
        ],
        compiler_params=_cparams(("parallel", "parallel")),
        name="dsa_prep",
    )(h3, h3, h3, h3, h3, tab_q, tab_i, lng, lnb)


INT_MIN = -2 ** 31
INT_MAX = 2 ** 31 - 1
NEG_INF_KEY = INT_MIN + 0x7FFFFF
NEG_BIG = -1e30
HEAD_PAIRS = IDX_HEADS // 2
COUNT_CHAINS = 8
COUNT16_ROWS = 64
HALF16 = 1 << 15
DENOM_ROWS = 16


def _dsa_kernel(qt_ref, qit_ref, wit_ref, ki_ref, k_ref, vt_ref, o_ref,
                key_ref, khi_ref, klo_ref, qs_ref, qip_ref, m_ref, acc_ref, cm_ref,
                *, tq, kb, l_true, pos0, topk, idx_bits):
    q0 = pos0 + pl.program_id(1) * tq
    qpos = q0 + lax.broadcasted_iota(jnp.int32, (1, tq), 1)
    lim = jnp.minimum((qpos // CHUNK + 1) * CHUNK, l_true)
    lim_max = jnp.minimum(((q0 + tq - 1) // CHUNK + 1) * CHUNK, l_true)
    nkb = (lim_max + kb - 1) // kb
    key_iota = lax.broadcasted_iota(jnp.int32, (kb, tq), 0)
    wit = wit_ref[...]

    for p in range(HEAD_PAIRS):
        for u in range(2):
            hh = 2 * p + u
            qip_ref[p, :, u * tq:(u + 1) * tq] = qit_ref[hh * IDX_DIM:(hh + 1) * IDX_DIM, :]
    for g in range(ATT_KV_HEADS):
        for r in range(REP):
            hh = g * REP + r
            qs_ref[g, :, r * tq:(r + 1) * tq] = qt_ref[hh * HEAD_DIM:(hh + 1) * HEAD_DIM, :]

    def score_blk(j, carry):
        off = pl.multiple_of(j * kb, kb)
        ki = ki_ref[pl.ds(off, kb), :]
        acc = jnp.zeros((kb, tq), F32)
        for p in range(HEAD_PAIRS):
            s2 = _dot(ki, qip_ref[p])
            acc = (acc + wit[2 * p:2 * p + 1, :] * jnp.maximum(s2[:, :tq], 0.0)
                   + wit[2 * p + 1:2 * p + 2, :] * jnp.maximum(s2[:, tq:], 0.0))
        score = jnp.where(off + key_iota < lim, acc, -jnp.inf)
        bits = pltpu.bitcast(score, jnp.int32)
        key = bits ^ ((bits >> 31) & INT_MAX)
        key_ref[pl.ds(off, kb), :] = key
        khi_ref[pl.ds(off, kb), :] = (key >> 16).astype(jnp.int16)
        return carry

    lax.fori_loop(0, nkb, score_blk, 0)

    def count(pred_fn):
        def blk(j, c):
            off = pl.multiple_of(j * kb, kb)
            part = jnp.where(pred_fn(key_ref[pl.ds(off, kb), :], off), 1.0, 0.0)
            return c + jnp.sum(part.reshape(kb // (COUNT_CHAINS * SUBLANES), COUNT_CHAINS * SUBLANES, tq), axis=0)
        c = lax.fori_loop(0, nkb, blk, jnp.zeros((COUNT_CHAINS * SUBLANES, tq), F32))
        return jnp.sum(c, axis=0, keepdims=True)

    def count16(ref, pred_fn):
        one, zero = jnp.ones((), jnp.int16), jnp.zeros((), jnp.int16)

        def blk(j, c):
            off = pl.multiple_of(j * kb, kb)
            part = jnp.where(pred_fn(ref[pl.ds(off, kb), :]), one, zero)
            for s in range(kb // COUNT16_ROWS):
                c = c + part[s * COUNT16_ROWS:(s + 1) * COUNT16_ROWS]
            return c
        c = lax.fori_loop(0, nkb, blk, jnp.zeros((COUNT16_ROWS, tq), jnp.int16))
        return jnp.sum(c.astype(jnp.int32), axis=0, keepdims=True)

    def search16(ref, want):
        def bit_step(t, cur):
            cand_u = cur | lax.shift_left(jnp.int32(1), 15 - t)
            cand = (cand_u - HALF16).astype(jnp.int16)
            cnt = count16(ref, lambda kk: kk >= cand)
            return jnp.where(cnt >= want, cand_u, cur)
        return lax.fori_loop(0, 16, bit_step, jnp.zeros((1, tq), jnp.int32))

    t_hi = search16(khi_ref, topk) - HALF16
    t_hi16 = t_hi.astype(jnp.int16)
    n_hi_gt = count16(khi_ref, lambda kk: kk > t_hi16)

    def low_blk(j, carry):
        off = pl.multiple_of(j * kb, kb)
        kk = key_ref[pl.ds(off, kb), :]
        low = jnp.where((kk >> 16) == t_hi, (kk & 0xFFFF) - HALF16, -HALF16)
        klo_ref[pl.ds(off, kb), :] = low.astype(jnp.int16)
        return carry

    lax.fori_loop(0, nkb, low_blk, 0)
    t_lo = search16(klo_ref, topk - n_hi_gt)
    thr = t_hi * (2 * HALF16) + t_lo
    n_gt = count(lambda kk, off: kk > thr)
    n_ge = count(lambda kk, off: kk >= thr)
    need = topk - n_gt

    cm_ref[...] = jnp.full((1, tq), INT_MAX, jnp.int32)

    @pl.when(jnp.max(jnp.where((n_ge > topk) & (thr > NEG_INF_KEY), 1, 0)) > 0)
    def _():
        def idx_step(t, cm):
            cand = cm | lax.shift_left(jnp.int32(1), idx_bits - 1 - t)
            before = count(lambda kk, off: (kk == thr) & (off + key_iota < cand))
            return jnp.where(before < need, cand, cm)
        cm_ref[...] = lax.fori_loop(0, idx_bits, idx_step, jnp.zeros((1, tq), jnp.int32))

    cm = cm_ref[...]

    m_ref[...] = jnp.full(m_ref.shape, NEG_BIG, F32)
    acc_ref[...] = jnp.zeros(acc_ref.shape, F32)
    groups = range(ATT_KV_HEADS)
    ones_rows = jnp.ones((DENOM_ROWS, kb), BF16)

    def attn_blk(j):
        off = pl.multiple_of(j * kb, kb)
        kk = key_ref[pl.ds(off, kb), :]
        kpos = off + key_iota
        sel = ((kk > thr) | ((kk == thr) & (kpos <= cm))) & (kpos < lim)
        bias = jnp.where(sel, 0.0, NEG_BIG)
        bias = jnp.concatenate([bias] * REP, axis=1)
        logits = [_dot(k_ref[pl.ds(off, kb), g * HEAD_DIM:(g + 1) * HEAD_DIM], qs_ref[g]) + bias for g in groups]
        m_old = [m_ref[g] for g in groups]
        m_new = [jnp.maximum(m_old[g], jnp.max(logits[g], axis=0, keepdims=True)) for g in groups]
        ps = [jnp.exp2(logits[g] - m_new[g]).astype(BF16) for g in groups]
        alphas = [jnp.exp2(m_old[g] - m_new[g]) for g in groups]
        pvs = [_dot(jnp.concatenate([vt_ref[g * HEAD_DIM:(g + 1) * HEAD_DIM, pl.ds(off, kb)], ones_rows], axis=0),
                    ps[g]) for g in groups]
        for g in groups:
            acc_ref[g] = alphas[g] * acc_ref[g] + pvs[g]
            m_ref[g] = m_new[g]

    def attn_pair(jp, carry):
        attn_blk(2 * jp)
        attn_blk(2 * jp + 1)
        return carry

    lax.fori_loop(0, nkb // 2, attn_pair, 0)

    @pl.when(nkb % 2 == 1)
    def _():
        attn_blk(nkb - 1)

    for g in groups:
        o_t = acc_ref[g, 0:HEAD_DIM, :] / acc_ref[g, HEAD_DIM:HEAD_DIM + 1, :]
        for r in range(REP):
            hh = g * REP + r
            o_ref[:, hh * HEAD_DIM:(hh + 1) * HEAD_DIM] = o_t[:, r * tq:(r + 1) * tq].T


def _dsa(qt, qit, wit, ki, k, vt, l_true, pos0):
    b, _, t = qt.shape
    lk = k.shape[1]
    tq, kb = DSA_Q_TILE, DSA_KEY_BLOCK
    assert t % tq == 0 and lk % kb == 0 and tq == HEAD_DIM
    assert lk // COUNT16_ROWS < HALF16 and kb % COUNT16_ROWS == 0
    topk = min(TOPK_MAX, l_true // 4)
    kern = functools.partial(_dsa_kernel, tq=tq, kb=kb, l_true=l_true, pos0=pos0, topk=topk,
                             idx_bits=int(lk).bit_length())
    return pl.pallas_call(
        kern,
        grid=(b, t // tq),
        in_specs=[
            pl.BlockSpec((None, ATT_Q, tq), lambda i, t_: (i, 0, t_)),
            pl.BlockSpec((None, IDX_Q, tq), lambda i, t_: (i, 0, t_)),
            pl.BlockSpec((None, IDX_HEADS, tq), lambda i, t_: (i, 0, t_)),
            pl.BlockSpec((None, lk, IDX_DIM), lambda i, t_: (i, 0, 0)),
            pl.BlockSpec((None, lk, ATT_KV), lambda i, t_: (i, 0, 0)),
            pl.BlockSpec((None, ATT_KV, lk), lambda i, t_: (i, 0, 0)),
        ],
        out_specs=pl.BlockSpec((None, tq, ATT_Q), lambda i, t_: (i, t_, 0)),
        out_shape=jax.ShapeDtypeStruct((b, t, ATT_Q), F32),
        scratch_shapes=[
            pltpu.VMEM((lk, tq), jnp.int32),
            pltpu.VMEM((lk, tq), jnp.int16),
            pltpu.VMEM((lk, tq), jnp.int16),
            pltpu.VMEM((ATT_KV_HEADS, HEAD_DIM, REP * tq), BF16),
            pltpu.VMEM((HEAD_PAIRS, IDX_DIM, 2 * tq), BF16),
            pltpu.VMEM((ATT_KV_HEADS, 1, REP * tq), F32),
            pltpu.VMEM((ATT_KV_HEADS, HEAD_DIM + DENOM_ROWS, REP * tq), F32),
            pltpu.VMEM((1, tq), jnp.int32),
        ],
        compiler_params=_cparams(("parallel", "arbitrary")),
        name="dsa",
    )(qt, qit, wit, ki, k, vt)


def _layer_norm(x, g, b):
    mu = jnp.mean(x, -1, keepdims=True)
    xc = x - mu
    var = jnp.mean(xc * xc, -1, keepdims=True)
    return xc * lax.rsqrt(var + LN_EPS) * g + b


def _merge_kernel(ya_ref, yb_ref, ga_ref, gb_ref, x_ref, wa_ref, wb_ref, wo_ref, g_ref, b_ref, o_ref):
    pa = _dot(ya_ref[...].astype(BF16), wa_ref[...])
    pb = _dot(yb_ref[...].astype(BF16), wb_ref[...])
    merged = _sigmoid(ga_ref[...]) * pa + _sigmoid(gb_ref[...]) * pb
    y = DN_ALPHA * x_ref[...] + _dot(merged.astype(BF16), wo_ref[...])
    o_ref[...] = _layer_norm(y, g_ref[...], b_ref[...])


def _merge(ya, yb, h3, x, wa, wb, wo, g, b):
    bsz, l, _ = x.shape
    tm = _pick_tile(l, TOKEN_TM)
    row = lambda c: pl.BlockSpec((None, tm, D_MODEL), lambda i, t: (i, t, c))
    full = lambda shp: pl.BlockSpec(shp, lambda i, t: (0, 0))
    return pl.pallas_call(
        _merge_kernel,
        grid=(bsz, l // tm),
        in_specs=[row(0), row(0), row(COL_GA // D_MODEL), row(COL_GB // D_MODEL), row(0),
                  full((GDN_V, D_MODEL)), full((ATT_Q, D_MODEL)), full((D_MODEL, D_MODEL)),
                  full((1, D_MODEL)), full((1, D_MODEL))],
        out_specs=row(0),
        out_shape=jax.ShapeDtypeStruct((bsz, l, D_MODEL), F32),
        compiler_params=_cparams(("parallel", "parallel")),
        name="merge_ln1",
    )(ya, yb, h3, h3, x, wa, wb, wo, g.reshape(1, D_MODEL), b.reshape(1, D_MODEL))


GROUP_SIZE = N_EXPERTS // N_GROUPS


def _first_max(cur, rows, n_rows):
    m = jnp.max(cur, axis=0, keepdims=True)
    idx = jnp.min(jnp.where(cur == m, rows, n_rows), axis=0, keepdims=True)
    return m, idx


def _router_kernel(x_ref, wr_ref, rb_ref, ws1_ref, ws3_ref, ws2_ref,
                   eidx_ref, wts_ref, rank_ref, cnt_ref, base_ref, run_ref):
    @pl.when(pl.program_id(0) == 0)
    def _():
        run_ref[...] = jnp.zeros(run_ref.shape, F32)

    x = x_ref[...]
    tm = x.shape[0]
    logits = _dot_nt_hi(wr_ref[...], x)
    scores = _sigmoid(logits)
    biased = scores + rb_ref[...]
    neg = -jnp.inf

    rows_g = lax.broadcasted_iota(jnp.int32, (GROUP_SIZE, tm), 0)
    gs = []
    for g in range(N_GROUPS):
        blk = biased[g * GROUP_SIZE:(g + 1) * GROUP_SIZE, :]
        m1, i1 = _first_max(blk, rows_g, GROUP_SIZE)
        m2 = jnp.max(jnp.where(rows_g == i1, neg, blk), axis=0, keepdims=True)
        gs.append(m1 + m2)
    cur = jnp.concatenate(gs, axis=0)
    rows_n = lax.broadcasted_iota(jnp.int32, (N_GROUPS, tm), 0)
    gsel = jnp.zeros((N_GROUPS, tm), F32)
    for _ in range(TOPK_GROUPS):
        _, ig = _first_max(cur, rows_n, N_GROUPS)
        hit = rows_n == ig
        gsel = jnp.where(hit, 1.0, gsel)
        cur = jnp.where(hit, neg, cur)
    gexp = jnp.concatenate([jnp.broadcast_to(gsel[g:g + 1, :], (GROUP_SIZE, tm)) for g in range(N_GROUPS)], axis=0)
    cur = jnp.where(gexp > 0.0, biased, neg)

    rows_e = lax.broadcasted_iota(jnp.int32, (N_EXPERTS, tm), 0)
    es, ws, hits = [], [], []
    for _ in range(EXPERT_TOPK):
        _, ie = _first_max(cur, rows_e, N_EXPERTS)
        hit = rows_e == ie
        es.append(ie)
        hits.append(hit)
        ws.append(jnp.sum(jnp.where(hit, scores, 0.0), axis=0, keepdims=True))
        cur = jnp.where(hit, neg, cur)
    w = jnp.concatenate(ws, axis=0)
    eidx_ref[...] = jnp.concatenate(es, axis=0)
    wts_ref[...] = w / jnp.sum(w, axis=0, keepdims=True) * ROUTED_SCALE

    assigned = jnp.zeros((N_EXPERTS, tm), F32)
    for hit in hits:
        assigned = jnp.where(hit, 1.0, assigned)
    earlier = (lax.broadcasted_iota(jnp.int32, (tm, tm), 0) < lax.broadcasted_iota(jnp.int32, (tm, tm), 1))
    before = run_ref[...] + _dot(assigned.astype(BF16), earlier.astype(BF16))
    rank_ref[...] = jnp.concatenate(
        [jnp.sum(jnp.where(hit, before, 0.0), axis=0, keepdims=True) for hit in hits], axis=0).astype(jnp.int32)
    run_ref[...] = run_ref[...] + jnp.sum(assigned, axis=1, keepdims=True)
    cnt_ref[...] = run_ref[...].astype(jnp.int32)

    xb = x.astype(BF16)
    hs = _silu(_dot(xb, ws1_ref[...])) * _dot(xb, ws3_ref[...])
    base_ref[...] = DN_ALPHA * x + _dot(hs.astype(BF16), ws2_ref[...])


def _router(x1, wr_t, rbias, ws1, ws3, ws2, tm):
    t = x1.shape[0]
    full = lambda shp: pl.BlockSpec(shp, lambda i: (0, 0))
    return pl.pallas_call(
        _router_kernel,
        grid=(t // tm,),
        in_specs=[pl.BlockSpec((tm, D_MODEL), lambda i: (i, 0)),
                  full((N_EXPERTS, D_MODEL)), full((N_EXPERTS, 1)),
                  full((D_MODEL, SHARED_DIM)), full((D_MODEL, SHARED_DIM)), full((SHARED_DIM, D_MODEL))],
        out_specs=[pl.BlockSpec((EXPERT_TOPK, tm), lambda i: (0, i)),
                   pl.BlockSpec((EXPERT_TOPK, tm), lambda i: (0, i)),
                   pl.BlockSpec((EXPERT_TOPK, tm), lambda i: (0, i)),
                   full((N_EXPERTS, 1)),
                   pl.BlockSpec((tm, D_MODEL), lambda i: (i, 0))],
        out_shape=[jax.ShapeDtypeStruct((EXPERT_TOPK, t), jnp.int32),
                   jax.ShapeDtypeStruct((EXPERT_TOPK, t), F32),
                   jax.ShapeDtypeStruct((EXPERT_TOPK, t), jnp.int32),
                   jax.ShapeDtypeStruct((N_EXPERTS, 1), jnp.int32),
                   jax.ShapeDtypeStruct((t, D_MODEL), F32)],
        scratch_shapes=[pltpu.VMEM((N_EXPERTS, 1), F32)],
        compiler_params=_cparams(("arbitrary",)),
        name="router_shared",
    )(x1, wr_t, rbias.reshape(N_EXPERTS, 1), ws1, ws3, ws2)


def _block_plan(counts, n_blocks):
    padded = (counts + MOE_ROWS - 1) // MOE_ROWS * MOE_ROWS
    pad_end = jnp.cumsum(padded)
    blk_first = jnp.arange(n_blocks, dtype=jnp.int32) * MOE_ROWS
    blk_exp = jnp.minimum(jnp.sum((pad_end[None, :] <= blk_first[:, None]).astype(jnp.int32), axis=1), N_EXPERTS - 1)
    return pad_end - padded, blk_exp, pad_end[-1:] // MOE_ROWS


def _pos_kernel(e_ref, r_ref, ps_ref, pos_ref):
    tm = e_ref.shape[1]
    rows_e = lax.broadcasted_iota(jnp.int32, (N_EXPERTS, tm), 0)
    first = [jnp.sum(jnp.where(rows_e == e_ref[j:j + 1, :], ps_ref[...], 0), axis=0, keepdims=True)
             for j in range(EXPERT_TOPK)]
    pos_ref[...] = jnp.concatenate(first, axis=0) + r_ref[...]


def _positions(eidx_t, rank_t, pad_start, tm):
    t = eidx_t.shape[1]
    blk = pl.BlockSpec((EXPERT_TOPK, tm), lambda i: (0, i))
    return pl.pallas_call(
        _pos_kernel,
        grid=(t // tm,),
        in_specs=[blk, blk, pl.BlockSpec((N_EXPERTS, 1), lambda i: (0, 0))],
        out_specs=blk,
        out_shape=jax.ShapeDtypeStruct((EXPERT_TOPK, t), jnp.int32),
        compiler_params=_cparams(("parallel",)),
        name="moe_positions",
    )(eidx_t, rank_t, pad_start.reshape(N_EXPERTS, 1))


def _scatter_kernel(pos_ref, x_ref, rows_in_ref, rows_ref, sem):
    del rows_in_ref
    tm = x_ref.shape[0]

    def row_copy(t, j):
        return pltpu.make_async_copy(x_ref.at[pl.ds(t, 1), :], rows_ref.at[pl.ds(pos_ref[j, t], 1), :], sem)

    def issue(t, carry):
        for j in range(EXPERT_TOPK):
            row_copy(t, j).start()
        return carry

    lax.fori_loop(0, tm, issue, 0)
    for j in range(EXPERT_TOPK):
        pltpu.make_async_copy(x_ref, rows_ref.at[pl.ds(0, tm), :], sem).wait()


def _scatter_rows(pos_t, x1, n_rows, tm):
    t = x1.shape[0]
    return pl.pallas_call(
        _scatter_kernel,
        grid=(t // tm,),
        in_specs=[pl.BlockSpec((EXPERT_TOPK, tm), lambda i: (0, i), memory_space=pltpu.SMEM),
                  pl.BlockSpec((tm, D_MODEL), lambda i: (i, 0)),
                  pl.BlockSpec(memory_space=pl.ANY)],
        out_specs=pl.BlockSpec(memory_space=pl.ANY),
        out_shape=jax.ShapeDtypeStruct((n_rows, D_MODEL), F32),
        scratch_shapes=[pltpu.SemaphoreType.DMA(())],
        input_output_aliases={2: 0},
        compiler_params=_cparams(("arbitrary",)),
        name="moe_scatter",
    )(pos_t, x1, jnp.zeros((n_rows, D_MODEL), F32))


def _expert_kernel(be_ref, nu_ref, x_ref, w1_ref, w3_ref, w2_ref, o_ref, w1b_ref, w3b_ref, w2b_ref):
    i = pl.program_id(0)
    prev = be_ref[jnp.maximum(i - 1, 0)]

    @pl.when((i == 0) | (be_ref[i] != prev))
    def _():
        w1b_ref[...] = w1_ref[...].astype(BF16)
        w3b_ref[...] = w3_ref[...].astype(BF16)
        w2b_ref[...] = w2_ref[...].astype(BF16)

    @pl.when(i < nu_ref[0])
    def _():
        x = x_ref[...].astype(BF16)
        hmid = _silu(_dot(x, w1b_ref[...])) * _dot(x, w3b_ref[...])
        o_ref[...] = _dot(hmid.astype(BF16), w2b_ref[...])

    @pl.when(i >= nu_ref[0])
    def _():
        o_ref[...] = jnp.zeros(o_ref.shape, F32)


def _experts(xg, blk_exp, n_used, w1, w3, w2):
    rows = xg.shape[0]
    n_blocks = rows // MOE_ROWS
    grid_spec = pltpu.PrefetchScalarGridSpec(
        num_scalar_prefetch=2,
        grid=(n_blocks,),
        in_specs=[
            pl.BlockSpec((MOE_ROWS, D_MODEL), lambda i, be, nu: (i, 0)),
            pl.BlockSpec((None, D_MODEL, EXPERT_DIM), lambda i, be, nu: (be[i], 0, 0)),
            pl.BlockSpec((None, D_MODEL, EXPERT_DIM), lambda i, be, nu: (be[i], 0, 0)),
            pl.BlockSpec((None, EXPERT_DIM, D_MODEL), lambda i, be, nu: (be[i], 0, 0)),
        ],
        out_specs=pl.BlockSpec((MOE_ROWS, D_MODEL), lambda i, be, nu: (i, 0)),
        scratch_shapes=[pltpu.VMEM((D_MODEL, EXPERT_DIM), BF16), pltpu.VMEM((D_MODEL, EXPERT_DIM), BF16),
                        pltpu.VMEM((EXPERT_DIM, D_MODEL), BF16)],
    )
    return pl.pallas_call(
        _expert_kernel,
        grid_spec=grid_spec,
        out_shape=jax.ShapeDtypeStruct((rows, D_MODEL), F32),
        compiler_params=_cparams(("arbitrary",)),
        name="experts",
    )(blk_exp, n_used, xg, w1, w3, w2)


def _combine_kernel(pos_ref, y_ref, w_ref, base_ref, g_ref, b_ref, o_ref, gbuf_ref, sem):
    tm = base_ref.shape[0]

    def row_copy(t, j):
        return pltpu.make_async_copy(y_ref.at[pl.ds(pos_ref[j, t], 1), :], gbuf_ref.at[j, pl.ds(t, 1), :], sem)

    def issue(t, carry):
        for j in range(EXPERT_TOPK):
            row_copy(t, j).start()
        return carry

    lax.fori_loop(0, tm, issue, 0)
    for j in range(EXPERT_TOPK):
        pltpu.make_async_copy(y_ref.at[pl.ds(0, tm), :], gbuf_ref.at[j], sem).wait()

    acc = base_ref[...]
    w = w_ref[...]
    for j in range(EXPERT_TOPK):
        acc = acc + w[:, j:j + 1] * gbuf_ref[j]
    o_ref[...] = _layer_norm(acc, g_ref[...], b_ref[...])


def _combine(pos_t, y_rows, wts, base, g, b, tm):
    t = base.shape[0]
    return pl.pallas_call(
        _combine_kernel,
        grid=(t // tm,),
        in_specs=[pl.BlockSpec((EXPERT_TOPK, tm), lambda i: (0, i), memory_space=pltpu.SMEM),
                  pl.BlockSpec(memory_space=pl.ANY),
                  pl.BlockSpec((tm, EXPERT_TOPK), lambda i: (i, 0)),
                  pl.BlockSpec((tm, D_MODEL), lambda i: (i, 0)),
                  pl.BlockSpec((1, D_MODEL), lambda i: (0, 0)),
                  pl.BlockSpec((1, D_MODEL), lambda i: (0, 0))],
        out_specs=pl.BlockSpec((tm, D_MODEL), lambda i: (i, 0)),
        out_shape=jax.ShapeDtypeStruct((t, D_MODEL), F32),
        scratch_shapes=[pltpu.VMEM((EXPERT_TOPK, tm, D_MODEL), F32), pltpu.SemaphoreType.DMA(())],
        compiler_params=_cparams(("arbitrary",)),
        name="combine_ln2",
    )(pos_t, y_rows, wts, base, g.reshape(1, D_MODEL), b.reshape(1, D_MODEL))


def _moe(x1, prm):
    t = x1.shape[0]
    tm = _pick_tile(t, TOKEN_TM)
    eidx_t, wts_t, rank_t, counts, base = _router(x1, prm["w_router_t"], prm["router_bias"], prm["ws1"],
                                                  prm["ws3"], prm["ws2"], tm)
    n_blocks = -(-t * EXPERT_TOPK // MOE_ROWS) + N_EXPERTS
    pad_start, blk_exp, n_used = _block_plan(counts.reshape(N_EXPERTS), n_blocks)
    pos_t = _positions(eidx_t, rank_t, pad_start, tm)
    x_rows = _scatter_rows(pos_t, x1, n_blocks * MOE_ROWS, tm)
    y_rows = _experts(x_rows, blk_exp, n_used, prm["w1"], prm["w3"], prm["w2"])
    return _combine(pos_t, y_rows, wts_t.T, base, prm["ln2_g"], prm["ln2_b"], _pick_tile(t, COMBINE_TM))


def _mixer(x, pos0, conv_buf, s0, k_past, v_past, ik_past, prm):
    b, l, _ = x.shape
    lp = _round_up(l, DSA_Q_TILE)
    xp = x if lp == l else jnp.pad(x, ((0, 0), (0, lp - l), (0, 0)))
    h3 = _project(xp.reshape(b * lp, D_MODEL), prm["w_in"]).reshape(b, lp, H_COLS)

    ya, s_new, buf_new = _gdn(h3, l, conv_buf, s0, prm["conv_w"], prm["a_log"], prm["dt_bias"], prm["gdn_norm_g"])

    pos = pos0 + jnp.arange(lp, dtype=jnp.int32)
    qt, qit, wit, k_new, k_bf, ki_new, ki_bf, v_new, vt = _prep(h3, pos, prm["idx_k_ln_g"], prm["idx_k_ln_b"])
    if k_past is None:
        l_all = l
        k_all, ki_all, vt_all = k_bf, ki_bf, vt
    else:
        past = k_past.shape[1]
        l_all = past + l
        k_all = jnp.concatenate([k_past.reshape(b, past, ATT_KV).astype(BF16), k_bf[:, :l]], 1)
        ki_all = jnp.concatenate([ik_past.astype(BF16), ki_bf[:, :l]], 1)
        vt_all = jnp.concatenate([jnp.swapaxes(v_past.reshape(b, past, ATT_KV), 1, 2).astype(BF16), vt[:, :, :l]], 2)
    lk = _round_up(l_all, DSA_KEY_BLOCK)
    if lk != l_all:
        k_all = jnp.pad(k_all, ((0, 0), (0, lk - l_all), (0, 0)))
        ki_all = jnp.pad(ki_all, ((0, 0), (0, lk - l_all), (0, 0)))
        vt_all = jnp.pad(vt_all, ((0, 0), (0, 0), (0, lk - l_all)))
    yb = _dsa(qt, qit, wit, ki_all, k_all, vt_all, l_all, pos0)

    x1 = _merge(ya, yb, h3, x, prm["w_o_gdn"], prm["w_o_dsa"], prm["w_out"], prm["ln1_g"], prm["ln1_b"])
    state = (k_new[:, :l].reshape(b, l, ATT_KV_HEADS, HEAD_DIM), v_new[:, :l].reshape(b, l, ATT_KV_HEADS, HEAD_DIM),
             ki_new[:, :l], s_new, buf_new)
    return x1, state


def kernel(x_prompt, x_sample, cache_k, cache_v, cache_idx_k, state_gdn, state_conv, w_in, conv_w, a_log, dt_bias, gdn_norm_g, w_o_gdn, idx_k_ln_g, idx_k_ln_b, w_o_dsa, w_out, ln1_g, ln1_b, w_router, router_bias, w1, w3, w2, ws1, ws3, ws2, ln2_g, ln2_b):
    assert w_in.shape[0] == DEPTH == 1
    bp, lp_, _ = x_prompt.shape
    bs, ls_, _ = x_sample.shape
    past = cache_k.shape[2]
    prm = dict(
        w_in=_repack_w_in(w_in[0]).astype(BF16), conv_w=conv_w[0], a_log=a_log[0], dt_bias=dt_bias[0],
        gdn_norm_g=gdn_norm_g[0], w_o_gdn=w_o_gdn[0].astype(BF16), idx_k_ln_g=idx_k_ln_g[0],
        idx_k_ln_b=idx_k_ln_b[0], w_o_dsa=w_o_dsa[0].astype(BF16), w_out=w_out[0].astype(BF16),
        ln1_g=ln1_g[0], ln1_b=ln1_b[0], w_router_t=w_router[0].T, router_bias=router_bias[0],
        w1=w1[0], w3=w3[0], w2=w2[0], ws1=ws1[0].astype(BF16), ws3=ws3[0].astype(BF16),
        ws2=ws2[0].astype(BF16), ln2_g=ln2_g[0], ln2_b=ln2_b[0])
    conv0 = jnp.zeros((bp, CONV_W - 1, CONV_CH), F32)
    s0 = jnp.zeros((bp, GDN_HEADS, GDN_DK, GDN_DV), F32)
    x1p, sp = _mixer(x_prompt, 0, conv0, s0, None, None, None, prm)
    x1s, ss = _mixer(x_sample, past, state_conv[0], state_gdn[0], cache_k[0], cache_v[0], cache_idx_k[0], prm)
    tp, ts = bp * lp_, bs * ls_
    y = _moe(jnp.concatenate([x1p.reshape(tp, D_MODEL), x1s.reshape(ts, D_MODEL)], 0), prm)
    yp = y[:tp].reshape(bp, lp_, D_MODEL)
    ys = y[tp:].reshape(bs, ls_, D_MODEL)
    return (yp, ys) + tuple(a[None] for a in sp) + tuple(a[None] for a in ss)
```

```python
import functools

import jax
import jax.numpy as jnp
import numpy as np
from jax import lax
from jax.experimental import pallas as pl
from jax.experimental.pallas import tpu as pltpu

F32 = jnp.float32
BF16 = jnp.bfloat16

D_MODEL = 1024
CHUNK = 64
GDN_HEADS = 8
GDN_DK = 128
GDN_DV = 128
CONV_W = 4
ATT_HEADS = 8
ATT_KV_HEADS = 2
HEAD_DIM = 128
IDX_HEADS = 16
IDX_DIM = 64
TOPK_MAX = 256
ROPE_THETA = 500000.0
N_EXPERTS = 256
EXPERT_TOPK = 8
N_GROUPS = 8
TOPK_GROUPS = 4
EXPERT_DIM = 256
SHARED_DIM = 256
ROUTED_SCALE = 2.5
DEPTH = 1
DN_ALPHA = (2.0 * DEPTH) ** 0.25
LN_EPS = 1e-5
RMS_EPS = 1e-6

GDN_QK = GDN_HEADS * GDN_DK
GDN_V = GDN_HEADS * GDN_DV
CONV_CH = 2 * GDN_QK + GDN_V
ATT_Q = ATT_HEADS * HEAD_DIM
ATT_KV = ATT_KV_HEADS * HEAD_DIM
IDX_Q = IDX_HEADS * IDX_DIM
REP = ATT_HEADS // ATT_KV_HEADS

LANES = 128
SUBLANES = 8
VMEM_LIMIT = 56 * 1024 * 1024

PROJ_TM = 1024
PROJ_TN = 1024
PREP_TM = 512
DSA_Q_TILE = 128
DSA_KEY_BLOCK = 512
TOKEN_TM = 256
COMBINE_TM = 128
GDN_BATCH_TILE = 2
MOE_ROWS = 256

COL_QKV = 0
COL_Z = COL_QKV + CONV_CH
COL_Q = COL_Z + GDN_V
COL_QI = COL_Q + ATT_Q
COL_GA = COL_QI + IDX_Q
COL_GB = COL_GA + D_MODEL
COL_K = COL_GB + D_MODEL
COL_V = COL_K + ATT_KV
COL_SMALL = COL_V + ATT_KV
SMALL_W = 512
SM_A = IDX_DIM
SM_B = SM_A + GDN_HEADS
SM_WI = SM_B + GDN_HEADS
H_COLS = COL_SMALL + SMALL_W


def _cparams(sem):
    return pltpu.CompilerParams(dimension_semantics=sem, vmem_limit_bytes=VMEM_LIMIT)


def _dot(a, b):
    return jnp.dot(a, b, preferred_element_type=F32)


def _dot_bf(a, b):
    return jnp.dot(a.astype(BF16), b.astype(BF16), preferred_element_type=F32)


def _dot_hi(a, b):
    return jnp.dot(a, b, precision=lax.Precision.HIGHEST, preferred_element_type=F32)


def _dot_nt_hi(a, b):
    return lax.dot_general(a, b, (((1,), (1,)), ((), ())), precision=lax.Precision.HIGHEST,
                           preferred_element_type=F32)


def _dot_nt_bf(a, b):
    return lax.dot_general(a.astype(BF16), b.astype(BF16), (((1,), (1,)), ((), ())),
                           preferred_element_type=F32)


def _dot_tn_bf(a, b):
    return lax.dot_general(a.astype(BF16), b.astype(BF16), (((0,), (0,)), ((), ())),
                           preferred_element_type=F32)


def _sigmoid(x):
    return 1.0 / (1.0 + jnp.exp(-x))


def _silu(x):
    return x * _sigmoid(x)


def _round_up(n, m):
    return -(-n // m) * m


def _pick_tile(n, pref):
    t = min(n, pref)
    assert n % t == 0
    return t


def _repack_w_in(w_in):
    sizes = (CONV_CH, GDN_V, GDN_HEADS, GDN_HEADS, ATT_Q, ATT_KV, ATT_KV, IDX_Q, IDX_DIM, IDX_HEADS,
             D_MODEL, D_MODEL)
    offs = np.concatenate([[0], np.cumsum(sizes)])
    (p_qkv, p_z, p_a, p_b, p_q, p_k, p_v, p_qi, p_ki, p_wi, p_ga, p_gb) = [
        w_in[:, offs[i]:offs[i + 1]] for i in range(len(sizes))]
    pad = jnp.zeros((w_in.shape[0], SMALL_W - (SM_WI + IDX_HEADS)), w_in.dtype)
    return jnp.concatenate([p_qkv, p_z, p_q, p_qi, p_ga, p_gb, p_k, p_v, p_ki, p_a, p_b, p_wi, pad], axis=1)


def _proj_kernel(x_ref, w_ref, o_ref):
    o_ref[...] = _dot(x_ref[...].astype(BF16), w_ref[...])


def _project(x2d, w_bf):
    t, d = x2d.shape
    n = w_bf.shape[1]
    tm, tn = _pick_tile(t, PROJ_TM), _pick_tile(n, PROJ_TN)
    return pl.pallas_call(
        _proj_kernel,
        grid=(t // tm, n // tn),
        in_specs=[pl.BlockSpec((tm, d), lambda i, j: (i, 0)),
                  pl.BlockSpec((d, tn), lambda i, j: (0, j))],
        out_specs=pl.BlockSpec((tm, tn), lambda i, j: (i, j)),
        out_shape=jax.ShapeDtypeStruct((t, n), F32),
        compiler_params=_cparams(("parallel", "arbitrary")),
        name="in_proj",
    )(x2d, w_bf)


TAIL_ROWS = SUBLANES
INV_BASE = 8


def _split(a):
    hi = a.astype(BF16)
    return hi, (a - hi.astype(F32)).astype(BF16)


def _dot3(a, b):
    return _dot(a[0], b[0]) + (_dot(a[0], b[1]) + _dot(a[1], b[0]))


def _unit_lower_inverse(ms, c):
    ri = lax.broadcasted_iota(jnp.int32, (c, c), 0)
    ci = lax.broadcasted_iota(jnp.int32, (c, c), 1)
    eye = (ri == ci).astype(F32)
    blk = INV_BASE
    same = (ri // blk) == (ci // blk)
    ns = [jnp.where(same, -m, 0.0) for m in ms]
    xs = [eye + n for n in ns]
    span = 1
    while span * 2 < blk:
        nsp = [_split(n) for n in ns]
        ns = [_dot3(n, n) for n in nsp]
        nsp = [_split(n) for n in ns]
        xs = [x + _dot3(_split(x), n) for x, n in zip(xs, nsp)]
        span *= 2
    while blk < c:
        nxt = blk * 2
        emask = ((ri // nxt) == (ci // nxt)) & ((ri // blk) != (ci // blk))
        xsp = [_split(x) for x in xs]
        ts = [_dot3(x, _split(jnp.where(emask, m, 0.0))) for x, m in zip(xsp, ms)]
        xs = [x - _dot3(_split(t), xp) for x, t, xp in zip(xs, ts, xsp)]
        blk = nxt
    return xs


def _gdn_kernel(hq_ref, z_ref, sm_ref, buf_ref, s0_ref, cw_ref, alog_ref, dtb_ref, ng_ref,
                y_ref, snew_ref, bufnew_ref, xp_ref, s_ref, *, chunk):
    c_idx = pl.program_id(1)
    n_c = pl.num_programs(1)
    C = chunk
    keep = CONV_W - 1

    @pl.when(c_idx == 0)
    def _():
        xp_ref[:, TAIL_ROWS - keep:TAIL_ROWS, :] = buf_ref[...]
        s_ref[...] = s0_ref[...]

    ri = lax.broadcasted_iota(jnp.int32, (C, C), 0)
    ci = lax.broadcasted_iota(jnp.int32, (C, C), 1)
    incl = ri >= ci
    strict = ri > ci

    pairs = [(bi, h) for bi in range(hq_ref.shape[0]) for h in range(GDN_HEADS)]
    qs, ks, vs, gcs, bhs, egs, decays = [], [], [], [], [], [], []
    for bi in range(hq_ref.shape[0]):
        xp_ref[bi, TAIL_ROWS:TAIL_ROWS + C, :] = hq_ref[bi]

        acc = cw_ref[0:1, :] * xp_ref[bi, TAIL_ROWS - keep:TAIL_ROWS - keep + C, :]
        for j in range(1, CONV_W):
            acc = acc + cw_ref[j:j + 1, :] * xp_ref[bi, TAIL_ROWS - keep + j:TAIL_ROWS - keep + j + C, :]
        conv = _silu(acc)

        @pl.when(c_idx == n_c - 1)
        def _():
            bufnew_ref[bi] = xp_ref[bi, TAIL_ROWS + C - keep:TAIL_ROWS + C, :]

        xp_ref[bi, TAIL_ROWS - keep:TAIL_ROWS, :] = xp_ref[bi, TAIL_ROWS + C - keep:TAIL_ROWS + C, :]

        sm = sm_ref[bi, :, 0:LANES]
        xg = sm + dtb_ref[...]
        softplus = jnp.maximum(xg, 0.0) + jnp.log(1.0 + jnp.exp(-jnp.abs(xg)))
        g = -jnp.exp(alog_ref[...]) * softplus
        beta = _sigmoid(sm)
        gc = _dot_hi(incl.astype(F32), g)
        gc_t = gc.T

        for h in range(GDN_HEADS):
            q = conv[:, h * GDN_DK:(h + 1) * GDN_DK]
            k = conv[:, GDN_QK + h * GDN_DK:GDN_QK + (h + 1) * GDN_DK]
            qs.append(q * lax.rsqrt(jnp.sum(q * q, -1, keepdims=True) + 1e-6) * (GDN_DK ** -0.5))
            ks.append(k * lax.rsqrt(jnp.sum(k * k, -1, keepdims=True) + 1e-6))
            vs.append(conv[:, 2 * GDN_QK + h * GDN_DV:2 * GDN_QK + (h + 1) * GDN_DV])
            gch = gc[:, SM_A + h:SM_A + h + 1]
            gcs.append(gch)
            bhs.append(beta[:, SM_B + h:SM_B + h + 1])
            egs.append(jnp.exp(gch))
            decays.append(jnp.exp(jnp.where(incl, gch - gc_t[SM_A + h:SM_A + h + 1, :], -jnp.inf)))
    kbs = [k * bh for k, bh in zip(ks, bhs)]
    ms = [jnp.where(strict, _dot_nt_bf(kb, k) * d, 0.0) for kb, k, d in zip(kbs, ks, decays)]
    attns = [_dot_nt_bf(q, k) * d for q, k, d in zip(qs, ks, decays)]
    tinvs = _unit_lower_inverse(ms, C)
    sols = [_dot_bf(t, jnp.concatenate([v * bh, kb * eg], axis=-1))
            for t, v, bh, kb, eg in zip(tinvs, vs, bhs, kbs, egs)]
    ss = [s_ref[bi, h] for bi, h in pairs]
    v_news = [sol[:, :GDN_DV] - _dot_bf(sol[:, GDN_DV:], s) for sol, s in zip(sols, ss)]
    os_ = [_dot_bf(q * eg, s) + _dot_bf(a, vn) for q, eg, s, a, vn in zip(qs, egs, ss, attns, v_news)]
    for n, (bi, h) in enumerate(pairs):
        glast = gcs[n][C - 1:C, :]
        s_ref[bi, h] = ss[n] * jnp.exp(glast) + _dot_tn_bf(ks[n] * jnp.exp(glast - gcs[n]), v_news[n])
    for n, (bi, h) in enumerate(pairs):
        o = os_[n]
        o = o * lax.rsqrt(jnp.mean(o * o, -1, keepdims=True) + RMS_EPS) * ng_ref[...]
        zh = z_ref[bi, :, h * GDN_DV:(h + 1) * GDN_DV]
        y_ref[bi, :, h * GDN_DV:(h + 1) * GDN_DV] = o * _silu(zh)

    @pl.when(c_idx == n_c - 1)
    def _():
        snew_ref[...] = s_ref[...]


def _gdn(h3, l, conv_buf, s0, conv_w, a_log, dt_bias, norm_g):
    b = h3.shape[0]
    chunk = min(CHUNK, l)
    assert l % chunk == 0 and chunk % SUBLANES == 0 and chunk >= CONV_W - 1
    kern = functools.partial(_gdn_kernel, chunk=chunk)
    keep = CONV_W - 1
    lane_row = lambda vec, at: jnp.zeros((1, LANES), F32).at[0, at:at + vec.shape[0]].set(vec)
    gb = _pick_tile(b, GDN_BATCH_TILE)
    return pl.pallas_call(
        kern,
        grid=(b // gb, l // chunk),
        in_specs=[
            pl.BlockSpec((gb, chunk, CONV_CH), lambda i, c: (i, c, COL_QKV // CONV_CH)),
            pl.BlockSpec((gb, chunk, GDN_V), lambda i, c: (i, c, COL_Z // GDN_V)),
            pl.BlockSpec((gb, chunk, SMALL_W), lambda i, c: (i, c, COL_SMALL // SMALL_W)),
            pl.BlockSpec((gb, keep, CONV_CH), lambda i, c: (i, 0, 0)),
            pl.BlockSpec((gb, GDN_HEADS, GDN_DK, GDN_DV), lambda i, c: (i, 0, 0, 0)),
            pl.BlockSpec((CONV_W, CONV_CH), lambda i, c: (0, 0)),
            pl.BlockSpec((1, LANES), lambda i, c: (0, 0)),
            pl.BlockSpec((1, LANES), lambda i, c: (0, 0)),
            pl.BlockSpec((1, GDN_DV), lambda i, c: (0, 0)),
        ],
        out_specs=[
            pl.BlockSpec((gb, chunk, GDN_V), lambda i, c: (i, c, 0)),
            pl.BlockSpec((gb, GDN_HEADS, GDN_DK, GDN_DV), lambda i, c: (i, 0, 0, 0)),
            pl.BlockSpec((gb, keep, CONV_CH), lambda i, c: (i, 0, 0)),
        ],
        out_shape=[
            jax.ShapeDtypeStruct((b, l, GDN_V), F32),
            jax.ShapeDtypeStruct((b, GDN_HEADS, GDN_DK, GDN_DV), F32),
            jax.ShapeDtypeStruct((b, keep, CONV_CH), F32),
        ],
        scratch_shapes=[
            pltpu.VMEM((gb, TAIL_ROWS + chunk, CONV_CH), F32),
            pltpu.VMEM((gb, GDN_HEADS, GDN_DK, GDN_DV), F32),
        ],
        compiler_params=_cparams(("parallel", "arbitrary")),
        name="gdn",
    )(h3, h3, h3, conv_buf, s0, conv_w, lane_row(a_log, SM_A), lane_row(dt_bias, SM_A),
      norm_g.reshape(1, GDN_DV))


def _rope_tables(pos, rot, period):
    half = rot // 2
    inv_freq = ROPE_THETA ** (-(2.0 / rot) * jnp.arange(half, dtype=F32))
    ang = pos.astype(F32)[:, None] * inv_freq[None, :]
    cos, sin = jnp.cos(ang), jnp.sin(ang)
    n = pos.shape[0]
    rest = period - rot
    c = jnp.concatenate([cos, cos, jnp.ones((n, rest), F32)], -1)
    sa = jnp.concatenate([-sin, jnp.zeros((n, half + rest), F32)], -1)
    sb = jnp.concatenate([jnp.zeros((n, half), F32), sin, jnp.zeros((n, rest), F32)], -1)
    reps = LANES // period
    return jnp.stack([jnp.tile(c, (1, reps)), jnp.tile(sa, (1, reps)), jnp.tile(sb, (1, reps))], 0)


Q_SCALE_LOG2 = HEAD_DIM ** -0.5 * float(np.log2(np.e))


def _rope128(x, tab_ref, half):
    return (x * tab_ref[0] + pltpu.roll(x, LANES - half, 1) * tab_ref[1]
            + pltpu.roll(x, half, 1) * tab_ref[2])


def _prep_kernel(q_ref, qi_ref, k_ref, v_ref, sm_ref, tq_ref, ti_ref, lng_ref, lnb_ref,
                 qt_ref, qit_ref, wit_ref, ko_ref, kb_ref, kio_ref, kib_ref, vo_ref, vt_ref):
    hq = HEAD_DIM // 8
    hi = IDX_DIM // 8
    for h in range(ATT_HEADS):
        x = _rope128(q_ref[:, h * HEAD_DIM:(h + 1) * HEAD_DIM], tq_ref, hq) * Q_SCALE_LOG2
        qt_ref[h * HEAD_DIM:(h + 1) * HEAD_DIM, :] = x.T.astype(BF16)
    for h in range(ATT_KV_HEADS):
        x = _rope128(k_ref[:, h * HEAD_DIM:(h + 1) * HEAD_DIM], tq_ref, hq)
        ko_ref[:, h * HEAD_DIM:(h + 1) * HEAD_DIM] = x
        kb_ref[:, h * HEAD_DIM:(h + 1) * HEAD_DIM] = x.astype(BF16)
        v = v_ref[:, h * HEAD_DIM:(h + 1) * HEAD_DIM]
        vo_ref[:, h * HEAD_DIM:(h + 1) * HEAD_DIM] = v
        vt_ref[h * HEAD_DIM:(h + 1) * HEAD_DIM, :] = v.T.astype(BF16)
    for c in range(IDX_Q // LANES):
        x = _rope128(qi_ref[:, c * LANES:(c + 1) * LANES], ti_ref, hi)
        qit_ref[c * LANES:(c + 1) * LANES, :] = x.T.astype(BF16)
    sm = sm_ref[:, 0:LANES]
    lane = lax.broadcasted_iota(jnp.int32, sm.shape, 1)
    is_ki = lane < IDX_DIM
    mu = jnp.sum(jnp.where(is_ki, sm, 0.0), -1, keepdims=True) * (1.0 / IDX_DIM)
    xc = jnp.where(is_ki, sm - mu, 0.0)
    var = jnp.sum(xc * xc, -1, keepdims=True) * (1.0 / IDX_DIM)
    ki = xc * lax.rsqrt(var + LN_EPS) * lng_ref[...] + lnb_ref[...]
    ki = _rope128(ki, ti_ref, hi)[:, 0:IDX_DIM]
    kio_ref[...] = ki
    kib_ref[...] = ki.astype(BF16)
    wit_ref[...] = sm.T[SM_WI:SM_WI + IDX_HEADS, :] * (IDX_HEADS ** -0.5 * IDX_DIM ** -0.5)


def _prep(h3, pos, ln_g, ln_b):
    b, l, _ = h3.shape
    tm = _pick_tile(l, PREP_TM)
    tab_q = _rope_tables(pos, HEAD_DIM // 4, HEAD_DIM)
    tab_i = _rope_tables(pos, IDX_DIM // 4, IDX_DIM)
    lng = jnp.concatenate([ln_g, jnp.zeros((LANES - IDX_DIM,), F32)]).reshape(1, LANES)
    lnb = jnp.concatenate([ln_b, jnp.zeros((LANES - IDX_DIM,), F32)]).reshape(1, LANES)
    rows = lambda w, col: pl.BlockSpec((None, tm, w), lambda i, t: (i, t, col))
    cols = lambda w: pl.BlockSpec((None, w, tm), lambda i, t: (i, 0, t))
    return pl.pallas_call(
        _prep_kernel,
        grid=(b, l // tm),
        in_specs=[
            rows(ATT_Q, COL_Q // ATT_Q), rows(IDX_Q, COL_QI // IDX_Q), rows(ATT_KV, COL_K // ATT_KV),
            rows(ATT_KV, COL_V // ATT_KV), rows(SMALL_W, COL_SMALL // SMALL_W),
            pl.BlockSpec((3, tm, LANES), lambda i, t: (0, t, 0)),
            pl.BlockSpec((3, tm, LANES), lambda i, t: (0, t, 0)),
            pl.BlockSpec((1, LANES), lambda i, t: (0, 0)),
            pl.BlockSpec((1, LANES), lambda i, t: (0, 0)),
        ],
        out_specs=[cols(ATT_Q), cols(IDX_Q), cols(IDX_HEADS), rows(ATT_KV, 0), rows(ATT_KV, 0),
                   rows(IDX_DIM, 0), rows(IDX_DIM, 0), rows(ATT_KV, 0), cols(ATT_KV)],
        out_shape=[
            jax.ShapeDtypeStruct((b, ATT_Q, l), BF16),
            jax.ShapeDtypeStruct((b, IDX_Q, l), BF16),
            jax.ShapeDtypeStruct((b, IDX_HEADS, l), F32),
            jax.ShapeDtypeStruct((b, l, ATT_KV), F32),
            jax.ShapeDtypeStruct((b, l, ATT_KV), BF16),
            jax.ShapeDtypeStruct((b, l, IDX_DIM), F32),
            jax.ShapeDtypeStruct((b, l, IDX_DIM), BF16),
            jax.ShapeDtypeStruct((b, l, ATT_KV), F32),
            jax.ShapeDtypeStruct((b, ATT_KV, l), BF16),
        ],
        compiler_params=_cparams(("parallel", "parallel")),
        name="dsa_prep",
    )(h3, h3, h3, h3, h3, tab_q, tab_i, lng, lnb)


INT_MIN = -2 ** 31
INT_MAX = 2 ** 31 - 1
NEG_INF_KEY = INT_MIN + 0x7FFFFF
NEG_BIG = -1e30
HEAD_PAIRS = IDX_HEADS // 2
COUNT_CHAINS = 8
COUNT16_ROWS = 64
HALF16 = 1 << 15
DENOM_ROWS = 16


def _dsa_kernel(qt_ref, qit_ref, wit_ref, ki_ref, k_ref, vt_ref, o_ref,
                key_ref, khi_ref, klo_ref, qs_ref, qip_ref, m_ref, acc_ref, cm_ref,
                *, tq, kb, l_true, pos0, topk, idx_bits, x_search16=True, x_pair=True, x_phases=3):
    q0 = pos0 + pl.program_id(1) * tq
    qpos = q0 + lax.broadcasted_iota(jnp.int32, (1, tq), 1)
    lim = jnp.minimum((qpos // CHUNK + 1) * CHUNK, l_true)
    lim_max = jnp.minimum(((q0 + tq - 1) // CHUNK + 1) * CHUNK, l_true)
    nkb = (lim_max + kb - 1) // kb
    key_iota = lax.broadcasted_iota(jnp.int32, (kb, tq), 0)
    wit = wit_ref[...]

    for p in range(HEAD_PAIRS):
        for u in range(2):
            hh = 2 * p + u
            qip_ref[p, :, u * tq:(u + 1) * tq] = qit_ref[hh * IDX_DIM:(hh + 1) * IDX_DIM, :]
    for g in range(ATT_KV_HEADS):
        for r in range(REP):
            hh = g * REP + r
            qs_ref[g, :, r * tq:(r + 1) * tq] = qt_ref[hh * HEAD_DIM:(hh + 1) * HEAD_DIM, :]

    def score_blk(j, carry):
        off = pl.multiple_of(j * kb, kb)
        ki = ki_ref[pl.ds(off, kb), :]
        acc = jnp.zeros((kb, tq), F32)
        for p in range(HEAD_PAIRS):
            s2 = _dot(ki, qip_ref[p])
            acc = (acc + wit[2 * p:2 * p + 1, :] * jnp.maximum(s2[:, :tq], 0.0)
                   + wit[2 * p + 1:2 * p + 2, :] * jnp.maximum(s2[:, tq:], 0.0))
        score = jnp.where(off + key_iota < lim, acc, -jnp.inf)
        bits = pltpu.bitcast(score, jnp.int32)
        key = bits ^ ((bits >> 31) & INT_MAX)
        key_ref[pl.ds(off, kb), :] = key
        khi_ref[pl.ds(off, kb), :] = (key >> 16).astype(jnp.int16)
        return carry

    lax.fori_loop(0, nkb, score_blk, 0)

    def count(pred_fn):
        def blk(j, c):
            off = pl.multiple_of(j * kb, kb)
            part = jnp.where(pred_fn(key_ref[pl.ds(off, kb), :], off), 1.0, 0.0)
            return c + jnp.sum(part.reshape(kb // (COUNT_CHAINS * SUBLANES), COUNT_CHAINS * SUBLANES, tq), axis=0)
        c = lax.fori_loop(0, nkb, blk, jnp.zeros((COUNT_CHAINS * SUBLANES, tq), F32))
        return jnp.sum(c, axis=0, keepdims=True)

    def count16(ref, pred_fn):
        one, zero = jnp.ones((), jnp.int16), jnp.zeros((), jnp.int16)

        def blk(j, c):
            off = pl.multiple_of(j * kb, kb)
            part = jnp.where(pred_fn(ref[pl.ds(off, kb), :]), one, zero)
            for s in range(kb // COUNT16_ROWS):
                c = c + part[s * COUNT16_ROWS:(s + 1) * COUNT16_ROWS]
            return c
        c = lax.fori_loop(0, nkb, blk, jnp.zeros((COUNT16_ROWS, tq), jnp.int16))
        return jnp.sum(c.astype(jnp.int32), axis=0, keepdims=True)

    def search16(ref, want):
        def bit_step(t, cur):
            cand_u = cur | lax.shift_left(jnp.int32(1), 15 - t)
            cand = (cand_u - HALF16).astype(jnp.int16)
            cnt = count16(ref, lambda kk: kk >= cand)
            return jnp.where(cnt >= want, cand_u, cur)
        return lax.fori_loop(0, 16, bit_step, jnp.zeros((1, tq), jnp.int32))

    if x_phases < 2:
        o_ref[...] = jnp.zeros(o_ref.shape, F32) + key_ref[0:tq, 0:1].astype(F32)
        return
    if x_search16:
        t_hi = search16(khi_ref, topk) - HALF16
        t_hi16 = t_hi.astype(jnp.int16)
        n_hi_gt = count16(khi_ref, lambda kk: kk > t_hi16)

        def low_blk(j, carry):
            off = pl.multiple_of(j * kb, kb)
            kk = key_ref[pl.ds(off, kb), :]
            low = jnp.where((kk >> 16) == t_hi, (kk & 0xFFFF) - HALF16, -HALF16)
            klo_ref[pl.ds(off, kb), :] = low.astype(jnp.int16)
            return carry

        lax.fori_loop(0, nkb, low_blk, 0)
        t_lo = search16(klo_ref, topk - n_hi_gt)
        thr = t_hi * (2 * HALF16) + t_lo
    else:
        def bit_step(t, cur):
            cand_u = cur | lax.shift_left(jnp.int32(1), 31 - t)
            cand_s = cand_u ^ INT_MIN
            cnt = count(lambda kk, off: kk >= cand_s)
            return jnp.where(cnt >= topk, cand_u, cur)
        thr = lax.fori_loop(0, 32, bit_step, jnp.zeros((1, tq), jnp.int32)) ^ INT_MIN
    n_gt = count(lambda kk, off: kk > thr)
    n_ge = count(lambda kk, off: kk >= thr)
    need = topk - n_gt

    cm_ref[...] = jnp.full((1, tq), INT_MAX, jnp.int32)

    @pl.when(jnp.max(jnp.where((n_ge > topk) & (thr > NEG_INF_KEY), 1, 0)) > 0)
    def _():
        def idx_step(t, cm):
            cand = cm | lax.shift_left(jnp.int32(1), idx_bits - 1 - t)
            before = count(lambda kk, off: (kk == thr) & (off + key_iota < cand))
            return jnp.where(before < need, cand, cm)
        cm_ref[...] = lax.fori_loop(0, idx_bits, idx_step, jnp.zeros((1, tq), jnp.int32))

    cm = cm_ref[...]
    if x_phases < 3:
        o_ref[...] = jnp.zeros(o_ref.shape, F32) + jnp.concatenate([(thr + cm).astype(F32)] * ATT_HEADS, axis=1)
        return

    m_ref[...] = jnp.full(m_ref.shape, NEG_BIG, F32)
    acc_ref[...] = jnp.zeros(acc_ref.shape, F32)
    groups = range(ATT_KV_HEADS)
    ones_rows = jnp.ones((DENOM_ROWS, kb), BF16)

    def attn_blk(j):
        off = pl.multiple_of(j * kb, kb)
        kk = key_ref[pl.ds(off, kb), :]
        kpos = off + key_iota
        sel = ((kk > thr) | ((kk == thr) & (kpos <= cm))) & (kpos < lim)
        bias = jnp.where(sel, 0.0, NEG_BIG)
        bias = jnp.concatenate([bias] * REP, axis=1)
        logits = [_dot(k_ref[pl.ds(off, kb), g * HEAD_DIM:(g + 1) * HEAD_DIM], qs_ref[g]) + bias for g in groups]
        m_old = [m_ref[g] for g in groups]
        m_new = [jnp.maximum(m_old[g], jnp.max(logits[g], axis=0, keepdims=True)) for g in groups]
        ps = [jnp.exp2(logits[g] - m_new[g]).astype(BF16) for g in groups]
        alphas = [jnp.exp2(m_old[g] - m_new[g]) for g in groups]
        pvs = [_dot(jnp.concatenate([vt_ref[g * HEAD_DIM:(g + 1) * HEAD_DIM, pl.ds(off, kb)], ones_rows], axis=0),
                    ps[g]) for g in groups]
        for g in groups:
            acc_ref[g] = alphas[g] * acc_ref[g] + pvs[g]
            m_ref[g] = m_new[g]

    def attn_pair(jp, carry):
        attn_blk(2 * jp)
        attn_blk(2 * jp + 1)
        return carry

    if x_pair:
        lax.fori_loop(0, nkb // 2, attn_pair, 0)

        @pl.when(nkb % 2 == 1)
        def _():
            attn_blk(nkb - 1)
    else:
        def attn_one(j, carry):
            attn_blk(j)
            return carry
        lax.fori_loop(0, nkb, attn_one, 0)

    for g in groups:
        o_t = acc_ref[g, 0:HEAD_DIM, :] / acc_ref[g, HEAD_DIM:HEAD_DIM + 1, :]
        for r in range(REP):
            hh = g * REP + r
            o_ref[:, hh * HEAD_DIM:(hh + 1) * HEAD_DIM] = o_t[:, r * tq:(r + 1) * tq].T


def _dsa(qt, qit, wit, ki, k, vt, l_true, pos0, name="dsa", **xflags):
    b, _, t = qt.shape
    lk = k.shape[1]
    tq, kb = DSA_Q_TILE, DSA_KEY_BLOCK
    assert t % tq == 0 and lk % kb == 0 and tq == HEAD_DIM
    assert lk // COUNT16_ROWS < HALF16 and kb % COUNT16_ROWS == 0
    topk = min(TOPK_MAX, l_true // 4)
    kern = functools.partial(_dsa_kernel, tq=tq, kb=kb, l_true=l_true, pos0=pos0, topk=topk,
                             idx_bits=int(lk).bit_length(), **xflags)
    return pl.pallas_call(
        kern,
        grid=(b, t // tq),
        in_specs=[
            pl.BlockSpec((None, ATT_Q, tq), lambda i, t_: (i, 0, t_)),
            pl.BlockSpec((None, IDX_Q, tq), lambda i, t_: (i, 0, t_)),
            pl.BlockSpec((None, IDX_HEADS, tq), lambda i, t_: (i, 0, t_)),
            pl.BlockSpec((None, lk, IDX_DIM), lambda i, t_: (i, 0, 0)),
            pl.BlockSpec((None, lk, ATT_KV), lambda i, t_: (i, 0, 0)),
            pl.BlockSpec((None, ATT_KV, lk), lambda i, t_: (i, 0, 0)),
        ],
        out_specs=pl.BlockSpec((None, tq, ATT_Q), lambda i, t_: (i, t_, 0)),
        out_shape=jax.ShapeDtypeStruct((b, t, ATT_Q), F32),
        scratch_shapes=[
            pltpu.VMEM((lk, tq), jnp.int32),
            pltpu.VMEM((lk, tq), jnp.int16),
            pltpu.VMEM((lk, tq), jnp.int16),
            pltpu.VMEM((ATT_KV_HEADS, HEAD_DIM, REP * tq), BF16),
            pltpu.VMEM((HEAD_PAIRS, IDX_DIM, 2 * tq), BF16),
            pltpu.VMEM((ATT_KV_HEADS, 1, REP * tq), F32),
            pltpu.VMEM((ATT_KV_HEADS, HEAD_DIM + DENOM_ROWS, REP * tq), F32),
            pltpu.VMEM((1, tq), jnp.int32),
        ],
        compiler_params=_cparams(("parallel", "arbitrary")),
        name=name,
    )(qt, qit, wit, ki, k, vt)


def _layer_norm(x, g, b):
    mu = jnp.mean(x, -1, keepdims=True)
    xc = x - mu
    var = jnp.mean(xc * xc, -1, keepdims=True)
    return xc * lax.rsqrt(var + LN_EPS) * g + b


def _merge_kernel(ya_ref, yb_ref, ga_ref, gb_ref, x_ref, wa_ref, wb_ref, wo_ref, g_ref, b_ref, o_ref):
    pa = _dot(ya_ref[...].astype(BF16), wa_ref[...])
    pb = _dot(yb_ref[...].astype(BF16), wb_ref[...])
    merged = _sigmoid(ga_ref[...]) * pa + _sigmoid(gb_ref[...]) * pb
    y = DN_ALPHA * x_ref[...] + _dot(merged.astype(BF16), wo_ref[...])
    o_ref[...] = _layer_norm(y, g_ref[...], b_ref[...])


def _merge(ya, yb, h3, x, wa, wb, wo, g, b):
    bsz, l, _ = x.shape
    tm = _pick_tile(l, TOKEN_TM)
    row = lambda c: pl.BlockSpec((None, tm, D_MODEL), lambda i, t: (i, t, c))
    full = lambda shp: pl.BlockSpec(shp, lambda i, t: (0, 0))
    return pl.pallas_call(
        _merge_kernel,
        grid=(bsz, l // tm),
        in_specs=[row(0), row(0), row(COL_GA // D_MODEL), row(COL_GB // D_MODEL), row(0),
                  full((GDN_V, D_MODEL)), full((ATT_Q, D_MODEL)), full((D_MODEL, D_MODEL)),
                  full((1, D_MODEL)), full((1, D_MODEL))],
        out_specs=row(0),
        out_shape=jax.ShapeDtypeStruct((bsz, l, D_MODEL), F32),
        compiler_params=_cparams(("parallel", "parallel")),
        name="merge_ln1",
    )(ya, yb, h3, h3, x, wa, wb, wo, g.reshape(1, D_MODEL), b.reshape(1, D_MODEL))


GROUP_SIZE = N_EXPERTS // N_GROUPS


def _first_max(cur, rows, n_rows):
    m = jnp.max(cur, axis=0, keepdims=True)
    idx = jnp.min(jnp.where(cur == m, rows, n_rows), axis=0, keepdims=True)
    return m, idx


def _router_kernel(x_ref, wr_ref, rb_ref, ws1_ref, ws3_ref, ws2_ref,
                   eidx_ref, wts_ref, rank_ref, cnt_ref, base_ref, run_ref):
    @pl.when(pl.program_id(0) == 0)
    def _():
        run_ref[...] = jnp.zeros(run_ref.shape, F32)

    x = x_ref[...]
    tm = x.shape[0]
    logits = _dot_nt_hi(wr_ref[...], x)
    scores = _sigmoid(logits)
    biased = scores + rb_ref[...]
    neg = -jnp.inf

    rows_g = lax.broadcasted_iota(jnp.int32, (GROUP_SIZE, tm), 0)
    gs = []
    for g in range(N_GROUPS):
        blk = biased[g * GROUP_SIZE:(g + 1) * GROUP_SIZE, :]
        m1, i1 = _first_max(blk, rows_g, GROUP_SIZE)
        m2 = jnp.max(jnp.where(rows_g == i1, neg, blk), axis=0, keepdims=True)
        gs.append(m1 + m2)
    cur = jnp.concatenate(gs, axis=0)
    rows_n = lax.broadcasted_iota(jnp.int32, (N_GROUPS, tm), 0)
    gsel = jnp.zeros((N_GROUPS, tm), F32)
    for _ in range(TOPK_GROUPS):
        _, ig = _first_max(cur, rows_n, N_GROUPS)
        hit = rows_n == ig
        gsel = jnp.where(hit, 1.0, gsel)
        cur = jnp.where(hit, neg, cur)
    gexp = jnp.concatenate([jnp.broadcast_to(gsel[g:g + 1, :], (GROUP_SIZE, tm)) for g in range(N_GROUPS)], axis=0)
    cur = jnp.where(gexp > 0.0, biased, neg)

    rows_e = lax.broadcasted_iota(jnp.int32, (N_EXPERTS, tm), 0)
    es, ws, hits = [], [], []
    for _ in range(EXPERT_TOPK):
        _, ie = _first_max(cur, rows_e, N_EXPERTS)
        hit = rows_e == ie
        es.append(ie)
        hits.append(hit)
        ws.append(jnp.sum(jnp.where(hit, scores, 0.0), axis=0, keepdims=True))
        cur = jnp.where(hit, neg, cur)
    w = jnp.concatenate(ws, axis=0)
    eidx_ref[...] = jnp.concatenate(es, axis=0)
    wts_ref[...] = w / jnp.sum(w, axis=0, keepdims=True) * ROUTED_SCALE

    assigned = jnp.zeros((N_EXPERTS, tm), F32)
    for hit in hits:
        assigned = jnp.where(hit, 1.0, assigned)
    earlier = (lax.broadcasted_iota(jnp.int32, (tm, tm), 0) < lax.broadcasted_iota(jnp.int32, (tm, tm), 1))
    before = run_ref[...] + _dot(assigned.astype(BF16), earlier.astype(BF16))
    rank_ref[...] = jnp.concatenate(
        [jnp.sum(jnp.where(hit, before, 0.0), axis=0, keepdims=True) for hit in hits], axis=0).astype(jnp.int32)
    run_ref[...] = run_ref[...] + jnp.sum(assigned, axis=1, keepdims=True)
    cnt_ref[...] = run_ref[...].astype(jnp.int32)

    xb = x.astype(BF16)
    hs = _silu(_dot(xb, ws1_ref[...])) * _dot(xb, ws3_ref[...])
    base_ref[...] = DN_ALPHA * x + _dot(hs.astype(BF16), ws2_ref[...])


def _router(x1, wr_t, rbias, ws1, ws3, ws2, tm):
    t = x1.shape[0]
    full = lambda shp: pl.BlockSpec(shp, lambda i: (0, 0))
    return pl.pallas_call(
        _router_kernel,
        grid=(t // tm,),
        in_specs=[pl.BlockSpec((tm, D_MODEL), lambda i: (i, 0)),
                  full((N_EXPERTS, D_MODEL)), full((N_EXPERTS, 1)),
                  full((D_MODEL, SHARED_DIM)), full((D_MODEL, SHARED_DIM)), full((SHARED_DIM, D_MODEL))],
        out_specs=[pl.BlockSpec((EXPERT_TOPK, tm), lambda i: (0, i)),
                   pl.BlockSpec((EXPERT_TOPK, tm), lambda i: (0, i)),
                   pl.BlockSpec((EXPERT_TOPK, tm), lambda i: (0, i)),
                   full((N_EXPERTS, 1)),
                   pl.BlockSpec((tm, D_MODEL), lambda i: (i, 0))],
        out_shape=[jax.ShapeDtypeStruct((EXPERT_TOPK, t), jnp.int32),
                   jax.ShapeDtypeStruct((EXPERT_TOPK, t), F32),
                   jax.ShapeDtypeStruct((EXPERT_TOPK, t), jnp.int32),
                   jax.ShapeDtypeStruct((N_EXPERTS, 1), jnp.int32),
                   jax.ShapeDtypeStruct((t, D_MODEL), F32)],
        scratch_shapes=[pltpu.VMEM((N_EXPERTS, 1), F32)],
        compiler_params=_cparams(("arbitrary",)),
        name="router_shared",
    )(x1, wr_t, rbias.reshape(N_EXPERTS, 1), ws1, ws3, ws2)


def _block_plan(counts, n_blocks):
    padded = (counts + MOE_ROWS - 1) // MOE_ROWS * MOE_ROWS
    pad_end = jnp.cumsum(padded)
    blk_first = jnp.arange(n_blocks, dtype=jnp.int32) * MOE_ROWS
    blk_exp = jnp.minimum(jnp.sum((pad_end[None, :] <= blk_first[:, None]).astype(jnp.int32), axis=1), N_EXPERTS - 1)
    return pad_end - padded, blk_exp, pad_end[-1:] // MOE_ROWS


def _pos_kernel(e_ref, r_ref, ps_ref, pos_ref):
    tm = e_ref.shape[1]
    rows_e = lax.broadcasted_iota(jnp.int32, (N_EXPERTS, tm), 0)
    first = [jnp.sum(jnp.where(rows_e == e_ref[j:j + 1, :], ps_ref[...], 0), axis=0, keepdims=True)
             for j in range(EXPERT_TOPK)]
    pos_ref[...] = jnp.concatenate(first, axis=0) + r_ref[...]


def _positions(eidx_t, rank_t, pad_start, tm):
    t = eidx_t.shape[1]
    blk = pl.BlockSpec((EXPERT_TOPK, tm), lambda i: (0, i))
    return pl.pallas_call(
        _pos_kernel,
        grid=(t // tm,),
        in_specs=[blk, blk, pl.BlockSpec((N_EXPERTS, 1), lambda i: (0, 0))],
        out_specs=blk,
        out_shape=jax.ShapeDtypeStruct((EXPERT_TOPK, t), jnp.int32),
        compiler_params=_cparams(("parallel",)),
        name="moe_positions",
    )(eidx_t, rank_t, pad_start.reshape(N_EXPERTS, 1))


def _scatter_kernel(pos_ref, x_ref, rows_in_ref, rows_ref, sem):
    del rows_in_ref
    tm = x_ref.shape[0]

    def row_copy(t, j):
        return pltpu.make_async_copy(x_ref.at[pl.ds(t, 1), :], rows_ref.at[pl.ds(pos_ref[j, t], 1), :], sem)

    def issue(t, carry):
        for j in range(EXPERT_TOPK):
            row_copy(t, j).start()
        return carry

    lax.fori_loop(0, tm, issue, 0)
    for j in range(EXPERT_TOPK):
        pltpu.make_async_copy(x_ref, rows_ref.at[pl.ds(0, tm), :], sem).wait()


def _scatter_rows(pos_t, x1, n_rows, tm):
    t = x1.shape[0]
    return pl.pallas_call(
        _scatter_kernel,
        grid=(t // tm,),
        in_specs=[pl.BlockSpec((EXPERT_TOPK, tm), lambda i: (0, i), memory_space=pltpu.SMEM),
                  pl.BlockSpec((tm, D_MODEL), lambda i: (i, 0)),
                  pl.BlockSpec(memory_space=pl.ANY)],
        out_specs=pl.BlockSpec(memory_space=pl.ANY),
        out_shape=jax.ShapeDtypeStruct((n_rows, D_MODEL), F32),
        scratch_shapes=[pltpu.SemaphoreType.DMA(())],
        input_output_aliases={2: 0},
        compiler_params=_cparams(("arbitrary",)),
        name="moe_scatter",
    )(pos_t, x1, jnp.zeros((n_rows, D_MODEL), F32))


def _expert_kernel(be_ref, nu_ref, x_ref, w1_ref, w3_ref, w2_ref, o_ref, w1b_ref, w3b_ref, w2b_ref):
    i = pl.program_id(0)
    prev = be_ref[jnp.maximum(i - 1, 0)]

    @pl.when((i == 0) | (be_ref[i] != prev))
    def _():
        w1b_ref[...] = w1_ref[...].astype(BF16)
        w3b_ref[...] = w3_ref[...].astype(BF16)
        w2b_ref[...] = w2_ref[...].astype(BF16)

    @pl.when(i < nu_ref[0])
    def _():
        x = x_ref[...].astype(BF16)
        hmid = _silu(_dot(x, w1b_ref[...])) * _dot(x, w3b_ref[...])
        o_ref[...] = _dot(hmid.astype(BF16), w2b_ref[...])

    @pl.when(i >= nu_ref[0])
    def _():
        o_ref[...] = jnp.zeros(o_ref.shape, F32)


def _experts(xg, blk_exp, n_used, w1, w3, w2):
    rows = xg.shape[0]
    n_blocks = rows // MOE_ROWS
    grid_spec = pltpu.PrefetchScalarGridSpec(
        num_scalar_prefetch=2,
        grid=(n_blocks,),
        in_specs=[
            pl.BlockSpec((MOE_ROWS, D_MODEL), lambda i, be, nu: (i, 0)),
            pl.BlockSpec((None, D_MODEL, EXPERT_DIM), lambda i, be, nu: (be[i], 0, 0)),
            pl.BlockSpec((None, D_MODEL, EXPERT_DIM), lambda i, be, nu: (be[i], 0, 0)),
            pl.BlockSpec((None, EXPERT_DIM, D_MODEL), lambda i, be, nu: (be[i], 0, 0)),
        ],
        out_specs=pl.BlockSpec((MOE_ROWS, D_MODEL), lambda i, be, nu: (i, 0)),
        scratch_shapes=[pltpu.VMEM((D_MODEL, EXPERT_DIM), BF16), pltpu.VMEM((D_MODEL, EXPERT_DIM), BF16),
                        pltpu.VMEM((EXPERT_DIM, D_MODEL), BF16)],
    )
    return pl.pallas_call(
        _expert_kernel,
        grid_spec=grid_spec,
        out_shape=jax.ShapeDtypeStruct((rows, D_MODEL), F32),
        compiler_params=_cparams(("arbitrary",)),
        name="experts",
    )(blk_exp, n_used, xg, w1, w3, w2)


def _combine_kernel(pos_ref, y_ref, w_ref, base_ref, g_ref, b_ref, o_ref, gbuf_ref, sem):
    tm = base_ref.shape[0]

    def row_copy(t, j):
        return pltpu.make_async_copy(y_ref.at[pl.ds(pos_ref[j, t], 1), :], gbuf_ref.at[j, pl.ds(t, 1), :], sem)

    def issue(t, carry):
        for j in range(EXPERT_TOPK):
            row_copy(t, j).start()
        return carry

    lax.fori_loop(0, tm, issue, 0)
    for j in range(EXPERT_TOPK):
        pltpu.make_async_copy(y_ref.at[pl.ds(0, tm), :], gbuf_ref.at[j], sem).wait()

    acc = base_ref[...]
    w = w_ref[...]
    for j in range(EXPERT_TOPK):
        acc = acc + w[:, j:j + 1] * gbuf_ref[j]
    o_ref[...] = _layer_norm(acc, g_ref[...], b_ref[...])


def _combine(pos_t, y_rows, wts, base, g, b, tm):
    t = base.shape[0]
    return pl.pallas_call(
        _combine_kernel,
        grid=(t // tm,),
        in_specs=[pl.BlockSpec((EXPERT_TOPK, tm), lambda i: (0, i), memory_space=pltpu.SMEM),
                  pl.BlockSpec(memory_space=pl.ANY),
                  pl.BlockSpec((tm, EXPERT_TOPK), lambda i: (i, 0)),
                  pl.BlockSpec((tm, D_MODEL), lambda i: (i, 0)),
                  pl.BlockSpec((1, D_MODEL), lambda i: (0, 0)),
                  pl.BlockSpec((1, D_MODEL), lambda i: (0, 0))],
        out_specs=pl.BlockSpec((tm, D_MODEL), lambda i: (i, 0)),
        out_shape=jax.ShapeDtypeStruct((t, D_MODEL), F32),
        scratch_shapes=[pltpu.VMEM((EXPERT_TOPK, tm, D_MODEL), F32), pltpu.SemaphoreType.DMA(())],
        compiler_params=_cparams(("arbitrary",)),
        name="combine_ln2",
    )(pos_t, y_rows, wts, base, g.reshape(1, D_MODEL), b.reshape(1, D_MODEL))


def _moe(x1, prm):
    t = x1.shape[0]
    tm = _pick_tile(t, TOKEN_TM)
    eidx_t, wts_t, rank_t, counts, base = _router(x1, prm["w_router_t"], prm["router_bias"], prm["ws1"],
                                                  prm["ws3"], prm["ws2"], tm)
    n_blocks = -(-t * EXPERT_TOPK // MOE_ROWS) + N_EXPERTS
    pad_start, blk_exp, n_used = _block_plan(counts.reshape(N_EXPERTS), n_blocks)
    pos_t = _positions(eidx_t, rank_t, pad_start, tm)
    x_rows = _scatter_rows(pos_t, x1, n_blocks * MOE_ROWS, tm)
    y_rows = _experts(x_rows, blk_exp, n_used, prm["w1"], prm["w3"], prm["w2"])
    return _combine(pos_t, y_rows, wts_t.T, base, prm["ln2_g"], prm["ln2_b"], _pick_tile(t, COMBINE_TM))


def _mixer(x, pos0, conv_buf, s0, k_past, v_past, ik_past, prm):
    b, l, _ = x.shape
    lp = _round_up(l, DSA_Q_TILE)
    xp = x if lp == l else jnp.pad(x, ((0, 0), (0, lp - l), (0, 0)))
    h3 = _project(xp.reshape(b * lp, D_MODEL), prm["w_in"]).reshape(b, lp, H_COLS)

    ya, s_new, buf_new = _gdn(h3, l, conv_buf, s0, prm["conv_w"], prm["a_log"], prm["dt_bias"], prm["gdn_norm_g"])

    pos = pos0 + jnp.arange(lp, dtype=jnp.int32)
    qt, qit, wit, k_new, k_bf, ki_new, ki_bf, v_new, vt = _prep(h3, pos, prm["idx_k_ln_g"], prm["idx_k_ln_b"])
    if k_past is None:
        l_all = l
        k_all, ki_all, vt_all = k_bf, ki_bf, vt
    else:
        past = k_past.shape[1]
        l_all = past + l
        k_all = jnp.concatenate([k_past.reshape(b, past, ATT_KV).astype(BF16), k_bf[:, :l]], 1)
        ki_all = jnp.concatenate([ik_past.astype(BF16), ki_bf[:, :l]], 1)
        vt_all = jnp.concatenate([jnp.swapaxes(v_past.reshape(b, past, ATT_KV), 1, 2).astype(BF16), vt[:, :, :l]], 2)
    lk = _round_up(l_all, DSA_KEY_BLOCK)
    if lk != l_all:
        k_all = jnp.pad(k_all, ((0, 0), (0, lk - l_all), (0, 0)))
        ki_all = jnp.pad(ki_all, ((0, 0), (0, lk - l_all), (0, 0)))
        vt_all = jnp.pad(vt_all, ((0, 0), (0, 0), (0, lk - l_all)))
    yb = _dsa(qt, qit, wit, ki_all, k_all, vt_all, l_all, pos0)
    if k_past is None:
        probes = [_dsa(qt, qit, wit, ki_all, k_all, vt_all, l_all, pos0, name=n, **f) for n, f in (
            ("dsaprobe_s32_pair", dict(x_search16=False)), ("dsaprobe_s16_single", dict(x_pair=False)),
            ("dsaprobe_p1", dict(x_phases=1)), ("dsaprobe_p12", dict(x_phases=2)))]
        yb = yb + 0.0 * sum(p[:, :1, :1] for p in probes)

    x1 = _merge(ya, yb, h3, x, prm["w_o_gdn"], prm["w_o_dsa"], prm["w_out"], prm["ln1_g"], prm["ln1_b"])
    state = (k_new[:, :l].reshape(b, l, ATT_KV_HEADS, HEAD_DIM), v_new[:, :l].reshape(b, l, ATT_KV_HEADS, HEAD_DIM),
             ki_new[:, :l], s_new, buf_new)
    return x1, state


def kernel(x_prompt, x_sample, cache_k, cache_v, cache_idx_k, state_gdn, state_conv, w_in, conv_w, a_log, dt_bias, gdn_norm_g, w_o_gdn, idx_k_ln_g, idx_k_ln_b, w_o_dsa, w_out, ln1_g, ln1_b, w_router, router_bias, w1, w3, w2, ws1, ws3, ws2, ln2_g, ln2_b):
    assert w_in.shape[0] == DEPTH == 1
    bp, lp_, _ = x_prompt.shape
    bs, ls_, _ = x_sample.shape
    past = cache_k.shape[2]
    prm = dict(
        w_in=_repack_w_in(w_in[0]).astype(BF16), conv_w=conv_w[0], a_log=a_log[0], dt_bias=dt_bias[0],
        gdn_norm_g=gdn_norm_g[0], w_o_gdn=w_o_gdn[0].astype(BF16), idx_k_ln_g=idx_k_ln_g[0],
        idx_k_ln_b=idx_k_ln_b[0], w_o_dsa=w_o_dsa[0].astype(BF16), w_out=w_out[0].astype(BF16),
        ln1_g=ln1_g[0], ln1_b=ln1_b[0], w_router_t=w_router[0].T, router_bias=router_bias[0],
        w1=w1[0], w3=w3[0], w2=w2[0], ws1=ws1[0].astype(BF16), ws3=ws3[0].astype(BF16),
        ws2=ws2[0].astype(BF16), ln2_g=ln2_g[0], ln2_b=ln2_b[0])
    conv0 = jnp.zeros((bp, CONV_W - 1, CONV_CH), F32)
    s0 = jnp.zeros((bp, GDN_HEADS, GDN_DK, GDN_DV), F32)
    x1p, sp = _mixer(x_prompt, 0, conv0, s0, None, None, None, prm)
    x1s, ss = _mixer(x_sample, past, state_conv[0], state_gdn[0], cache_k[0], cache_v[0], cache_idx_k[0], prm)
    tp, ts = bp * lp_, bs * ls_
    y = _moe(jnp.concatenate([x1p.reshape(tp, D_MODEL), x1s.reshape(ts, D_MODEL)], 0), prm)
    yp = y[:tp].reshape(bp, lp_, D_MODEL)
    ys = y[tp:].reshape(bs, ls_, D_MODEL)
    return (yp, ys) + tuple(a[None] for a in sp) + tuple(a[None] for a in ss)
```

```python
import functools

import jax
import jax.numpy as jnp
import numpy as np
from jax import lax
from jax.experimental import pallas as pl
from jax.experimental.pallas import tpu as pltpu

F32 = jnp.float32
BF16 = jnp.bfloat16

D_MODEL = 1024
CHUNK = 64
GDN_HEADS = 8
GDN_DK = 128
GDN_DV = 128
CONV_W = 4
ATT_HEADS = 8
ATT_KV_HEADS = 2
HEAD_DIM = 128
IDX_HEADS = 16
IDX_DIM = 64
TOPK_MAX = 256
ROPE_THETA = 500000.0
N_EXPERTS = 256
EXPERT_TOPK = 8
N_GROUPS = 8
TOPK_GROUPS = 4
EXPERT_DIM = 256
SHARED_DIM = 256
ROUTED_SCALE = 2.5
DEPTH = 1
DN_ALPHA = (2.0 * DEPTH) ** 0.25
LN_EPS = 1e-5
RMS_EPS = 1e-6

GDN_QK = GDN_HEADS * GDN_DK
GDN_V = GDN_HEADS * GDN_DV
CONV_CH = 2 * GDN_QK + GDN_V
ATT_Q = ATT_HEADS * HEAD_DIM
ATT_KV = ATT_KV_HEADS * HEAD_DIM
IDX_Q = IDX_HEADS * IDX_DIM
REP = ATT_HEADS // ATT_KV_HEADS

LANES = 128
SUBLANES = 8
VMEM_LIMIT = 56 * 1024 * 1024

PROJ_TM = 1024
PROJ_TN = 1024
PREP_TM = 512
DSA_Q_TILE = 128
DSA_KEY_BLOCK = 512
TOKEN_TM = 256
COMBINE_TM = 128
GDN_BATCH_TILE = 2
MOE_ROWS = 256

COL_QKV = 0
COL_Z = COL_QKV + CONV_CH
COL_Q = COL_Z + GDN_V
COL_QI = COL_Q + ATT_Q
COL_GA = COL_QI + IDX_Q
COL_GB = COL_GA + D_MODEL
COL_K = COL_GB + D_MODEL
COL_V = COL_K + ATT_KV
COL_SMALL = COL_V + ATT_KV
SMALL_W = 512
SM_A = IDX_DIM
SM_B = SM_A + GDN_HEADS
SM_WI = SM_B + GDN_HEADS
H_COLS = COL_SMALL + SMALL_W


def _cparams(sem):
    return pltpu.CompilerParams(dimension_semantics=sem, vmem_limit_bytes=VMEM_LIMIT)


def _dot(a, b):
    return jnp.dot(a, b, preferred_element_type=F32)


def _dot_bf(a, b):
    return jnp.dot(a.astype(BF16), b.astype(BF16), preferred_element_type=F32)


def _dot_hi(a, b):
    return jnp.dot(a, b, precision=lax.Precision.HIGHEST, preferred_element_type=F32)


def _dot_nt_hi(a, b):
    return lax.dot_general(a, b, (((1,), (1,)), ((), ())), precision=lax.Precision.HIGHEST,
                           preferred_element_type=F32)


def _dot_nt_bf(a, b):
    return lax.dot_general(a.astype(BF16), b.astype(BF16), (((1,), (1,)), ((), ())),
                           preferred_element_type=F32)


def _dot_tn_bf(a, b):
    return lax.dot_general(a.astype(BF16), b.astype(BF16), (((0,), (0,)), ((), ())),
                           preferred_element_type=F32)


def _sigmoid(x):
    return 1.0 / (1.0 + jnp.exp(-x))


def _silu(x):
    return x * _sigmoid(x)


def _round_up(n, m):
    return -(-n // m) * m


def _pick_tile(n, pref):
    t = min(n, pref)
    assert n % t == 0
    return t


def _repack_w_in(w_in):
    sizes = (CONV_CH, GDN_V, GDN_HEADS, GDN_HEADS, ATT_Q, ATT_KV, ATT_KV, IDX_Q, IDX_DIM, IDX_HEADS,
             D_MODEL, D_MODEL)
    offs = np.concatenate([[0], np.cumsum(sizes)])
    (p_qkv, p_z, p_a, p_b, p_q, p_k, p_v, p_qi, p_ki, p_wi, p_ga, p_gb) = [
        w_in[:, offs[i]:offs[i + 1]] for i in range(len(sizes))]
    pad = jnp.zeros((w_in.shape[0], SMALL_W - (SM_WI + IDX_HEADS)), w_in.dtype)
    return jnp.concatenate([p_qkv, p_z, p_q, p_qi, p_ga, p_gb, p_k, p_v, p_ki, p_a, p_b, p_wi, pad], axis=1)


def _proj_kernel(x_ref, w_ref, o_ref):
    o_ref[...] = _dot(x_ref[...].astype(BF16), w_ref[...])


def _project(x2d, w_bf):
    t, d = x2d.shape
    n = w_bf.shape[1]
    tm, tn = _pick_tile(t, PROJ_TM), _pick_tile(n, PROJ_TN)
    return pl.pallas_call(
        _proj_kernel,
        grid=(t // tm, n // tn),
        in_specs=[pl.BlockSpec((tm, d), lambda i, j: (i, 0)),
                  pl.BlockSpec((d, tn), lambda i, j: (0, j))],
        out_specs=pl.BlockSpec((tm, tn), lambda i, j: (i, j)),
        out_shape=jax.ShapeDtypeStruct((t, n), F32),
        compiler_params=_cparams(("parallel", "arbitrary")),
        name="in_proj",
    )(x2d, w_bf)


TAIL_ROWS = SUBLANES
INV_BASE = 8


def _split(a):
    hi = a.astype(BF16)
    return hi, (a - hi.astype(F32)).astype(BF16)


def _dot3(a, b):
    return _dot(a[0], b[0]) + (_dot(a[0], b[1]) + _dot(a[1], b[0]))


def _unit_lower_inverse(ms, c):
    ri = lax.broadcasted_iota(jnp.int32, (c, c), 0)
    ci = lax.broadcasted_iota(jnp.int32, (c, c), 1)
    eye = (ri == ci).astype(F32)
    blk = INV_BASE
    same = (ri // blk) == (ci // blk)
    ns = [jnp.where(same, -m, 0.0) for m in ms]
    xs = [eye + n for n in ns]
    span = 1
    while span * 2 < blk:
        nsp = [_split(n) for n in ns]
        ns = [_dot3(n, n) for n in nsp]
        nsp = [_split(n) for n in ns]
        xs = [x + _dot3(_split(x), n) for x, n in zip(xs, nsp)]
        span *= 2
    while blk < c:
        nxt = blk * 2
        emask = ((ri // nxt) == (ci // nxt)) & ((ri // blk) != (ci // blk))
        xsp = [_split(x) for x in xs]
        ts = [_dot3(x, _split(jnp.where(emask, m, 0.0))) for x, m in zip(xsp, ms)]
        xs = [x - _dot3(_split(t), xp) for x, t, xp in zip(xs, ts, xsp)]
        blk = nxt
    return xs


def _gdn_kernel(hq_ref, z_ref, sm_ref, buf_ref, s0_ref, cw_ref, alog_ref, dtb_ref, ng_ref,
                y_ref, snew_ref, bufnew_ref, xp_ref, s_ref, *, chunk):
    c_idx = pl.program_id(1)
    n_c = pl.num_programs(1)
    C = chunk
    keep = CONV_W - 1

    @pl.when(c_idx == 0)
    def _():
        xp_ref[:, TAIL_ROWS - keep:TAIL_ROWS, :] = buf_ref[...]
        s_ref[...] = s0_ref[...]

    ri = lax.broadcasted_iota(jnp.int32, (C, C), 0)
    ci = lax.broadcasted_iota(jnp.int32, (C, C), 1)
    incl = ri >= ci
    strict = ri > ci

    pairs = [(bi, h) for bi in range(hq_ref.shape[0]) for h in range(GDN_HEADS)]
    qs, ks, vs, gcs, bhs, egs, decays = [], [], [], [], [], [], []
    for bi in range(hq_ref.shape[0]):
        xp_ref[bi, TAIL_ROWS:TAIL_ROWS + C, :] = hq_ref[bi]

        acc = cw_ref[0:1, :] * xp_ref[bi, TAIL_ROWS - keep:TAIL_ROWS - keep + C, :]
        for j in range(1, CONV_W):
            acc = acc + cw_ref[j:j + 1, :] * xp_ref[bi, TAIL_ROWS - keep + j:TAIL_ROWS - keep + j + C, :]
        conv = _silu(acc)

        @pl.when(c_idx == n_c - 1)
        def _():
            bufnew_ref[bi] = xp_ref[bi, TAIL_ROWS + C - keep:TAIL_ROWS + C, :]

        xp_ref[bi, TAIL_ROWS - keep:TAIL_ROWS, :] = xp_ref[bi, TAIL_ROWS + C - keep:TAIL_ROWS + C, :]

        sm = sm_ref[bi, :, 0:LANES]
        xg = sm + dtb_ref[...]
        softplus = jnp.maximum(xg, 0.0) + jnp.log(1.0 + jnp.exp(-jnp.abs(xg)))
        g = -jnp.exp(alog_ref[...]) * softplus
        beta = _sigmoid(sm)
        gc = _dot_hi(incl.astype(F32), g)
        gc_t = gc.T

        for h in range(GDN_HEADS):
            q = conv[:, h * GDN_DK:(h + 1) * GDN_DK]
            k = conv[:, GDN_QK + h * GDN_DK:GDN_QK + (h + 1) * GDN_DK]
            qs.append(q * lax.rsqrt(jnp.sum(q * q, -1, keepdims=True) + 1e-6) * (GDN_DK ** -0.5))
            ks.append(k * lax.rsqrt(jnp.sum(k * k, -1, keepdims=True) + 1e-6))
            vs.append(conv[:, 2 * GDN_QK + h * GDN_DV:2 * GDN_QK + (h + 1) * GDN_DV])
            gch = gc[:, SM_A + h:SM_A + h + 1]
            gcs.append(gch)
            bhs.append(beta[:, SM_B + h:SM_B + h + 1])
            egs.append(jnp.exp(gch))
            decays.append(jnp.exp(jnp.where(incl, gch - gc_t[SM_A + h:SM_A + h + 1, :], -jnp.inf)))
    kbs = [k * bh for k, bh in zip(ks, bhs)]
    ms = [jnp.where(strict, _dot_nt_bf(kb, k) * d, 0.0) for kb, k, d in zip(kbs, ks, decays)]
    attns = [_dot_nt_bf(q, k) * d for q, k, d in zip(qs, ks, decays)]
    tinvs = _unit_lower_inverse(ms, C)
    sols = [_dot_bf(t, jnp.concatenate([v * bh, kb * eg], axis=-1))
            for t, v, bh, kb, eg in zip(tinvs, vs, bhs, kbs, egs)]
    ss = [s_ref[bi, h] for bi, h in pairs]
    v_news = [sol[:, :GDN_DV] - _dot_bf(sol[:, GDN_DV:], s) for sol, s in zip(sols, ss)]
    os_ = [_dot_bf(q * eg, s) + _dot_bf(a, vn) for q, eg, s, a, vn in zip(qs, egs, ss, attns, v_news)]
    for n, (bi, h) in enumerate(pairs):
        glast = gcs[n][C - 1:C, :]
        s_ref[bi, h] = ss[n] * jnp.exp(glast) + _dot_tn_bf(ks[n] * jnp.exp(glast - gcs[n]), v_news[n])
    for n, (bi, h) in enumerate(pairs):
        o = os_[n]
        o = o * lax.rsqrt(jnp.mean(o * o, -1, keepdims=True) + RMS_EPS) * ng_ref[...]
        zh = z_ref[bi, :, h * GDN_DV:(h + 1) * GDN_DV]
        y_ref[bi, :, h * GDN_DV:(h + 1) * GDN_DV] = o * _silu(zh)

    @pl.when(c_idx == n_c - 1)
    def _():
        snew_ref[...] = s_ref[...]


def _gdn(h3, l, conv_buf, s0, conv_w, a_log, dt_bias, norm_g):
    b = h3.shape[0]
    chunk = min(CHUNK, l)
    assert l % chunk == 0 and chunk % SUBLANES == 0 and chunk >= CONV_W - 1
    kern = functools.partial(_gdn_kernel, chunk=chunk)
    keep = CONV_W - 1
    lane_row = lambda vec, at: jnp.zeros((1, LANES), F32).at[0, at:at + vec.shape[0]].set(vec)
    gb = _pick_tile(b, GDN_BATCH_TILE)
    return pl.pallas_call(
        kern,
        grid=(b // gb, l // chunk),
        in_specs=[
            pl.BlockSpec((gb, chunk, CONV_CH), lambda i, c: (i, c, COL_QKV // CONV_CH)),
            pl.BlockSpec((gb, chunk, GDN_V), lambda i, c: (i, c, COL_Z // GDN_V)),
            pl.BlockSpec((gb, chunk, SMALL_W), lambda i, c: (i, c, COL_SMALL // SMALL_W)),
            pl.BlockSpec((gb, keep, CONV_CH), lambda i, c: (i, 0, 0)),
            pl.BlockSpec((gb, GDN_HEADS, GDN_DK, GDN_DV), lambda i, c: (i, 0, 0, 0)),
            pl.BlockSpec((CONV_W, CONV_CH), lambda i, c: (0, 0)),
            pl.BlockSpec((1, LANES), lambda i, c: (0, 0)),
            pl.BlockSpec((1, LANES), lambda i, c: (0, 0)),
            pl.BlockSpec((1, GDN_DV), lambda i, c: (0, 0)),
        ],
        out_specs=[
            pl.BlockSpec((gb, chunk, GDN_V), lambda i, c: (i, c, 0)),
            pl.BlockSpec((gb, GDN_HEADS, GDN_DK, GDN_DV), lambda i, c: (i, 0, 0, 0)),
            pl.BlockSpec((gb, keep, CONV_CH), lambda i, c: (i, 0, 0)),
        ],
        out_shape=[
            jax.ShapeDtypeStruct((b, l, GDN_V), F32),
            jax.ShapeDtypeStruct((b, GDN_HEADS, GDN_DK, GDN_DV), F32),
            jax.ShapeDtypeStruct((b, keep, CONV_CH), F32),
        ],
        scratch_shapes=[
            pltpu.VMEM((gb, TAIL_ROWS + chunk, CONV_CH), F32),
            pltpu.VMEM((gb, GDN_HEADS, GDN_DK, GDN_DV), F32),
        ],
        compiler_params=_cparams(("parallel", "arbitrary")),
        name="gdn",
    )(h3, h3, h3, conv_buf, s0, conv_w, lane_row(a_log, SM_A), lane_row(dt_bias, SM_A),
      norm_g.reshape(1, GDN_DV))


def _rope_tables(pos, rot, period):
    half = rot // 2
    inv_freq = ROPE_THETA ** (-(2.0 / rot) * jnp.arange(half, dtype=F32))
    ang = pos.astype(F32)[:, None] * inv_freq[None, :]
    cos, sin = jnp.cos(ang), jnp.sin(ang)
    n = pos.shape[0]
    rest = period - rot
    c = jnp.concatenate([cos, cos, jnp.ones((n, rest), F32)], -1)
    sa = jnp.concatenate([-sin, jnp.zeros((n, half + rest), F32)], -1)
    sb = jnp.concatenate([jnp.zeros((n, half), F32), sin, jnp.zeros((n, rest), F32)], -1)
    reps = LANES // period
    return jnp.stack([jnp.tile(c, (1, reps)), jnp.tile(sa, (1, reps)), jnp.tile(sb, (1, reps))], 0)


Q_SCALE_LOG2 = HEAD_DIM ** -0.5 * float(np.log2(np.e))


def _rope128(x, tab_ref, half):
    return (x * tab_ref[0] + pltpu.roll(x, LANES - half, 1) * tab_ref[1]
            + pltpu.roll(x, half, 1) * tab_ref[2])


def _prep_kernel(q_ref, qi_ref, k_ref, v_ref, sm_ref, tq_ref, ti_ref, lng_ref, lnb_ref,
                 qt_ref, qit_ref, wit_ref, ko_ref, kb_ref, kio_ref, kib_ref, vo_ref, vt_ref):
    hq = HEAD_DIM // 8
    hi = IDX_DIM // 8
    for h in range(ATT_HEADS):
        x = _rope128(q_ref[:, h * HEAD_DIM:(h + 1) * HEAD_DIM], tq_ref, hq) * Q_SCALE_LOG2
        qt_ref[h * HEAD_DIM:(h + 1) * HEAD_DIM, :] = x.T.astype(BF16)
    for h in range(ATT_KV_HEADS):
        x = _rope128(k_ref[:, h * HEAD_DIM:(h + 1) * HEAD_DIM], tq_ref, hq)
        ko_ref[:, h * HEAD_DIM:(h + 1) * HEAD_DIM] = x
        kb_ref[:, h * HEAD_DIM:(h + 1) * HEAD_DIM] = x.astype(BF16)
        v = v_ref[:, h * HEAD_DIM:(h + 1) * HEAD_DIM]
        vo_ref[:, h * HEAD_DIM:(h + 1) * HEAD_DIM] = v
        vt_ref[h * HEAD_DIM:(h + 1) * HEAD_DIM, :] = v.T.astype(BF16)
    for c in range(IDX_Q // LANES):
        x = _rope128(qi_ref[:, c * LANES:(c + 1) * LANES], ti_ref, hi)
        qit_ref[c * LANES:(c + 1) * LANES, :] = x.T.astype(BF16)
    sm = sm_ref[:, 0:LANES]
    lane = lax.broadcasted_iota(jnp.int32, sm.shape, 1)
    is_ki = lane < IDX_DIM
    mu = jnp.sum(jnp.where(is_ki, sm, 0.0), -1, keepdims=True) * (1.0 / IDX_DIM)
    xc = jnp.where(is_ki, sm - mu, 0.0)
    var = jnp.sum(xc * xc, -1, keepdims=True) * (1.0 / IDX_DIM)
    ki = xc * lax.rsqrt(var + LN_EPS) * lng_ref[...] + lnb_ref[...]
    ki = _rope128(ki, ti_ref, hi)[:, 0:IDX_DIM]
    kio_ref[...] = ki
    kib_ref[...] = ki.astype(BF16)
    wit_ref[...] = sm.T[SM_WI:SM_WI + IDX_HEADS, :] * (IDX_HEADS ** -0.5 * IDX_DIM ** -0.5)


def _prep(h3, pos, ln_g, ln_b):
    b, l, _ = h3.shape
    tm = _pick_tile(l, PREP_TM)
    tab_q = _rope_tables(pos, HEAD_DIM // 4, HEAD_DIM)
    tab_i = _rope_tables(pos, IDX_DIM // 4, IDX_DIM)
    lng = jnp.concatenate([ln_g, jnp.zeros((LANES - IDX_DIM,), F32)]).reshape(1, LANES)
    lnb = jnp.concatenate([ln_b, jnp.zeros((LANES - IDX_DIM,), F32)]).reshape(1, LANES)
    rows = lambda w, col: pl.BlockSpec((None, tm, w), lambda i, t: (i, t, col))
    cols = lambda w: pl.BlockSpec((None, w, tm), lambda i, t: (i, 0, t))
    return pl.pallas_call(
        _prep_kernel,
        grid=(b, l // tm),
        in_specs=[
            rows(ATT_Q, COL_Q // ATT_Q), rows(IDX_Q, COL_QI // IDX_Q), rows(ATT_KV, COL_K // ATT_KV),
            rows(ATT_KV, COL_V // ATT_KV), rows(SMALL_W, COL_SMALL // SMALL_W),
            pl.BlockSpec((3, tm, LANES), lambda i, t: (0, t, 0)),
            pl.BlockSpec((3, tm, LANES), lambda i, t: (0, t, 0)),
            pl.BlockSpec((1, LANES), lambda i, t: (0, 0)),
            pl.BlockSpec((1, LANES), lambda i, t: (0, 0)),
        ],
        out_specs=[cols(ATT_Q), cols(IDX_Q), cols(IDX_HEADS), rows(ATT_KV, 0), rows(ATT_KV, 0),
                   rows(IDX_DIM, 0), rows(IDX_DIM, 0), rows(ATT_KV, 0), cols(ATT_KV)],
        out_shape=[
            jax.ShapeDtypeStruct((b, ATT_Q, l), BF16),
            jax.ShapeDtypeStruct((b, IDX_Q, l), BF16),
            jax.ShapeDtypeStruct((b, IDX_HEADS, l), F32),
            jax.ShapeDtypeStruct((b, l, ATT_KV), F32),
            jax.ShapeDtypeStruct((b, l, ATT_KV), BF16),
            jax.ShapeDtypeStruct((b, l, IDX_DIM), F32),
            jax.ShapeDtypeStruct((b, l, IDX_DIM), BF16),
            jax.ShapeDtypeStruct((b, l, ATT_KV), F32),
            jax.ShapeDtypeStruct((b, ATT_KV, l), BF16),
        ],
        compiler_params=_cparams(("parallel", "parallel")),
        name="dsa_prep",
    )(h3, h3, h3, h3, h3, tab_q, tab_i, lng, lnb)


INT_MIN = -2 ** 31
INT_MAX = 2 ** 31 - 1
NEG_INF_KEY = INT_MIN + 0x7FFFFF
NEG_BIG = -1e30
HEAD_PAIRS = IDX_HEADS // 2
COUNT_CHAINS = 8
DENOM_ROWS = 16


def _dsa_kernel(qt_ref, qit_ref, wit_ref, ki_ref, k_ref, vt_ref, o_ref,
                key_ref, qs_ref, qip_ref, m_ref, acc_ref, cm_ref,
                *, tq, kb, l_true, pos0, topk, idx_bits):
    q0 = pos0 + pl.program_id(1) * tq
    qpos = q0 + lax.broadcasted_iota(jnp.int32, (1, tq), 1)
    lim = jnp.minimum((qpos // CHUNK + 1) * CHUNK, l_true)
    lim_max = jnp.minimum(((q0 + tq - 1) // CHUNK + 1) * CHUNK, l_true)
    nkb = (lim_max + kb - 1) // kb
    key_iota = lax.broadcasted_iota(jnp.int32, (kb, tq), 0)
    wit = wit_ref[...]

    for p in range(HEAD_PAIRS):
        for u in range(2):
            hh = 2 * p + u
            qip_ref[p, :, u * tq:(u + 1) * tq] = qit_ref[hh * IDX_DIM:(hh + 1) * IDX_DIM, :]
    for g in range(ATT_KV_HEADS):
        for r in range(REP):
            hh = g * REP + r
            qs_ref[g, :, r * tq:(r + 1) * tq] = qt_ref[hh * HEAD_DIM:(hh + 1) * HEAD_DIM, :]

    def score_blk(j, carry):
        off = pl.multiple_of(j * kb, kb)
        ki = ki_ref[pl.ds(off, kb), :]
        acc = jnp.zeros((kb, tq), F32)
        for p in range(HEAD_PAIRS):
            s2 = _dot(ki, qip_ref[p])
            acc = (acc + wit[2 * p:2 * p + 1, :] * jnp.maximum(s2[:, :tq], 0.0)
                   + wit[2 * p + 1:2 * p + 2, :] * jnp.maximum(s2[:, tq:], 0.0))
        score = jnp.where(off + key_iota < lim, acc, -jnp.inf)
        bits = pltpu.bitcast(score, jnp.int32)
        key_ref[pl.ds(off, kb), :] = bits ^ ((bits >> 31) & INT_MAX)
        return carry

    lax.fori_loop(0, nkb, score_blk, 0)

    def count(pred_fn):
        def blk(j, c):
            off = pl.multiple_of(j * kb, kb)
            part = jnp.where(pred_fn(key_ref[pl.ds(off, kb), :], off), 1.0, 0.0)
            return c + jnp.sum(part.reshape(kb // (COUNT_CHAINS * SUBLANES), COUNT_CHAINS * SUBLANES, tq), axis=0)
        c = lax.fori_loop(0, nkb, blk, jnp.zeros((COUNT_CHAINS * SUBLANES, tq), F32))
        return jnp.sum(c, axis=0, keepdims=True)

    def bit_cond(state):
        t, _, unsettled = state
        return (t < 32) & (jnp.max(unsettled) > 0)

    def bit_step(state):
        t, cur, unsettled = state
        cand_u = cur | lax.shift_left(jnp.int32(1), 31 - t)
        cand_s = cand_u ^ INT_MIN
        cnt = count(lambda kk, off: kk >= cand_s)
        take = (cnt >= topk) & (unsettled > 0)
        return t + 1, jnp.where(take, cand_u, cur), jnp.where(take & (cnt == topk), 0, unsettled)

    _, cur, unsettled = lax.while_loop(
        bit_cond, bit_step, (jnp.int32(0), jnp.zeros((1, tq), jnp.int32), jnp.ones((1, tq), jnp.int32)))
    thr = cur ^ INT_MIN

    cm_ref[...] = jnp.full((1, tq), INT_MAX, jnp.int32)

    @pl.when(jnp.max(jnp.where((unsettled > 0) & (thr > NEG_INF_KEY), 1, 0)) > 0)
    def _():
        need = topk - count(lambda kk, off: kk > thr)

        def idx_step(t, cm):
            cand = cm | lax.shift_left(jnp.int32(1), idx_bits - 1 - t)
            before = count(lambda kk, off: (kk == thr) & (off + key_iota < cand))
            return jnp.where(before < need, cand, cm)
        cm_ref[...] = lax.fori_loop(0, idx_bits, idx_step, jnp.zeros((1, tq), jnp.int32))

    cm = cm_ref[...]

    m_ref[...] = jnp.full(m_ref.shape, NEG_BIG, F32)
    acc_ref[...] = jnp.zeros(acc_ref.shape, F32)
    groups = range(ATT_KV_HEADS)
    ones_rows = jnp.ones((DENOM_ROWS, kb), BF16)

    def attn_blk(j):
        off = pl.multiple_of(j * kb, kb)
        kk = key_ref[pl.ds(off, kb), :]
        kpos = off + key_iota
        sel = ((kk > thr) | ((kk == thr) & (kpos <= cm))) & (kpos < lim)
        bias = jnp.where(sel, 0.0, NEG_BIG)
        bias = jnp.concatenate([bias] * REP, axis=1)
        logits = [_dot(k_ref[pl.ds(off, kb), g * HEAD_DIM:(g + 1) * HEAD_DIM], qs_ref[g]) + bias for g in groups]
        m_old = [m_ref[g] for g in groups]
        m_new = [jnp.maximum(m_old[g], jnp.max(logits[g], axis=0, keepdims=True)) for g in groups]
        ps = [jnp.exp2(logits[g] - m_new[g]).astype(BF16) for g in groups]
        alphas = [jnp.exp2(m_old[g] - m_new[g]) for g in groups]
        pvs = [_dot(jnp.concatenate([vt_ref[g * HEAD_DIM:(g + 1) * HEAD_DIM, pl.ds(off, kb)], ones_rows], axis=0),
                    ps[g]) for g in groups]
        for g in groups:
            acc_ref[g] = alphas[g] * acc_ref[g] + pvs[g]
            m_ref[g] = m_new[g]

    def attn_pair(jp, carry):
        attn_blk(2 * jp)
        attn_blk(2 * jp + 1)
        return carry

    lax.fori_loop(0, nkb // 2, attn_pair, 0)

    @pl.when(nkb % 2 == 1)
    def _():
        attn_blk(nkb - 1)

    for g in groups:
        o_t = acc_ref[g, 0:HEAD_DIM, :] / acc_ref[g, HEAD_DIM:HEAD_DIM + 1, :]
        for r in range(REP):
            hh = g * REP + r
            o_ref[:, hh * HEAD_DIM:(hh + 1) * HEAD_DIM] = o_t[:, r * tq:(r + 1) * tq].T


def _dsa(qt, qit, wit, ki, k, vt, l_true, pos0):
    b, _, t = qt.shape
    lk = k.shape[1]
    tq, kb = DSA_Q_TILE, DSA_KEY_BLOCK
    assert t % tq == 0 and lk % kb == 0 and tq == HEAD_DIM
    topk = min(TOPK_MAX, l_true // 4)
    kern = functools.partial(_dsa_kernel, tq=tq, kb=kb, l_true=l_true, pos0=pos0, topk=topk,
                             idx_bits=int(lk).bit_length())
    return pl.pallas_call(
        kern,
        grid=(b, t // tq),
        in_specs=[
            pl.BlockSpec((None, ATT_Q, tq), lambda i, t_: (i, 0, t_)),
            pl.BlockSpec((None, IDX_Q, tq), lambda i, t_: (i, 0, t_)),
            pl.BlockSpec((None, IDX_HEADS, tq), lambda i, t_: (i, 0, t_)),
            pl.BlockSpec((None, lk, IDX_DIM), lambda i, t_: (i, 0, 0)),
            pl.BlockSpec((None, lk, ATT_KV), lambda i, t_: (i, 0, 0)),
            pl.BlockSpec((None, ATT_KV, lk), lambda i, t_: (i, 0, 0)),
        ],
        out_specs=pl.BlockSpec((None, tq, ATT_Q), lambda i, t_: (i, t_, 0)),
        out_shape=jax.ShapeDtypeStruct((b, t, ATT_Q), F32),
        scratch_shapes=[
            pltpu.VMEM((lk, tq), jnp.int32),
            pltpu.VMEM((ATT_KV_HEADS, HEAD_DIM, REP * tq), BF16),
            pltpu.VMEM((HEAD_PAIRS, IDX_DIM, 2 * tq), BF16),
            pltpu.VMEM((ATT_KV_HEADS, 1, REP * tq), F32),
            pltpu.VMEM((ATT_KV_HEADS, HEAD_DIM + DENOM_ROWS, REP * tq), F32),
            pltpu.VMEM((1, tq), jnp.int32),
        ],
        compiler_params=_cparams(("parallel", "arbitrary")),
        name="dsa",
    )(qt, qit, wit, ki, k, vt)


def _layer_norm(x, g, b):
    mu = jnp.mean(x, -1, keepdims=True)
    xc = x - mu
    var = jnp.mean(xc * xc, -1, keepdims=True)
    return xc * lax.rsqrt(var + LN_EPS) * g + b


def _merge_kernel(ya_ref, yb_ref, ga_ref, gb_ref, x_ref, wa_ref, wb_ref, wo_ref, g_ref, b_ref, o_ref):
    pa = _dot(ya_ref[...].astype(BF16), wa_ref[...])
    pb = _dot(yb_ref[...].astype(BF16), wb_ref[...])
    merged = _sigmoid(ga_ref[...]) * pa + _sigmoid(gb_ref[...]) * pb
    y = DN_ALPHA * x_ref[...] + _dot(merged.astype(BF16), wo_ref[...])
    o_ref[...] = _layer_norm(y, g_ref[...], b_ref[...])


def _merge(ya, yb, h3, x, wa, wb, wo, g, b):
    bsz, l, _ = x.shape
    tm = _pick_tile(l, TOKEN_TM)
    row = lambda c: pl.BlockSpec((None, tm, D_MODEL), lambda i, t: (i, t, c))
    full = lambda shp: pl.BlockSpec(shp, lambda i, t: (0, 0))
    return pl.pallas_call(
        _merge_kernel,
        grid=(bsz, l // tm),
        in_specs=[row(0), row(0), row(COL_GA // D_MODEL), row(COL_GB // D_MODEL), row(0),
                  full((GDN_V, D_MODEL)), full((ATT_Q, D_MODEL)), full((D_MODEL, D_MODEL)),
                  full((1, D_MODEL)), full((1, D_MODEL))],
        out_specs=row(0),
        out_shape=jax.ShapeDtypeStruct((bsz, l, D_MODEL), F32),
        compiler_params=_cparams(("parallel", "parallel")),
        name="merge_ln1",
    )(ya, yb, h3, h3, x, wa, wb, wo, g.reshape(1, D_MODEL), b.reshape(1, D_MODEL))


GROUP_SIZE = N_EXPERTS // N_GROUPS


def _first_max(cur, rows, n_rows):
    m = jnp.max(cur, axis=0, keepdims=True)
    idx = jnp.min(jnp.where(cur == m, rows, n_rows), axis=0, keepdims=True)
    return m, idx


def _router_kernel(x_ref, wr_ref, rb_ref, ws1_ref, ws3_ref, ws2_ref,
                   eidx_ref, wts_ref, rank_ref, cnt_ref, base_ref, run_ref):
    @pl.when(pl.program_id(0) == 0)
    def _():
        run_ref[...] = jnp.zeros(run_ref.shape, F32)

    x = x_ref[...]
    tm = x.shape[0]
    logits = _dot_nt_hi(wr_ref[...], x)
    scores = _sigmoid(logits)
    biased = scores + rb_ref[...]
    neg = -jnp.inf

    rows_g = lax.broadcasted_iota(jnp.int32, (GROUP_SIZE, tm), 0)
    gs = []
    for g in range(N_GROUPS):
        blk = biased[g * GROUP_SIZE:(g + 1) * GROUP_SIZE, :]
        m1, i1 = _first_max(blk, rows_g, GROUP_SIZE)
        m2 = jnp.max(jnp.where(rows_g == i1, neg, blk), axis=0, keepdims=True)
        gs.append(m1 + m2)
    cur = jnp.concatenate(gs, axis=0)
    rows_n = lax.broadcasted_iota(jnp.int32, (N_GROUPS, tm), 0)
    gsel = jnp.zeros((N_GROUPS, tm), F32)
    for _ in range(TOPK_GROUPS):
        _, ig = _first_max(cur, rows_n, N_GROUPS)
        hit = rows_n == ig
        gsel = jnp.where(hit, 1.0, gsel)
        cur = jnp.where(hit, neg, cur)
    gexp = jnp.concatenate([jnp.broadcast_to(gsel[g:g + 1, :], (GROUP_SIZE, tm)) for g in range(N_GROUPS)], axis=0)
    cur = jnp.where(gexp > 0.0, biased, neg)

    rows_e = lax.broadcasted_iota(jnp.int32, (N_EXPERTS, tm), 0)
    es, ws, hits = [], [], []
    for _ in range(EXPERT_TOPK):
        _, ie = _first_max(cur, rows_e, N_EXPERTS)
        hit = rows_e == ie
        es.append(ie)
        hits.append(hit)
        ws.append(jnp.sum(jnp.where(hit, scores, 0.0), axis=0, keepdims=True))
        cur = jnp.where(hit, neg, cur)
    w = jnp.concatenate(ws, axis=0)
    eidx_ref[...] = jnp.concatenate(es, axis=0)
    wts_ref[...] = w / jnp.sum(w, axis=0, keepdims=True) * ROUTED_SCALE

    assigned = jnp.zeros((N_EXPERTS, tm), F32)
    for hit in hits:
        assigned = jnp.where(hit, 1.0, assigned)
    earlier = (lax.broadcasted_iota(jnp.int32, (tm, tm), 0) < lax.broadcasted_iota(jnp.int32, (tm, tm), 1))
    before = run_ref[...] + _dot(assigned.astype(BF16), earlier.astype(BF16))
    rank_ref[...] = jnp.concatenate(
        [jnp.sum(jnp.where(hit, before, 0.0), axis=0, keepdims=True) for hit in hits], axis=0).astype(jnp.int32)
    run_ref[...] = run_ref[...] + jnp.sum(assigned, axis=1, keepdims=True)
    cnt_ref[...] = run_ref[...].astype(jnp.int32)

    xb = x.astype(BF16)
    hs = _silu(_dot(xb, ws1_ref[...])) * _dot(xb, ws3_ref[...])
    base_ref[...] = DN_ALPHA * x + _dot(hs.astype(BF16), ws2_ref[...])


def _router(x1, wr_t, rbias, ws1, ws3, ws2, tm):
    t = x1.shape[0]
    full = lambda shp: pl.BlockSpec(shp, lambda i: (0, 0))
    return pl.pallas_call(
        _router_kernel,
        grid=(t // tm,),
        in_specs=[pl.BlockSpec((tm, D_MODEL), lambda i: (i, 0)),
                  full((N_EXPERTS, D_MODEL)), full((N_EXPERTS, 1)),
                  full((D_MODEL, SHARED_DIM)), full((D_MODEL, SHARED_DIM)), full((SHARED_DIM, D_MODEL))],
        out_specs=[pl.BlockSpec((EXPERT_TOPK, tm), lambda i: (0, i)),
                   pl.BlockSpec((EXPERT_TOPK, tm), lambda i: (0, i)),
                   pl.BlockSpec((EXPERT_TOPK, tm), lambda i: (0, i)),
                   full((N_EXPERTS, 1)),
                   pl.BlockSpec((tm, D_MODEL), lambda i: (i, 0))],
        out_shape=[jax.ShapeDtypeStruct((EXPERT_TOPK, t), jnp.int32),
                   jax.ShapeDtypeStruct((EXPERT_TOPK, t), F32),
                   jax.ShapeDtypeStruct((EXPERT_TOPK, t), jnp.int32),
                   jax.ShapeDtypeStruct((N_EXPERTS, 1), jnp.int32),
                   jax.ShapeDtypeStruct((t, D_MODEL), F32)],
        scratch_shapes=[pltpu.VMEM((N_EXPERTS, 1), F32)],
        compiler_params=_cparams(("arbitrary",)),
        name="router_shared",
    )(x1, wr_t, rbias.reshape(N_EXPERTS, 1), ws1, ws3, ws2)


def _block_plan(counts, n_blocks):
    padded = (counts + MOE_ROWS - 1) // MOE_ROWS * MOE_ROWS
    pad_end = jnp.cumsum(padded)
    blk_first = jnp.arange(n_blocks, dtype=jnp.int32) * MOE_ROWS
    blk_exp = jnp.minimum(jnp.sum((pad_end[None, :] <= blk_first[:, None]).astype(jnp.int32), axis=1), N_EXPERTS - 1)
    return pad_end - padded, jnp.maximum(pad_end - MOE_ROWS, 0), blk_exp, pad_end[-1:] // MOE_ROWS


def _pos_kernel(e_ref, r_ref, ps_ref, pos_ref):
    tm = e_ref.shape[1]
    rows_e = lax.broadcasted_iota(jnp.int32, (N_EXPERTS, tm), 0)
    first = [jnp.sum(jnp.where(rows_e == e_ref[j:j + 1, :], ps_ref[...], 0), axis=0, keepdims=True)
             for j in range(EXPERT_TOPK)]
    pos_ref[...] = jnp.concatenate(first, axis=0) + r_ref[...]


def _positions(eidx_t, rank_t, pad_start, tm):
    t = eidx_t.shape[1]
    blk = pl.BlockSpec((EXPERT_TOPK, tm), lambda i: (0, i))
    return pl.pallas_call(
        _pos_kernel,
        grid=(t // tm,),
        in_specs=[blk, blk, pl.BlockSpec((N_EXPERTS, 1), lambda i: (0, 0))],
        out_specs=blk,
        out_shape=jax.ShapeDtypeStruct((EXPERT_TOPK, t), jnp.int32),
        compiler_params=_cparams(("parallel",)),
        name="moe_positions",
    )(eidx_t, rank_t, pad_start.reshape(N_EXPERTS, 1))


def _scatter_kernel(last_ref, nu_ref, pos_ref, x_ref, rows_ref, zero_ref, sem, zsem):
    tm = x_ref.shape[0]
    n_blocks = rows_ref.shape[0] // MOE_ROWS

    @pl.when(pl.program_id(0) == 0)
    def _():
        zero_ref[...] = jnp.zeros(zero_ref.shape, F32)

        def zero_copy(first_row):
            first = pl.multiple_of(first_row, MOE_ROWS)
            return pltpu.make_async_copy(zero_ref, rows_ref.at[pl.ds(first, MOE_ROWS), :], zsem)

        def each(fn):
            def expert_block(e, carry):
                fn(zero_copy(last_ref[e]))
                return carry

            def unused_block(blk, carry):
                fn(zero_copy(blk * MOE_ROWS))
                return carry

            lax.fori_loop(0, N_EXPERTS, expert_block, 0)
            lax.fori_loop(nu_ref[0], n_blocks, unused_block, 0)

        each(lambda copy: copy.start())
        each(lambda copy: copy.wait())

    def row_copy(t, j):
        return pltpu.make_async_copy(x_ref.at[pl.ds(t, 1), :], rows_ref.at[pl.ds(pos_ref[j, t], 1), :], sem)

    def issue(t, carry):
        for j in range(EXPERT_TOPK):
            row_copy(t, j).start()
        return carry

    lax.fori_loop(0, tm, issue, 0)
    for j in range(EXPERT_TOPK):
        pltpu.make_async_copy(x_ref, rows_ref.at[pl.ds(0, tm), :], sem).wait()


def _scatter_rows(last_block_row, n_used, pos_t, x1, n_rows, tm):
    t = x1.shape[0]
    grid_spec = pltpu.PrefetchScalarGridSpec(
        num_scalar_prefetch=2,
        grid=(t // tm,),
        in_specs=[pl.BlockSpec((EXPERT_TOPK, tm), lambda i, lr, nu: (0, i), memory_space=pltpu.SMEM),
                  pl.BlockSpec((tm, D_MODEL), lambda i, lr, nu: (i, 0))],
        out_specs=pl.BlockSpec(memory_space=pl.ANY),
        scratch_shapes=[pltpu.VMEM((MOE_ROWS, D_MODEL), F32), pltpu.SemaphoreType.DMA(()),
                        pltpu.SemaphoreType.DMA(())],
    )
    return pl.pallas_call(
        _scatter_kernel,
        grid_spec=grid_spec,
        out_shape=jax.ShapeDtypeStruct((n_rows, D_MODEL), F32),
        compiler_params=_cparams(("arbitrary",)),
        name="moe_scatter",
    )(last_block_row, n_used, pos_t, x1)


def _expert_kernel(be_ref, nu_ref, x_ref, w1_ref, w3_ref, w2_ref, o_ref, w1b_ref, w3b_ref, w2b_ref):
    i = pl.program_id(0)
    prev = be_ref[jnp.maximum(i - 1, 0)]

    @pl.when((i == 0) | (be_ref[i] != prev))
    def _():
        w1b_ref[...] = w1_ref[...].astype(BF16)
        w3b_ref[...] = w3_ref[...].astype(BF16)
        w2b_ref[...] = w2_ref[...].astype(BF16)

    @pl.when(i < nu_ref[0])
    def _():
        x = x_ref[...].astype(BF16)
        hmid = _silu(_dot(x, w1b_ref[...])) * _dot(x, w3b_ref[...])
        o_ref[...] = _dot(hmid.astype(BF16), w2b_ref[...])

    @pl.when(i >= nu_ref[0])
    def _():
        o_ref[...] = jnp.zeros(o_ref.shape, F32)


def _experts(xg, blk_exp, n_used, w1, w3, w2):
    rows = xg.shape[0]
    n_blocks = rows // MOE_ROWS
    grid_spec = pltpu.PrefetchScalarGridSpec(
        num_scalar_prefetch=2,
        grid=(n_blocks,),
        in_specs=[
            pl.BlockSpec((MOE_ROWS, D_MODEL), lambda i, be, nu: (jnp.minimum(i, nu[0] - 1), 0)),
            pl.BlockSpec((None, D_MODEL, EXPERT_DIM), lambda i, be, nu: (be[i], 0, 0)),
            pl.BlockSpec((None, D_MODEL, EXPERT_DIM), lambda i, be, nu: (be[i], 0, 0)),
            pl.BlockSpec((None, EXPERT_DIM, D_MODEL), lambda i, be, nu: (be[i], 0, 0)),
        ],
        out_specs=pl.BlockSpec((MOE_ROWS, D_MODEL), lambda i, be, nu: (i, 0)),
        scratch_shapes=[pltpu.VMEM((D_MODEL, EXPERT_DIM), BF16), pltpu.VMEM((D_MODEL, EXPERT_DIM), BF16),
                        pltpu.VMEM((EXPERT_DIM, D_MODEL), BF16)],
    )
    return pl.pallas_call(
        _expert_kernel,
        grid_spec=grid_spec,
        out_shape=jax.ShapeDtypeStruct((rows, D_MODEL), F32),
        compiler_params=_cparams(("arbitrary",)),
        name="experts",
    )(blk_exp, n_used, xg, w1, w3, w2)


def _combine_kernel(pos_ref, pos_next_ref, y_ref, w_ref, base_ref, g_ref, b_ref, o_ref, gbuf_ref, sem):
    tm = base_ref.shape[0]
    i = pl.program_id(0)
    slot = lax.rem(i, 2)

    def issue_tile(p_ref, s):
        def issue(t, carry):
            for j in range(EXPERT_TOPK):
                pltpu.make_async_copy(y_ref.at[pl.ds(p_ref[j, t], 1), :], gbuf_ref.at[s, j, pl.ds(t, 1), :],
                                      sem.at[s]).start()
            return carry
        lax.fori_loop(0, tm, issue, 0)

    @pl.when(i == 0)
    def _():
        issue_tile(pos_ref, 0)

    @pl.when(i + 1 < pl.num_programs(0))
    def _():
        issue_tile(pos_next_ref, 1 - slot)

    for j in range(EXPERT_TOPK):
        pltpu.make_async_copy(y_ref.at[pl.ds(0, tm), :], gbuf_ref.at[slot, j], sem.at[slot]).wait()

    acc = base_ref[...]
    w = w_ref[...]
    for j in range(EXPERT_TOPK):
        acc = acc + w[:, j:j + 1] * gbuf_ref[slot, j]
    o_ref[...] = _layer_norm(acc, g_ref[...], b_ref[...])


def _combine(pos_t, y_rows, wts, base, g, b, tm):
    t = base.shape[0]
    n_tiles = t // tm
    return pl.pallas_call(
        _combine_kernel,
        grid=(n_tiles,),
        in_specs=[pl.BlockSpec((EXPERT_TOPK, tm), lambda i: (0, i), memory_space=pltpu.SMEM),
                  pl.BlockSpec((EXPERT_TOPK, tm), lambda i: (0, jnp.minimum(i + 1, n_tiles - 1)),
                               memory_space=pltpu.SMEM),
                  pl.BlockSpec(memory_space=pl.ANY),
                  pl.BlockSpec((tm, EXPERT_TOPK), lambda i: (i, 0)),
                  pl.BlockSpec((tm, D_MODEL), lambda i: (i, 0)),
                  pl.BlockSpec((1, D_MODEL), lambda i: (0, 0)),
                  pl.BlockSpec((1, D_MODEL), lambda i: (0, 0))],
        out_specs=pl.BlockSpec((tm, D_MODEL), lambda i: (i, 0)),
        out_shape=jax.ShapeDtypeStruct((t, D_MODEL), F32),
        scratch_shapes=[pltpu.VMEM((2, EXPERT_TOPK, tm, D_MODEL), F32), pltpu.SemaphoreType.DMA((2,))],
        compiler_params=_cparams(("arbitrary",)),
        name="combine_ln2",
    )(pos_t, pos_t, y_rows, wts, base, g.reshape(1, D_MODEL), b.reshape(1, D_MODEL))


def _moe(x1, prm):
    t = x1.shape[0]
    tm = _pick_tile(t, TOKEN_TM)
    eidx_t, wts_t, rank_t, counts, base = _router(x1, prm["w_router_t"], prm["router_bias"], prm["ws1"],
                                                  prm["ws3"], prm["ws2"], tm)
    n_blocks = -(-t * EXPERT_TOPK // MOE_ROWS) + N_EXPERTS
    pad_start, last_block_row, blk_exp, n_used = _block_plan(counts.reshape(N_EXPERTS), n_blocks)
    pos_t = _positions(eidx_t, rank_t, pad_start, tm)
    x_rows = _scatter_rows(last_block_row, n_used, pos_t, x1, n_blocks * MOE_ROWS, tm)
    y_rows = _experts(x_rows, blk_exp, n_used, prm["w1"], prm["w3"], prm["w2"])
    return _combine(pos_t, y_rows, wts_t.T, base, prm["ln2_g"], prm["ln2_b"], _pick_tile(t, COMBINE_TM))


def _mixer(x, pos0, conv_buf, s0, k_past, v_past, ik_past, prm):
    b, l, _ = x.shape
    lp = _round_up(l, DSA_Q_TILE)
    xp = x if lp == l else jnp.pad(x, ((0, 0), (0, lp - l), (0, 0)))
    h3 = _project(xp.reshape(b * lp, D_MODEL), prm["w_in"]).reshape(b, lp, H_COLS)

    ya, s_new, buf_new = _gdn(h3, l, conv_buf, s0, prm["conv_w"], prm["a_log"], prm["dt_bias"], prm["gdn_norm_g"])

    pos = pos0 + jnp.arange(lp, dtype=jnp.int32)
    qt, qit, wit, k_new, k_bf, ki_new, ki_bf, v_new, vt = _prep(h3, pos, prm["idx_k_ln_g"], prm["idx_k_ln_b"])
    if k_past is None:
        l_all = l
        k_all, ki_all, vt_all = k_bf, ki_bf, vt
    else:
        past = k_past.shape[1]
        l_all = past + l
        k_all = jnp.concatenate([k_past.reshape(b, past, ATT_KV).astype(BF16), k_bf[:, :l]], 1)
        ki_all = jnp.concatenate([ik_past.astype(BF16), ki_bf[:, :l]], 1)
        vt_all = jnp.concatenate([jnp.swapaxes(v_past.reshape(b, past, ATT_KV), 1, 2).astype(BF16), vt[:, :, :l]], 2)
    lk = _round_up(l_all, DSA_KEY_BLOCK)
    if lk != l_all:
        k_all = jnp.pad(k_all, ((0, 0), (0, lk - l_all), (0, 0)))
        ki_all = jnp.pad(ki_all, ((0, 0), (0, lk - l_all), (0, 0)))
        vt_all = jnp.pad(vt_all, ((0, 0), (0, 0), (0, lk - l_all)))
    yb = _dsa(qt, qit, wit, ki_all, k_all, vt_all, l_all, pos0)

    x1 = _merge(ya, yb, h3, x, prm["w_o_gdn"], prm["w_o_dsa"], prm["w_out"], prm["ln1_g"], prm["ln1_b"])
    state = (k_new[:, :l].reshape(b, l, ATT_KV_HEADS, HEAD_DIM), v_new[:, :l].reshape(b, l, ATT_KV_HEADS, HEAD_DIM),
             ki_new[:, :l], s_new, buf_new)
    return x1, state


def kernel(x_prompt, x_sample, cache_k, cache_v, cache_idx_k, state_gdn, state_conv, w_in, conv_w, a_log, dt_bias, gdn_norm_g, w_o_gdn, idx_k_ln_g, idx_k_ln_b, w_o_dsa, w_out, ln1_g, ln1_b, w_router, router_bias, w1, w3, w2, ws1, ws3, ws2, ln2_g, ln2_b):
    assert w_in.shape[0] == DEPTH == 1
    bp, lp_, _ = x_prompt.shape
    bs, ls_, _ = x_sample.shape
    past = cache_k.shape[2]
    prm = dict(
        w_in=_repack_w_in(w_in[0]).astype(BF16), conv_w=conv_w[0], a_log=a_log[0], dt_bias=dt_bias[0],
        gdn_norm_g=gdn_norm_g[0], w_o_gdn=w_o_gdn[0].astype(BF16), idx_k_ln_g=idx_k_ln_g[0],
        idx_k_ln_b=idx_k_ln_b[0], w_o_dsa=w_o_dsa[0].astype(BF16), w_out=w_out[0].astype(BF16),
        ln1_g=ln1_g[0], ln1_b=ln1_b[0], w_router_t=w_router[0].T, router_bias=router_bias[0],
        w1=w1[0], w3=w3[0], w2=w2[0], ws1=ws1[0].astype(BF16), ws3=ws3[0].astype(BF16),
        ws2=ws2[0].astype(BF16), ln2_g=ln2_g[0], ln2_b=ln2_b[0])
    conv0 = jnp.zeros((bp, CONV_W - 1, CONV_CH), F32)
    s0 = jnp.zeros((bp, GDN_HEADS, GDN_DK, GDN_DV), F32)
    x1p, sp = _mixer(x_prompt, 0, conv0, s0, None, None, None, prm)
    x1s, ss = _mixer(x_sample, past, state_conv[0], state_gdn[0], cache_k[0], cache_v[0], cache_idx_k[0], prm)
    tp, ts = bp * lp_, bs * ls_
    y = _moe(jnp.concatenate([x1p.reshape(tp, D_MODEL), x1s.reshape(ts, D_MODEL)], 0), prm)
    yp = y[:tp].reshape(bp, lp_, D_MODEL)
    ys = y[tp:].reshape(bs, ls_, D_MODEL)
    return (yp, ys) + tuple(a[None] for a in sp) + tuple(a[None] for a in ss)
```

```python
import functools

import jax
import jax.numpy as jnp
import numpy as np
from jax import lax
from jax.experimental import pallas as pl
from jax.experimental.pallas import tpu as pltpu

F32 = jnp.float32
BF16 = jnp.bfloat16

D_MODEL = 1024
CHUNK = 64
GDN_HEADS = 8
GDN_DK = 128
GDN_DV = 128
CONV_W = 4
ATT_HEADS = 8
ATT_KV_HEADS = 2
HEAD_DIM = 128
IDX_HEADS = 16
IDX_DIM = 64
TOPK_MAX = 256
ROPE_THETA = 500000.0
N_EXPERTS = 256
EXPERT_TOPK = 8
N_GROUPS = 8
TOPK_GROUPS = 4
EXPERT_DIM = 256
SHARED_DIM = 256
ROUTED_SCALE = 2.5
DEPTH = 1
DN_ALPHA = (2.0 * DEPTH) ** 0.25
LN_EPS = 1e-5
RMS_EPS = 1e-6

GDN_QK = GDN_HEADS * GDN_DK
GDN_V = GDN_HEADS * GDN_DV
CONV_CH = 2 * GDN_QK + GDN_V
ATT_Q = ATT_HEADS * HEAD_DIM
ATT_KV = ATT_KV_HEADS * HEAD_DIM
IDX_Q = IDX_HEADS * IDX_DIM
REP = ATT_HEADS // ATT_KV_HEADS

LANES = 128
SUBLANES = 8
VMEM_LIMIT = 56 * 1024 * 1024

PROJ_TM = 1024
PROJ_TN = 1024
PREP_TM = 512
DSA_Q_TILE = 128
DSA_KEY_BLOCK = 512
TOKEN_TM = 256
COMBINE_TM = 128
GDN_BATCH_TILE = 4
MOE_ROWS = 256

COL_QKV = 0
COL_Z = COL_QKV + CONV_CH
COL_Q = COL_Z + GDN_V
COL_QI = COL_Q + ATT_Q
COL_GA = COL_QI + IDX_Q
COL_GB = COL_GA + D_MODEL
COL_K = COL_GB + D_MODEL
COL_V = COL_K + ATT_KV
COL_SMALL = COL_V + ATT_KV
SMALL_W = 512
SM_A = IDX_DIM
SM_B = SM_A + GDN_HEADS
SM_WI = SM_B + GDN_HEADS
H_COLS = COL_SMALL + SMALL_W


def _cparams(sem):
    return pltpu.CompilerParams(dimension_semantics=sem, vmem_limit_bytes=VMEM_LIMIT)


def _dot(a, b):
    return jnp.dot(a, b, preferred_element_type=F32)


def _dot_bf(a, b):
    return jnp.dot(a.astype(BF16), b.astype(BF16), preferred_element_type=F32)


def _dot_hi(a, b):
    return jnp.dot(a, b, precision=lax.Precision.HIGHEST, preferred_element_type=F32)


def _dot_nt_hi(a, b):
    return lax.dot_general(a, b, (((1,), (1,)), ((), ())), precision=lax.Precision.HIGHEST,
                           preferred_element_type=F32)


def _dot_nt_bf(a, b):
    return lax.dot_general(a.astype(BF16), b.astype(BF16), (((1,), (1,)), ((), ())),
                           preferred_element_type=F32)


def _dot_tn_bf(a, b):
    return lax.dot_general(a.astype(BF16), b.astype(BF16), (((0,), (0,)), ((), ())),
                           preferred_element_type=F32)


def _sigmoid(x):
    return 1.0 / (1.0 + jnp.exp(-x))


def _silu(x):
    return x * _sigmoid(x)


def _round_up(n, m):
    return -(-n // m) * m


def _pick_tile(n, pref):
    t = min(n, pref)
    assert n % t == 0
    return t


def _repack_w_in(w_in):
    sizes = (CONV_CH, GDN_V, GDN_HEADS, GDN_HEADS, ATT_Q, ATT_KV, ATT_KV, IDX_Q, IDX_DIM, IDX_HEADS,
             D_MODEL, D_MODEL)
    offs = np.concatenate([[0], np.cumsum(sizes)])
    (p_qkv, p_z, p_a, p_b, p_q, p_k, p_v, p_qi, p_ki, p_wi, p_ga, p_gb) = [
        w_in[:, offs[i]:offs[i + 1]] for i in range(len(sizes))]
    pad = jnp.zeros((w_in.shape[0], SMALL_W - (SM_WI + IDX_HEADS)), w_in.dtype)
    return jnp.concatenate([p_qkv, p_z, p_q, p_qi, p_ga, p_gb, p_k, p_v, p_ki, p_a, p_b, p_wi, pad], axis=1)


def _proj_kernel(x_ref, w_ref, o_ref):
    o_ref[...] = _dot(x_ref[...].astype(BF16), w_ref[...])


def _project(x2d, w_bf):
    t, d = x2d.shape
    n = w_bf.shape[1]
    tm, tn = _pick_tile(t, PROJ_TM), _pick_tile(n, PROJ_TN)
    return pl.pallas_call(
        _proj_kernel,
        grid=(t // tm, n // tn),
        in_specs=[pl.BlockSpec((tm, d), lambda i, j: (i, 0)),
                  pl.BlockSpec((d, tn), lambda i, j: (0, j))],
        out_specs=pl.BlockSpec((tm, tn), lambda i, j: (i, j)),
        out_shape=jax.ShapeDtypeStruct((t, n), F32),
        compiler_params=_cparams(("parallel", "arbitrary")),
        name="in_proj",
    )(x2d, w_bf)


TAIL_ROWS = SUBLANES
INV_BASE = 8


def _split(a):
    hi = a.astype(BF16)
    return hi, (a - hi.astype(F32)).astype(BF16)


def _dot3(a, b):
    return _dot(a[0], b[0]) + (_dot(a[0], b[1]) + _dot(a[1], b[0]))


def _unit_lower_inverse_steps(ms, c, out):
    ri = lax.broadcasted_iota(jnp.int32, (c, c), 0)
    ci = lax.broadcasted_iota(jnp.int32, (c, c), 1)
    eye = (ri == ci).astype(F32)
    blk = INV_BASE
    same = (ri // blk) == (ci // blk)
    ns = [jnp.where(same, -m, 0.0) for m in ms]
    xs = [eye + n for n in ns]
    span = 1
    while span * 2 < blk:
        nsp = [_split(n) for n in ns]
        ns = [_dot3(n, n) for n in nsp]
        yield
        nsp = [_split(n) for n in ns]
        xs = [x + _dot3(_split(x), n) for x, n in zip(xs, nsp)]
        yield
        span *= 2
    while blk < c:
        nxt = blk * 2
        emask = ((ri // nxt) == (ci // nxt)) & ((ri // blk) != (ci // blk))
        xsp = [_split(x) for x in xs]
        ts = [_dot3(x, _split(jnp.where(emask, m, 0.0))) for x, m in zip(xsp, ms)]
        yield
        xs = [x - _dot3(_split(t), xp) for x, t, xp in zip(xs, ts, xsp)]
        yield
        blk = nxt
    out.extend(xs)


def _gdn_kernel(hq_ref, z_ref, sm_ref, buf_ref, s0_ref, cw_ref, alog_ref, dtb_ref, ng_ref,
                y_ref, snew_ref, bufnew_ref, xp_ref, s_ref, *, chunk):
    c_idx = pl.program_id(1)
    n_c = pl.num_programs(1)
    C = chunk
    keep = CONV_W - 1

    @pl.when(c_idx == 0)
    def _():
        xp_ref[:, TAIL_ROWS - keep:TAIL_ROWS, :] = buf_ref[...]
        s_ref[...] = s0_ref[...]

    ri = lax.broadcasted_iota(jnp.int32, (C, C), 0)
    ci = lax.broadcasted_iota(jnp.int32, (C, C), 1)
    incl = ri >= ci
    strict = ri > ci

    heads = range(GDN_HEADS)

    def prepare(bi, d):
        xp_ref[bi, TAIL_ROWS:TAIL_ROWS + C, :] = hq_ref[bi]
        acc = cw_ref[0:1, :] * xp_ref[bi, TAIL_ROWS - keep:TAIL_ROWS - keep + C, :]
        for j in range(1, CONV_W):
            acc = acc + cw_ref[j:j + 1, :] * xp_ref[bi, TAIL_ROWS - keep + j:TAIL_ROWS - keep + j + C, :]
        conv = _silu(acc)
        xp_ref[bi, TAIL_ROWS - keep:TAIL_ROWS, :] = xp_ref[bi, TAIL_ROWS + C - keep:TAIL_ROWS + C, :]
        yield
        sm = sm_ref[bi, :, 0:LANES]
        xg = sm + dtb_ref[...]
        softplus = jnp.maximum(xg, 0.0) + jnp.log(1.0 + jnp.exp(-jnp.abs(xg)))
        g = -jnp.exp(alog_ref[...]) * softplus
        beta = _sigmoid(sm)
        gc = _dot_hi(incl.astype(F32), g)
        gc_t = gc.T
        yield
        for name in ("q", "k", "v", "gc", "bh", "eg", "decay"):
            d[name] = []
        for h in heads:
            q = conv[:, h * GDN_DK:(h + 1) * GDN_DK]
            k = conv[:, GDN_QK + h * GDN_DK:GDN_QK + (h + 1) * GDN_DK]
            d["q"].append(q * lax.rsqrt(jnp.sum(q * q, -1, keepdims=True) + 1e-6) * (GDN_DK ** -0.5))
            d["k"].append(k * lax.rsqrt(jnp.sum(k * k, -1, keepdims=True) + 1e-6))
            d["v"].append(conv[:, 2 * GDN_QK + h * GDN_DV:2 * GDN_QK + (h + 1) * GDN_DV])
            gch = gc[:, SM_A + h:SM_A + h + 1]
            d["gc"].append(gch)
            d["bh"].append(beta[:, SM_B + h:SM_B + h + 1])
            d["eg"].append(jnp.exp(gch))
            d["decay"].append(jnp.exp(jnp.where(incl, gch - gc_t[SM_A + h:SM_A + h + 1, :], -jnp.inf)))
            yield

    def recur(bi, d):
        qs, ks, vs, gcs, bhs, egs, decays = (d[n] for n in ("q", "k", "v", "gc", "bh", "eg", "decay"))
        kbs = [k * bh for k, bh in zip(ks, bhs)]
        ms = [jnp.where(strict, _dot_nt_bf(kb, k) * dc, 0.0) for kb, k, dc in zip(kbs, ks, decays)]
        yield
        attns = [_dot_nt_bf(q, k) * dc for q, k, dc in zip(qs, ks, decays)]
        yield
        tinvs = []
        yield from _unit_lower_inverse_steps(ms, C, tinvs)
        sols = [_dot_bf(t, jnp.concatenate([v * bh, kb * eg], axis=-1))
                for t, v, bh, kb, eg in zip(tinvs, vs, bhs, kbs, egs)]
        yield
        ss = [s_ref[bi, h] for h in heads]
        v_news = [sol[:, :GDN_DV] - _dot_bf(sol[:, GDN_DV:], s) for sol, s in zip(sols, ss)]
        yield
        os_ = [_dot_bf(q * eg, s) + _dot_bf(a, vn) for q, eg, s, a, vn in zip(qs, egs, ss, attns, v_news)]
        yield
        for h in heads:
            glast = gcs[h][C - 1:C, :]
            s_ref[bi, h] = ss[h] * jnp.exp(glast) + _dot_tn_bf(ks[h] * jnp.exp(glast - gcs[h]), v_news[h])
        yield
        for h in heads:
            o = os_[h]
            o = o * lax.rsqrt(jnp.mean(o * o, -1, keepdims=True) + RMS_EPS) * ng_ref[...]
            zh = z_ref[bi, :, h * GDN_DV:(h + 1) * GDN_DV]
            y_ref[bi, :, h * GDN_DV:(h + 1) * GDN_DV] = o * _silu(zh)

    n_seq = hq_ref.shape[0]
    data = [{} for _ in range(n_seq)]
    for _ in prepare(0, data[0]):
        pass
    for bi in range(n_seq):
        filler = prepare(bi + 1, data[bi + 1]) if bi + 1 < n_seq else iter(())
        for _ in recur(bi, data[bi]):
            next(filler, None)
        for _ in filler:
            pass

    @pl.when(c_idx == n_c - 1)
    def _():
        snew_ref[...] = s_ref[...]
        bufnew_ref[...] = xp_ref[:, TAIL_ROWS - keep:TAIL_ROWS, :]


def _gdn(h3, l, conv_buf, s0, conv_w, a_log, dt_bias, norm_g):
    b = h3.shape[0]
    chunk = min(CHUNK, l)
    assert l % chunk == 0 and chunk % SUBLANES == 0 and chunk >= CONV_W - 1
    kern = functools.partial(_gdn_kernel, chunk=chunk)
    keep = CONV_W - 1
    lane_row = lambda vec, at: jnp.zeros((1, LANES), F32).at[0, at:at + vec.shape[0]].set(vec)
    gb = _pick_tile(b, GDN_BATCH_TILE)
    return pl.pallas_call(
        kern,
        grid=(b // gb, l // chunk),
        in_specs=[
            pl.BlockSpec((gb, chunk, CONV_CH), lambda i, c: (i, c, COL_QKV // CONV_CH)),
            pl.BlockSpec((gb, chunk, GDN_V), lambda i, c: (i, c, COL_Z // GDN_V)),
            pl.BlockSpec((gb, chunk, SMALL_W), lambda i, c: (i, c, COL_SMALL // SMALL_W)),
            pl.BlockSpec((gb, keep, CONV_CH), lambda i, c: (i, 0, 0)),
            pl.BlockSpec((gb, GDN_HEADS, GDN_DK, GDN_DV), lambda i, c: (i, 0, 0, 0)),
            pl.BlockSpec((CONV_W, CONV_CH), lambda i, c: (0, 0)),
            pl.BlockSpec((1, LANES), lambda i, c: (0, 0)),
            pl.BlockSpec((1, LANES), lambda i, c: (0, 0)),
            pl.BlockSpec((1, GDN_DV), lambda i, c: (0, 0)),
        ],
        out_specs=[
            pl.BlockSpec((gb, chunk, GDN_V), lambda i, c: (i, c, 0)),
            pl.BlockSpec((gb, GDN_HEADS, GDN_DK, GDN_DV), lambda i, c: (i, 0, 0, 0)),
            pl.BlockSpec((gb, keep, CONV_CH), lambda i, c: (i, 0, 0)),
        ],
        out_shape=[
            jax.ShapeDtypeStruct((b, l, GDN_V), F32),
            jax.ShapeDtypeStruct((b, GDN_HEADS, GDN_DK, GDN_DV), F32),
            jax.ShapeDtypeStruct((b, keep, CONV_CH), F32),
        ],
        scratch_shapes=[
            pltpu.VMEM((gb, TAIL_ROWS + chunk, CONV_CH), F32),
            pltpu.VMEM((gb, GDN_HEADS, GDN_DK, GDN_DV), F32),
        ],
        compiler_params=_cparams(("parallel", "arbitrary")),
        name="gdn",
    )(h3, h3, h3, conv_buf, s0, conv_w, lane_row(a_log, SM_A), lane_row(dt_bias, SM_A),
      norm_g.reshape(1, GDN_DV))


def _rope_tables(pos, rot, period):
    half = rot // 2
    inv_freq = ROPE_THETA ** (-(2.0 / rot) * jnp.arange(half, dtype=F32))
    ang = pos.astype(F32)[:, None] * inv_freq[None, :]
    cos, sin = jnp.cos(ang), jnp.sin(ang)
    n = pos.shape[0]
    rest = period - rot
    c = jnp.concatenate([cos, cos, jnp.ones((n, rest), F32)], -1)
    sa = jnp.concatenate([-sin, jnp.zeros((n, half + rest), F32)], -1)
    sb = jnp.concatenate([jnp.zeros((n, half), F32), sin, jnp.zeros((n, rest), F32)], -1)
    reps = LANES // period
    return jnp.stack([jnp.tile(c, (1, reps)), jnp.tile(sa, (1, reps)), jnp.tile(sb, (1, reps))], 0)


Q_SCALE_LOG2 = HEAD_DIM ** -0.5 * float(np.log2(np.e))


def _rope128(x, tab_ref, half):
    return (x * tab_ref[0] + pltpu.roll(x, LANES - half, 1) * tab_ref[1]
            + pltpu.roll(x, half, 1) * tab_ref[2])


def _prep_kernel(q_ref, qi_ref, k_ref, v_ref, sm_ref, tq_ref, ti_ref, lng_ref, lnb_ref,
                 qt_ref, qit_ref, wit_ref, ko_ref, kb_ref, kio_ref, kib_ref, vo_ref, vt_ref):
    hq = HEAD_DIM // 8
    hi = IDX_DIM // 8
    for h in range(ATT_HEADS):
        x = _rope128(q_ref[:, h * HEAD_DIM:(h + 1) * HEAD_DIM], tq_ref, hq) * Q_SCALE_LOG2
        qt_ref[h * HEAD_DIM:(h + 1) * HEAD_DIM, :] = x.T.astype(BF16)
    for h in range(ATT_KV_HEADS):
        x = _rope128(k_ref[:, h * HEAD_DIM:(h + 1) * HEAD_DIM], tq_ref, hq)
        ko_ref[:, h * HEAD_DIM:(h + 1) * HEAD_DIM] = x
        kb_ref[:, h * HEAD_DIM:(h + 1) * HEAD_DIM] = x.astype(BF16)
        v = v_ref[:, h * HEAD_DIM:(h + 1) * HEAD_DIM]
        vo_ref[:, h * HEAD_DIM:(h + 1) * HEAD_DIM] = v
        vt_ref[h * HEAD_DIM:(h + 1) * HEAD_DIM, :] = v.T.astype(BF16)
    for c in range(IDX_Q // LANES):
        x = _rope128(qi_ref[:, c * LANES:(c + 1) * LANES], ti_ref, hi)
        qit_ref[c * LANES:(c + 1) * LANES, :] = x.T.astype(BF16)
    sm = sm_ref[:, 0:LANES]
    lane = lax.broadcasted_iota(jnp.int32, sm.shape, 1)
    is_ki = lane < IDX_DIM
    mu = jnp.sum(jnp.where(is_ki, sm, 0.0), -1, keepdims=True) * (1.0 / IDX_DIM)
    xc = jnp.where(is_ki, sm - mu, 0.0)
    var = jnp.sum(xc * xc, -1, keepdims=True) * (1.0 / IDX_DIM)
    ki = xc * lax.rsqrt(var + LN_EPS) * lng_ref[...] + lnb_ref[...]
    ki = _rope128(ki, ti_ref, hi)[:, 0:IDX_DIM]
    kio_ref[...] = ki
    kib_ref[...] = ki.astype(BF16)
    wit_ref[...] = sm.T[SM_WI:SM_WI + IDX_HEADS, :] * (IDX_HEADS ** -0.5 * IDX_DIM ** -0.5)


def _prep(h3, pos, ln_g, ln_b):
    b, l, _ = h3.shape
    tm = _pick_tile(l, PREP_TM)
    tab_q = _rope_tables(pos, HEAD_DIM // 4, HEAD_DIM)
    tab_i = _rope_tables(pos, IDX_DIM // 4, IDX_DIM)
    lng = jnp.concatenate([ln_g, jnp.zeros((LANES - IDX_DIM,), F32)]).reshape(1, LANES)
    lnb = jnp.concatenate([ln_b, jnp.zeros((LANES - IDX_DIM,), F32)]).reshape(1, LANES)
    rows = lambda w, col: pl.BlockSpec((None, tm, w), lambda i, t: (i, t, col))
    cols = lambda w: pl.BlockSpec((None, w, tm), lambda i, t: (i, 0, t))
    return pl.pallas_call(
        _prep_kernel,
        grid=(b, l // tm),
        in_specs=[
            rows(ATT_Q, COL_Q // ATT_Q), rows(IDX_Q, COL_QI // IDX_Q), rows(ATT_KV, COL_K // ATT_KV),
            rows(ATT_KV, COL_V // ATT_KV), rows(SMALL_W, COL_SMALL // SMALL_W),
            pl.BlockSpec((3, tm, LANES), lambda i, t: (0, t, 0)),
            pl.BlockSpec((3, tm, LANES), lambda i, t: (0, t, 0)),
            pl.BlockSpec((1, LANES), lambda i, t: (0, 0)),
            pl.BlockSpec((1, LANES), lambda i, t: (0, 0)),
        ],
        out_specs=[cols(ATT_Q), cols(IDX_Q), cols(IDX_HEADS), rows(ATT_KV, 0), rows(ATT_KV, 0),
                   rows(IDX_DIM, 0), rows(IDX_DIM, 0), rows(ATT_KV, 0), cols(ATT_KV)],
        out_shape=[
            jax.ShapeDtypeStruct((b, ATT_Q, l), BF16),
            jax.ShapeDtypeStruct((b, IDX_Q, l), BF16),
            jax.ShapeDtypeStruct((b, IDX_HEADS, l), F32),
            jax.ShapeDtypeStruct((b, l, ATT_KV), F32),
            jax.ShapeDtypeStruct((b, l, ATT_KV), BF16),
            jax.ShapeDtypeStruct((b, l, IDX_DIM), F32),
            jax.ShapeDtypeStruct((b, l, IDX_DIM), BF16),
            jax.ShapeDtypeStruct((b, l, ATT_KV), F32),
            jax.ShapeDtypeStruct((b, ATT_KV, l), BF16),
        ],
        compiler_params=_cparams(("parallel", "parallel")),
        name="dsa_prep",
    )(h3, h3, h3, h3, h3, tab_q, tab_i, lng, lnb)


INT_MIN = -2 ** 31
INT_MAX = 2 ** 31 - 1
NEG_INF_KEY = INT_MIN + 0x7FFFFF
NEG_BIG = -1e30
HEAD_PAIRS = IDX_HEADS // 2
COUNT_CHAINS = 8
DENOM_ROWS = 16


def _dsa_kernel(qt_ref, qit_ref, wit_ref, ki_ref, k_ref, vt_ref, o_ref,
                key_ref, qs_ref, qip_ref, m_ref, acc_ref, cm_ref,
                *, tq, kb, l_true, pos0, topk, idx_bits):
    q0 = pos0 + pl.program_id(1) * tq
    qpos = q0 + lax.broadcasted_iota(jnp.int32, (1, tq), 1)
    lim = jnp.minimum((qpos // CHUNK + 1) * CHUNK, l_true)
    lim_max = jnp.minimum(((q0 + tq - 1) // CHUNK + 1) * CHUNK, l_true)
    nkb = (lim_max + kb - 1) // kb
    key_iota = lax.broadcasted_iota(jnp.int32, (kb, tq), 0)
    wit = wit_ref[...]

    for p in range(HEAD_PAIRS):
        for u in range(2):
            hh = 2 * p + u
            qip_ref[p, :, u * tq:(u + 1) * tq] = qit_ref[hh * IDX_DIM:(hh + 1) * IDX_DIM, :]
    for g in range(ATT_KV_HEADS):
        for r in range(REP):
            hh = g * REP + r
            qs_ref[g, :, r * tq:(r + 1) * tq] = qt_ref[hh * HEAD_DIM:(hh + 1) * HEAD_DIM, :]

    def score_blk(j, carry):
        off = pl.multiple_of(j * kb, kb)
        ki = ki_ref[pl.ds(off, kb), :]
        acc = jnp.zeros((kb, tq), F32)
        for p in range(HEAD_PAIRS):
            s2 = _dot(ki, qip_ref[p])
            acc = (acc + wit[2 * p:2 * p + 1, :] * jnp.maximum(s2[:, :tq], 0.0)
                   + wit[2 * p + 1:2 * p + 2, :] * jnp.maximum(s2[:, tq:], 0.0))
        score = jnp.where(off + key_iota < lim, acc, -jnp.inf)
        bits = pltpu.bitcast(score, jnp.int32)
        key_ref[pl.ds(off, kb), :] = bits ^ ((bits >> 31) & INT_MAX)
        return carry

    lax.fori_loop(0, nkb, score_blk, 0)

    def count(pred_fn):
        def blk(j, c):
            off = pl.multiple_of(j * kb, kb)
            part = jnp.where(pred_fn(key_ref[pl.ds(off, kb), :], off), 1.0, 0.0)
            return c + jnp.sum(part.reshape(kb // (COUNT_CHAINS * SUBLANES), COUNT_CHAINS * SUBLANES, tq), axis=0)
        c = lax.fori_loop(0, nkb, blk, jnp.zeros((COUNT_CHAINS * SUBLANES, tq), F32))
        return jnp.sum(c, axis=0, keepdims=True)

    def bit_step(t, cur):
        cand_u = cur | lax.shift_left(jnp.int32(1), 31 - t)
        cand_s = cand_u ^ INT_MIN
        cnt = count(lambda kk, off: kk >= cand_s)
        return jnp.where(cnt >= topk, cand_u, cur)

    thr = lax.fori_loop(0, 32, bit_step, jnp.zeros((1, tq), jnp.int32)) ^ INT_MIN
    n_ge = count(lambda kk, off: kk >= thr)

    cm_ref[...] = jnp.full((1, tq), INT_MAX, jnp.int32)

    @pl.when(jnp.max(jnp.where((n_ge > topk) & (thr > NEG_INF_KEY), 1, 0)) > 0)
    def _():
        need = topk - count(lambda kk, off: kk > thr)

        def idx_step(t, cm):
            cand = cm | lax.shift_left(jnp.int32(1), idx_bits - 1 - t)
            before = count(lambda kk, off: (kk == thr) & (off + key_iota < cand))
            return jnp.where(before < need, cand, cm)
        cm_ref[...] = lax.fori_loop(0, idx_bits, idx_step, jnp.zeros((1, tq), jnp.int32))

    cm = cm_ref[...]

    m_ref[...] = jnp.full(m_ref.shape, NEG_BIG, F32)
    acc_ref[...] = jnp.zeros(acc_ref.shape, F32)
    groups = range(ATT_KV_HEADS)
    ones_rows = jnp.ones((DENOM_ROWS, kb), BF16)

    def attn_blk(j):
        off = pl.multiple_of(j * kb, kb)
        kk = key_ref[pl.ds(off, kb), :]
        kpos = off + key_iota
        sel = ((kk > thr) | ((kk == thr) & (kpos <= cm))) & (kpos < lim)
        bias = jnp.where(sel, 0.0, NEG_BIG)
        bias = jnp.concatenate([bias] * REP, axis=1)
        logits = [_dot(k_ref[pl.ds(off, kb), g * HEAD_DIM:(g + 1) * HEAD_DIM], qs_ref[g]) + bias for g in groups]
        m_old = [m_ref[g] for g in groups]
        m_new = [jnp.maximum(m_old[g], jnp.max(logits[g], axis=0, keepdims=True)) for g in groups]
        ps = [jnp.exp2(logits[g] - m_new[g]).astype(BF16) for g in groups]
        alphas = [jnp.exp2(m_old[g] - m_new[g]) for g in groups]
        pvs = [_dot(jnp.concatenate([vt_ref[g * HEAD_DIM:(g + 1) * HEAD_DIM, pl.ds(off, kb)], ones_rows], axis=0),
                    ps[g]) for g in groups]
        for g in groups:
            acc_ref[g] = alphas[g] * acc_ref[g] + pvs[g]
            m_ref[g] = m_new[g]

    def attn_pair(jp, carry):
        attn_blk(2 * jp)
        attn_blk(2 * jp + 1)
        return carry

    lax.fori_loop(0, nkb // 2, attn_pair, 0)

    @pl.when(nkb % 2 == 1)
    def _():
        attn_blk(nkb - 1)

    for g in groups:
        o_t = acc_ref[g, 0:HEAD_DIM, :] / acc_ref[g, HEAD_DIM:HEAD_DIM + 1, :]
        for r in range(REP):
            hh = g * REP + r
            o_ref[:, hh * HEAD_DIM:(hh + 1) * HEAD_DIM] = o_t[:, r * tq:(r + 1) * tq].T


def _dsa(qt, qit, wit, ki, k, vt, l_true, pos0):
    b, _, t = qt.shape
    lk = k.shape[1]
    tq, kb = DSA_Q_TILE, DSA_KEY_BLOCK
    assert t % tq == 0 and lk % kb == 0 and tq == HEAD_DIM
    topk = min(TOPK_MAX, l_true // 4)
    kern = functools.partial(_dsa_kernel, tq=tq, kb=kb, l_true=l_true, pos0=pos0, topk=topk,
                             idx_bits=int(lk).bit_length())
    return pl.pallas_call(
        kern,
        grid=(b, t // tq),
        in_specs=[
            pl.BlockSpec((None, ATT_Q, tq), lambda i, t_: (i, 0, t_)),
            pl.BlockSpec((None, IDX_Q, tq), lambda i, t_: (i, 0, t_)),
            pl.BlockSpec((None, IDX_HEADS, tq), lambda i, t_: (i, 0, t_)),
            pl.BlockSpec((None, lk, IDX_DIM), lambda i, t_: (i, 0, 0)),
            pl.BlockSpec((None, lk, ATT_KV), lambda i, t_: (i, 0, 0)),
            pl.BlockSpec((None, ATT_KV, lk), lambda i, t_: (i, 0, 0)),
        ],
        out_specs=pl.BlockSpec((None, tq, ATT_Q), lambda i, t_: (i, t_, 0)),
        out_shape=jax.ShapeDtypeStruct((b, t, ATT_Q), F32),
        scratch_shapes=[
            pltpu.VMEM((lk, tq), jnp.int32),
            pltpu.VMEM((ATT_KV_HEADS, HEAD_DIM, REP * tq), BF16),
            pltpu.VMEM((HEAD_PAIRS, IDX_DIM, 2 * tq), BF16),
            pltpu.VMEM((ATT_KV_HEADS, 1, REP * tq), F32),
            pltpu.VMEM((ATT_KV_HEADS, HEAD_DIM + DENOM_ROWS, REP * tq), F32),
            pltpu.VMEM((1, tq), jnp.int32),
        ],
        compiler_params=_cparams(("parallel", "arbitrary")),
        name="dsa",
    )(qt, qit, wit, ki, k, vt)


def _layer_norm(x, g, b):
    mu = jnp.mean(x, -1, keepdims=True)
    xc = x - mu
    var = jnp.mean(xc * xc, -1, keepdims=True)
    return xc * lax.rsqrt(var + LN_EPS) * g + b


def _merge_kernel(ya_ref, yb_ref, ga_ref, gb_ref, x_ref, wa_ref, wb_ref, wo_ref, g_ref, b_ref, o_ref):
    pa = _dot(ya_ref[...].astype(BF16), wa_ref[...])
    pb = _dot(yb_ref[...].astype(BF16), wb_ref[...])
    merged = _sigmoid(ga_ref[...]) * pa + _sigmoid(gb_ref[...]) * pb
    y = DN_ALPHA * x_ref[...] + _dot(merged.astype(BF16), wo_ref[...])
    o_ref[...] = _layer_norm(y, g_ref[...], b_ref[...])


def _merge(ya, yb, h3, x, wa, wb, wo, g, b):
    bsz, l, _ = x.shape
    tm = _pick_tile(l, TOKEN_TM)
    row = lambda c: pl.BlockSpec((None, tm, D_MODEL), lambda i, t: (i, t, c))
    full = lambda shp: pl.BlockSpec(shp, lambda i, t: (0, 0))
    return pl.pallas_call(
        _merge_kernel,
        grid=(bsz, l // tm),
        in_specs=[row(0), row(0), row(COL_GA // D_MODEL), row(COL_GB // D_MODEL), row(0),
                  full((GDN_V, D_MODEL)), full((ATT_Q, D_MODEL)), full((D_MODEL, D_MODEL)),
                  full((1, D_MODEL)), full((1, D_MODEL))],
        out_specs=row(0),
        out_shape=jax.ShapeDtypeStruct((bsz, l, D_MODEL), F32),
        compiler_params=_cparams(("parallel", "parallel")),
        name="merge_ln1",
    )(ya, yb, h3, h3, x, wa, wb, wo, g.reshape(1, D_MODEL), b.reshape(1, D_MODEL))


GROUP_SIZE = N_EXPERTS // N_GROUPS


def _first_max(cur, rows, n_rows):
    m = jnp.max(cur, axis=0, keepdims=True)
    idx = jnp.min(jnp.where(cur == m, rows, n_rows), axis=0, keepdims=True)
    return m, idx


def _router_kernel(x_ref, wr_ref, rb_ref, ws1_ref, ws3_ref, ws2_ref,
                   eidx_ref, wts_ref, rank_ref, cnt_ref, base_ref, run_ref):
    @pl.when(pl.program_id(0) == 0)
    def _():
        run_ref[...] = jnp.zeros(run_ref.shape, F32)

    x = x_ref[...]
    tm = x.shape[0]
    logits = _dot_nt_hi(wr_ref[...], x)
    scores = _sigmoid(logits)
    biased = scores + rb_ref[...]
    neg = -jnp.inf

    rows_g = lax.broadcasted_iota(jnp.int32, (GROUP_SIZE, tm), 0)
    gs = []
    for g in range(N_GROUPS):
        blk = biased[g * GROUP_SIZE:(g + 1) * GROUP_SIZE, :]
        m1, i1 = _first_max(blk, rows_g, GROUP_SIZE)
        m2 = jnp.max(jnp.where(rows_g == i1, neg, blk), axis=0, keepdims=True)
        gs.append(m1 + m2)
    cur = jnp.concatenate(gs, axis=0)
    rows_n = lax.broadcasted_iota(jnp.int32, (N_GROUPS, tm), 0)
    gsel = jnp.zeros((N_GROUPS, tm), F32)
    for _ in range(TOPK_GROUPS):
        _, ig = _first_max(cur, rows_n, N_GROUPS)
        hit = rows_n == ig
        gsel = jnp.where(hit, 1.0, gsel)
        cur = jnp.where(hit, neg, cur)
    gexp = jnp.concatenate([jnp.broadcast_to(gsel[g:g + 1, :], (GROUP_SIZE, tm)) for g in range(N_GROUPS)], axis=0)
    cur = jnp.where(gexp > 0.0, biased, neg)

    rows_e = lax.broadcasted_iota(jnp.int32, (N_EXPERTS, tm), 0)
    es, ws, hits = [], [], []
    for _ in range(EXPERT_TOPK):
        _, ie = _first_max(cur, rows_e, N_EXPERTS)
        hit = rows_e == ie
        es.append(ie)
        hits.append(hit)
        ws.append(jnp.sum(jnp.where(hit, scores, 0.0), axis=0, keepdims=True))
        cur = jnp.where(hit, neg, cur)
    w = jnp.concatenate(ws, axis=0)
    eidx_ref[...] = jnp.concatenate(es, axis=0)
    wts_ref[...] = w / jnp.sum(w, axis=0, keepdims=True) * ROUTED_SCALE

    assigned = jnp.zeros((N_EXPERTS, tm), F32)
    for hit in hits:
        assigned = jnp.where(hit, 1.0, assigned)
    earlier = (lax.broadcasted_iota(jnp.int32, (tm, tm), 0) < lax.broadcasted_iota(jnp.int32, (tm, tm), 1))
    before = run_ref[...] + _dot(assigned.astype(BF16), earlier.astype(BF16))
    rank_ref[...] = jnp.concatenate(
        [jnp.sum(jnp.where(hit, before, 0.0), axis=0, keepdims=True) for hit in hits], axis=0).astype(jnp.int32)
    run_ref[...] = run_ref[...] + jnp.sum(assigned, axis=1, keepdims=True)
    cnt_ref[...] = run_ref[...].astype(jnp.int32)

    xb = x.astype(BF16)
    hs = _silu(_dot(xb, ws1_ref[...])) * _dot(xb, ws3_ref[...])
    base_ref[...] = DN_ALPHA * x + _dot(hs.astype(BF16), ws2_ref[...])


def _router(x1, wr_t, rbias, ws1, ws3, ws2, tm):
    t = x1.shape[0]
    full = lambda shp: pl.BlockSpec(shp, lambda i: (0, 0))
    return pl.pallas_call(
        _router_kernel,
        grid=(t // tm,),
        in_specs=[pl.BlockSpec((tm, D_MODEL), lambda i: (i, 0)),
                  full((N_EXPERTS, D_MODEL)), full((N_EXPERTS, 1)),
                  full((D_MODEL, SHARED_DIM)), full((D_MODEL, SHARED_DIM)), full((SHARED_DIM, D_MODEL))],
        out_specs=[pl.BlockSpec((EXPERT_TOPK, tm), lambda i: (0, i)),
                   pl.BlockSpec((EXPERT_TOPK, tm), lambda i: (0, i)),
                   pl.BlockSpec((EXPERT_TOPK, tm), lambda i: (0, i)),
                   full((N_EXPERTS, 1)),
                   pl.BlockSpec((tm, D_MODEL), lambda i: (i, 0))],
        out_shape=[jax.ShapeDtypeStruct((EXPERT_TOPK, t), jnp.int32),
                   jax.ShapeDtypeStruct((EXPERT_TOPK, t), F32),
                   jax.ShapeDtypeStruct((EXPERT_TOPK, t), jnp.int32),
                   jax.ShapeDtypeStruct((N_EXPERTS, 1), jnp.int32),
                   jax.ShapeDtypeStruct((t, D_MODEL), F32)],
        scratch_shapes=[pltpu.VMEM((N_EXPERTS, 1), F32)],
        compiler_params=_cparams(("arbitrary",)),
        name="router_shared",
    )(x1, wr_t, rbias.reshape(N_EXPERTS, 1), ws1, ws3, ws2)


def _block_plan(counts, n_blocks):
    padded = (counts + MOE_ROWS - 1) // MOE_ROWS * MOE_ROWS
    pad_end = jnp.cumsum(padded)
    blk_first = jnp.arange(n_blocks, dtype=jnp.int32) * MOE_ROWS
    blk_exp = jnp.minimum(jnp.sum((pad_end[None, :] <= blk_first[:, None]).astype(jnp.int32), axis=1), N_EXPERTS - 1)
    return pad_end - padded, jnp.maximum(pad_end - MOE_ROWS, 0), blk_exp, pad_end[-1:] // MOE_ROWS


def _pos_kernel(e_ref, r_ref, ps_ref, pos_ref):
    tm = e_ref.shape[1]
    rows_e = lax.broadcasted_iota(jnp.int32, (N_EXPERTS, tm), 0)
    first = [jnp.sum(jnp.where(rows_e == e_ref[j:j + 1, :], ps_ref[...], 0), axis=0, keepdims=True)
             for j in range(EXPERT_TOPK)]
    pos_ref[...] = jnp.concatenate(first, axis=0) + r_ref[...]


def _positions(eidx_t, rank_t, pad_start, tm):
    t = eidx_t.shape[1]
    blk = pl.BlockSpec((EXPERT_TOPK, tm), lambda i: (0, i))
    return pl.pallas_call(
        _pos_kernel,
        grid=(t // tm,),
        in_specs=[blk, blk, pl.BlockSpec((N_EXPERTS, 1), lambda i: (0, 0))],
        out_specs=blk,
        out_shape=jax.ShapeDtypeStruct((EXPERT_TOPK, t), jnp.int32),
        compiler_params=_cparams(("parallel",)),
        name="moe_positions",
    )(eidx_t, rank_t, pad_start.reshape(N_EXPERTS, 1))


def _scatter_kernel(last_ref, nu_ref, pos_ref, x_ref, rows_ref, zero_ref, sem, zsem):
    tm = x_ref.shape[0]
    n_blocks = rows_ref.shape[0] // MOE_ROWS

    @pl.when(pl.program_id(0) == 0)
    def _():
        zero_ref[...] = jnp.zeros(zero_ref.shape, F32)

        def zero_copy(first_row):
            first = pl.multiple_of(first_row, MOE_ROWS)
            return pltpu.make_async_copy(zero_ref, rows_ref.at[pl.ds(first, MOE_ROWS), :], zsem)

        def each(fn):
            def expert_block(e, carry):
                fn(zero_copy(last_ref[e]))
                return carry

            def unused_block(blk, carry):
                fn(zero_copy(blk * MOE_ROWS))
                return carry

            lax.fori_loop(0, N_EXPERTS, expert_block, 0)
            lax.fori_loop(nu_ref[0], n_blocks, unused_block, 0)

        each(lambda copy: copy.start())
        each(lambda copy: copy.wait())

    def row_copy(t, j):
        return pltpu.make_async_copy(x_ref.at[pl.ds(t, 1), :], rows_ref.at[pl.ds(pos_ref[j, t], 1), :], sem)

    def issue(t, carry):
        for j in range(EXPERT_TOPK):
            row_copy(t, j).start()
        return carry

    lax.fori_loop(0, tm, issue, 0)
    for j in range(EXPERT_TOPK):
        pltpu.make_async_copy(x_ref, rows_ref.at[pl.ds(0, tm), :], sem).wait()


def _scatter_rows(last_block_row, n_used, pos_t, x1, n_rows, tm):
    t = x1.shape[0]
    grid_spec = pltpu.PrefetchScalarGridSpec(
        num_scalar_prefetch=2,
        grid=(t // tm,),
        in_specs=[pl.BlockSpec((EXPERT_TOPK, tm), lambda i, lr, nu: (0, i), memory_space=pltpu.SMEM),
                  pl.BlockSpec((tm, D_MODEL), lambda i, lr, nu: (i, 0))],
        out_specs=pl.BlockSpec(memory_space=pl.ANY),
        scratch_shapes=[pltpu.VMEM((MOE_ROWS, D_MODEL), F32), pltpu.SemaphoreType.DMA(()),
                        pltpu.SemaphoreType.DMA(())],
    )
    return pl.pallas_call(
        _scatter_kernel,
        grid_spec=grid_spec,
        out_shape=jax.ShapeDtypeStruct((n_rows, D_MODEL), F32),
        compiler_params=_cparams(("arbitrary",)),
        name="moe_scatter",
    )(last_block_row, n_used, pos_t, x1)


def _expert_kernel(be_ref, nu_ref, x_ref, w1_ref, w3_ref, w2_ref, o_ref, w1b_ref, w3b_ref, w2b_ref):
    i = pl.program_id(0)
    prev = be_ref[jnp.maximum(i - 1, 0)]

    @pl.when((i == 0) | (be_ref[i] != prev))
    def _():
        w1b_ref[...] = w1_ref[...].astype(BF16)
        w3b_ref[...] = w3_ref[...].astype(BF16)
        w2b_ref[...] = w2_ref[...].astype(BF16)

    @pl.when(i < nu_ref[0])
    def _():
        x = x_ref[...].astype(BF16)
        hmid = _silu(_dot(x, w1b_ref[...])) * _dot(x, w3b_ref[...])
        o_ref[...] = _dot(hmid.astype(BF16), w2b_ref[...])

    @pl.when(i >= nu_ref[0])
    def _():
        o_ref[...] = jnp.zeros(o_ref.shape, F32)


def _experts(xg, blk_exp, n_used, w1, w3, w2):
    rows = xg.shape[0]
    n_blocks = rows // MOE_ROWS
    grid_spec = pltpu.PrefetchScalarGridSpec(
        num_scalar_prefetch=2,
        grid=(n_blocks,),
        in_specs=[
            pl.BlockSpec((MOE_ROWS, D_MODEL), lambda i, be, nu: (jnp.minimum(i, nu[0] - 1), 0)),
            pl.BlockSpec((None, D_MODEL, EXPERT_DIM), lambda i, be, nu: (be[i], 0, 0)),
            pl.BlockSpec((None, D_MODEL, EXPERT_DIM), lambda i, be, nu: (be[i], 0, 0)),
            pl.BlockSpec((None, EXPERT_DIM, D_MODEL), lambda i, be, nu: (be[i], 0, 0)),
        ],
        out_specs=pl.BlockSpec((MOE_ROWS, D_MODEL), lambda i, be, nu: (i, 0)),
        scratch_shapes=[pltpu.VMEM((D_MODEL, EXPERT_DIM), BF16), pltpu.VMEM((D_MODEL, EXPERT_DIM), BF16),
                        pltpu.VMEM((EXPERT_DIM, D_MODEL), BF16)],
    )
    return pl.pallas_call(
        _expert_kernel,
        grid_spec=grid_spec,
        out_shape=jax.ShapeDtypeStruct((rows, D_MODEL), F32),
        compiler_params=_cparams(("arbitrary",)),
        name="experts",
    )(blk_exp, n_used, xg, w1, w3, w2)


def _combine_kernel(pos_ref, pos_next_ref, y_ref, w_ref, base_ref, g_ref, b_ref, o_ref, gbuf_ref, sem):
    tm = base_ref.shape[0]
    i = pl.program_id(0)
    slot = lax.rem(i, 2)

    def issue_tile(p_ref, s):
        def issue(t, carry):
            for j in range(EXPERT_TOPK):
                pltpu.make_async_copy(y_ref.at[pl.ds(p_ref[j, t], 1), :], gbuf_ref.at[s, j, pl.ds(t, 1), :],
                                      sem.at[s]).start()
            return carry
        lax.fori_loop(0, tm, issue, 0)

    @pl.when(i == 0)
    def _():
        issue_tile(pos_ref, 0)

    @pl.when(i + 1 < pl.num_programs(0))
    def _():
        issue_tile(pos_next_ref, 1 - slot)

    for j in range(EXPERT_TOPK):
        pltpu.make_async_copy(y_ref.at[pl.ds(0, tm), :], gbuf_ref.at[slot, j], sem.at[slot]).wait()

    acc = base_ref[...]
    w = w_ref[...]
    for j in range(EXPERT_TOPK):
        acc = acc + w[:, j:j + 1] * gbuf_ref[slot, j]
    o_ref[...] = _layer_norm(acc, g_ref[...], b_ref[...])


def _combine(pos_t, y_rows, wts, base, g, b, tm):
    t = base.shape[0]
    n_tiles = t // tm
    return pl.pallas_call(
        _combine_kernel,
        grid=(n_tiles,),
        in_specs=[pl.BlockSpec((EXPERT_TOPK, tm), lambda i: (0, i), memory_space=pltpu.SMEM),
                  pl.BlockSpec((EXPERT_TOPK, tm), lambda i: (0, jnp.minimum(i + 1, n_tiles - 1)),
                               memory_space=pltpu.SMEM),
                  pl.BlockSpec(memory_space=pl.ANY),
                  pl.BlockSpec((tm, EXPERT_TOPK), lambda i: (i, 0)),
                  pl.BlockSpec((tm, D_MODEL), lambda i: (i, 0)),
                  pl.BlockSpec((1, D_MODEL), lambda i: (0, 0)),
                  pl.BlockSpec((1, D_MODEL), lambda i: (0, 0))],
        out_specs=pl.BlockSpec((tm, D_MODEL), lambda i: (i, 0)),
        out_shape=jax.ShapeDtypeStruct((t, D_MODEL), F32),
        scratch_shapes=[pltpu.VMEM((2, EXPERT_TOPK, tm, D_MODEL), F32), pltpu.SemaphoreType.DMA((2,))],
        compiler_params=_cparams(("arbitrary",)),
        name="combine_ln2",
    )(pos_t, pos_t, y_rows, wts, base, g.reshape(1, D_MODEL), b.reshape(1, D_MODEL))


def _moe(x1, prm):
    t = x1.shape[0]
    tm = _pick_tile(t, TOKEN_TM)
    eidx_t, wts_t, rank_t, counts, base = _router(x1, prm["w_router_t"], prm["router_bias"], prm["ws1"],
                                                  prm["ws3"], prm["ws2"], tm)
    n_blocks = -(-t * EXPERT_TOPK // MOE_ROWS) + N_EXPERTS
    pad_start, last_block_row, blk_exp, n_used = _block_plan(counts.reshape(N_EXPERTS), n_blocks)
    pos_t = _positions(eidx_t, rank_t, pad_start, tm)
    x_rows = _scatter_rows(last_block_row, n_used, pos_t, x1, n_blocks * MOE_ROWS, tm)
    y_rows = _experts(x_rows, blk_exp, n_used, prm["w1"], prm["w3"], prm["w2"])
    return _combine(pos_t, y_rows, wts_t.T, base, prm["ln2_g"], prm["ln2_b"], _pick_tile(t, COMBINE_TM))


def _mixer(x, pos0, conv_buf, s0, k_past, v_past, ik_past, prm):
    b, l, _ = x.shape
    lp = _round_up(l, DSA_Q_TILE)
    xp = x if lp == l else jnp.pad(x, ((0, 0), (0, lp - l), (0, 0)))
    h3 = _project(xp.reshape(b * lp, D_MODEL), prm["w_in"]).reshape(b, lp, H_COLS)

    ya, s_new, buf_new = _gdn(h3, l, conv_buf, s0, prm["conv_w"], prm["a_log"], prm["dt_bias"], prm["gdn_norm_g"])

    pos = pos0 + jnp.arange(lp, dtype=jnp.int32)
    qt, qit, wit, k_new, k_bf, ki_new, ki_bf, v_new, vt = _prep(h3, pos, prm["idx_k_ln_g"], prm["idx_k_ln_b"])
    if k_past is None:
        l_all = l
        k_all, ki_all, vt_all = k_bf, ki_bf, vt
    else:
        past = k_past.shape[1]
        l_all = past + l
        k_all = jnp.concatenate([k_past.reshape(b, past, ATT_KV).astype(BF16), k_bf[:, :l]], 1)
        ki_all = jnp.concatenate([ik_past.astype(BF16), ki_bf[:, :l]], 1)
        vt_all = jnp.concatenate([jnp.swapaxes(v_past.reshape(b, past, ATT_KV), 1, 2).astype(BF16), vt[:, :, :l]], 2)
    lk = _round_up(l_all, DSA_KEY_BLOCK)
    if lk != l_all:
        k_all = jnp.pad(k_all, ((0, 0), (0, lk - l_all), (0, 0)))
        ki_all = jnp.pad(ki_all, ((0, 0), (0, lk - l_all), (0, 0)))
        vt_all = jnp.pad(vt_all, ((0, 0), (0, 0), (0, lk - l_all)))
    yb = _dsa(qt, qit, wit, ki_all, k_all, vt_all, l_all, pos0)

    x1 = _merge(ya, yb, h3, x, prm["w_o_gdn"], prm["w_o_dsa"], prm["w_out"], prm["ln1_g"], prm["ln1_b"])
    state = (k_new[:, :l].reshape(b, l, ATT_KV_HEADS, HEAD_DIM), v_new[:, :l].reshape(b, l, ATT_KV_HEADS, HEAD_DIM),
             ki_new[:, :l], s_new, buf_new)
    return x1, state


def kernel(x_prompt, x_sample, cache_k, cache_v, cache_idx_k, state_gdn, state_conv, w_in, conv_w, a_log, dt_bias, gdn_norm_g, w_o_gdn, idx_k_ln_g, idx_k_ln_b, w_o_dsa, w_out, ln1_g, ln1_b, w_router, router_bias, w1, w3, w2, ws1, ws3, ws2, ln2_g, ln2_b):
    assert w_in.shape[0] == DEPTH == 1
    bp, lp_, _ = x_prompt.shape
    bs, ls_, _ = x_sample.shape
    past = cache_k.shape[2]
    prm = dict(
        w_in=_repack_w_in(w_in[0]).astype(BF16), conv_w=conv_w[0], a_log=a_log[0], dt_bias=dt_bias[0],
        gdn_norm_g=gdn_norm_g[0], w_o_gdn=w_o_gdn[0].astype(BF16), idx_k_ln_g=idx_k_ln_g[0],
        idx_k_ln_b=idx_k_ln_b[0], w_o_dsa=w_o_dsa[0].astype(BF16), w_out=w_out[0].astype(BF16),
        ln1_g=ln1_g[0], ln1_b=ln1_b[0], w_router_t=w_router[0].T, router_bias=router_bias[0],
        w1=w1[0], w3=w3[0], w2=w2[0], ws1=ws1[0].astype(BF16), ws3=ws3[0].astype(BF16),
        ws2=ws2[0].astype(BF16), ln2_g=ln2_g[0], ln2_b=ln2_b[0])
    conv0 = jnp.zeros((bp, CONV_W - 1, CONV_CH), F32)
    s0 = jnp.zeros((bp, GDN_HEADS, GDN_DK, GDN_DV), F32)
    x1p, sp = _mixer(x_prompt, 0, conv0, s0, None, None, None, prm)
    x1s, ss = _mixer(x_sample, past, state_conv[0], state_gdn[0], cache_k[0], cache_v[0], cache_idx_k[0], prm)
    tp, ts = bp * lp_, bs * ls_
    y = _moe(jnp.concatenate([x1p.reshape(tp, D_MODEL), x1s.reshape(ts, D_MODEL)], 0), prm)
    yp = y[:tp].reshape(bp, lp_, D_MODEL)
    ys = y[tp:].reshape(bs, ls_, D_MODEL)
    return (yp, ys) + tuple(a[None] for a in sp) + tuple(a[None] for a in ss)
```

```python
import functools

import jax
import jax.numpy as jnp
import numpy as np
from jax import lax
from jax.experimental import pallas as pl
from jax.experimental.pallas import tpu as pltpu

F32 = jnp.float32
BF16 = jnp.bfloat16

D_MODEL = 1024
CHUNK = 64
GDN_HEADS = 8
GDN_DK = 128
GDN_DV = 128
CONV_W = 4
ATT_HEADS = 8
ATT_KV_HEADS = 2
HEAD_DIM = 128
IDX_HEADS = 16
IDX_DIM = 64
TOPK_MAX = 256
ROPE_THETA = 500000.0
N_EXPERTS = 256
EXPERT_TOPK = 8
N_GROUPS = 8
TOPK_GROUPS = 4
EXPERT_DIM = 256
SHARED_DIM = 256
ROUTED_SCALE = 2.5
DEPTH = 1
DN_ALPHA = (2.0 * DEPTH) ** 0.25
LN_EPS = 1e-5
RMS_EPS = 1e-6

GDN_QK = GDN_HEADS * GDN_DK
GDN_V = GDN_HEADS * GDN_DV
CONV_CH = 2 * GDN_QK + GDN_V
ATT_Q = ATT_HEADS * HEAD_DIM
ATT_KV = ATT_KV_HEADS * HEAD_DIM
IDX_Q = IDX_HEADS * IDX_DIM
REP = ATT_HEADS // ATT_KV_HEADS

LANES = 128
SUBLANES = 8
VMEM_LIMIT = 56 * 1024 * 1024
DMA_PRIORITIES = 2

PROJ_TM = 1024
PROJ_TN = 1024
PREP_TM = 512
DSA_Q_TILE = 128
DSA_KEY_BLOCK = 512
TOKEN_TM = 256
COMBINE_TM = 128
GDN_BATCH_TILE = 4
MOE_ROWS = 256

COL_QKV = 0
COL_Z = COL_QKV + CONV_CH
COL_Q = COL_Z + GDN_V
COL_QI = COL_Q + ATT_Q
COL_GA = COL_QI + IDX_Q
COL_GB = COL_GA + D_MODEL
COL_K = COL_GB + D_MODEL
COL_V = COL_K + ATT_KV
COL_SMALL = COL_V + ATT_KV
SMALL_W = 512
SM_A = IDX_DIM
SM_B = SM_A + GDN_HEADS
SM_WI = SM_B + GDN_HEADS
H_COLS = COL_SMALL + SMALL_W


def _cparams(sem):
    return pltpu.CompilerParams(dimension_semantics=sem, vmem_limit_bytes=VMEM_LIMIT)


def _dot(a, b):
    return jnp.dot(a, b, preferred_element_type=F32)


def _dot_bf(a, b):
    return jnp.dot(a.astype(BF16), b.astype(BF16), preferred_element_type=F32)


def _dot_hi(a, b):
    return jnp.dot(a, b, precision=lax.Precision.HIGHEST, preferred_element_type=F32)


def _dot_nt_hi(a, b):
    return lax.dot_general(a, b, (((1,), (1,)), ((), ())), precision=lax.Precision.HIGHEST,
                           preferred_element_type=F32)


def _dot_nt_bf(a, b):
    return lax.dot_general(a.astype(BF16), b.astype(BF16), (((1,), (1,)), ((), ())),
                           preferred_element_type=F32)


def _dot_tn_bf(a, b):
    return lax.dot_general(a.astype(BF16), b.astype(BF16), (((0,), (0,)), ((), ())),
                           preferred_element_type=F32)


def _sigmoid(x):
    return 1.0 / (1.0 + jnp.exp(-x))


def _silu(x):
    return x * _sigmoid(x)


def _round_up(n, m):
    return -(-n // m) * m


def _pick_tile(n, pref):
    t = min(n, pref)
    assert n % t == 0
    return t


def _repack_w_in(w_in):
    sizes = (CONV_CH, GDN_V, GDN_HEADS, GDN_HEADS, ATT_Q, ATT_KV, ATT_KV, IDX_Q, IDX_DIM, IDX_HEADS,
             D_MODEL, D_MODEL)
    offs = np.concatenate([[0], np.cumsum(sizes)])
    (p_qkv, p_z, p_a, p_b, p_q, p_k, p_v, p_qi, p_ki, p_wi, p_ga, p_gb) = [
        w_in[:, offs[i]:offs[i + 1]] for i in range(len(sizes))]
    pad = jnp.zeros((w_in.shape[0], SMALL_W - (SM_WI + IDX_HEADS)), w_in.dtype)
    return jnp.concatenate([p_qkv, p_z, p_q, p_qi, p_ga, p_gb, p_k, p_v, p_ki, p_a, p_b, p_wi, pad], axis=1)


def _proj_kernel(x_ref, w_ref, o_ref):
    o_ref[...] = _dot(x_ref[...].astype(BF16), w_ref[...])


def _project(x2d, w_bf):
    t, d = x2d.shape
    n = w_bf.shape[1]
    tm, tn = _pick_tile(t, PROJ_TM), _pick_tile(n, PROJ_TN)
    return pl.pallas_call(
        _proj_kernel,
        grid=(t // tm, n // tn),
        in_specs=[pl.BlockSpec((tm, d), lambda i, j: (i, 0)),
                  pl.BlockSpec((d, tn), lambda i, j: (0, j))],
        out_specs=pl.BlockSpec((tm, tn), lambda i, j: (i, j)),
        out_shape=jax.ShapeDtypeStruct((t, n), F32),
        compiler_params=_cparams(("parallel", "arbitrary")),
        name="in_proj",
    )(x2d, w_bf)


TAIL_ROWS = SUBLANES
INV_BASE = 8


def _split(a):
    hi = a.astype(BF16)
    return hi, (a - hi.astype(F32)).astype(BF16)


def _dot3(a, b):
    return _dot(a[0], b[0]) + (_dot(a[0], b[1]) + _dot(a[1], b[0]))


def _unit_lower_inverse_steps(ms, c, out):
    ri = lax.broadcasted_iota(jnp.int32, (c, c), 0)
    ci = lax.broadcasted_iota(jnp.int32, (c, c), 1)
    eye = (ri == ci).astype(F32)
    blk = INV_BASE
    same = (ri // blk) == (ci // blk)
    ns = [jnp.where(same, -m, 0.0) for m in ms]
    xs = [eye + n for n in ns]
    span = 1
    while span * 2 < blk:
        nsp = [_split(n) for n in ns]
        ns = [_dot3(n, n) for n in nsp]
        yield
        nsp = [_split(n) for n in ns]
        xs = [x + _dot3(_split(x), n) for x, n in zip(xs, nsp)]
        yield
        span *= 2
    while blk < c:
        nxt = blk * 2
        emask = ((ri // nxt) == (ci // nxt)) & ((ri // blk) != (ci // blk))
        xsp = [_split(x) for x in xs]
        ts = [_dot3(x, _split(jnp.where(emask, m, 0.0))) for x, m in zip(xsp, ms)]
        yield
        xs = [x - _dot3(_split(t), xp) for x, t, xp in zip(xs, ts, xsp)]
        yield
        blk = nxt
    out.extend(xs)


def _gdn_kernel(hq_ref, z_ref, sm_ref, buf_ref, s0_ref, cw_ref, alog_ref, dtb_ref, ng_ref,
                y_ref, snew_ref, bufnew_ref, xp_ref, s_ref, *, chunk):
    c_idx = pl.program_id(1)
    n_c = pl.num_programs(1)
    C = chunk
    keep = CONV_W - 1

    @pl.when(c_idx == 0)
    def _():
        xp_ref[:, TAIL_ROWS - keep:TAIL_ROWS, :] = buf_ref[...]
        s_ref[...] = s0_ref[...]

    ri = lax.broadcasted_iota(jnp.int32, (C, C), 0)
    ci = lax.broadcasted_iota(jnp.int32, (C, C), 1)
    incl = ri >= ci
    strict = ri > ci

    heads = range(GDN_HEADS)

    def prepare(bi, d):
        xp_ref[bi, TAIL_ROWS:TAIL_ROWS + C, :] = hq_ref[bi]
        acc = cw_ref[0:1, :] * xp_ref[bi, TAIL_ROWS - keep:TAIL_ROWS - keep + C, :]
        for j in range(1, CONV_W):
            acc = acc + cw_ref[j:j + 1, :] * xp_ref[bi, TAIL_ROWS - keep + j:TAIL_ROWS - keep + j + C, :]
        conv = _silu(acc)
        xp_ref[bi, TAIL_ROWS - keep:TAIL_ROWS, :] = xp_ref[bi, TAIL_ROWS + C - keep:TAIL_ROWS + C, :]
        yield
        sm = sm_ref[bi, :, 0:LANES]
        xg = sm + dtb_ref[...]
        softplus = jnp.maximum(xg, 0.0) + jnp.log(1.0 + jnp.exp(-jnp.abs(xg)))
        g = -jnp.exp(alog_ref[...]) * softplus
        beta = _sigmoid(sm)
        gc = _dot_hi(incl.astype(F32), g)
        gc_t = gc.T
        yield
        for name in ("q", "k", "v", "gc", "bh", "eg", "decay"):
            d[name] = []
        for h in heads:
            q = conv[:, h * GDN_DK:(h + 1) * GDN_DK]
            k = conv[:, GDN_QK + h * GDN_DK:GDN_QK + (h + 1) * GDN_DK]
            d["q"].append(q * lax.rsqrt(jnp.sum(q * q, -1, keepdims=True) + 1e-6) * (GDN_DK ** -0.5))
            d["k"].append(k * lax.rsqrt(jnp.sum(k * k, -1, keepdims=True) + 1e-6))
            d["v"].append(conv[:, 2 * GDN_QK + h * GDN_DV:2 * GDN_QK + (h + 1) * GDN_DV])
            gch = gc[:, SM_A + h:SM_A + h + 1]
            d["gc"].append(gch)
            d["bh"].append(beta[:, SM_B + h:SM_B + h + 1])
            d["eg"].append(jnp.exp(gch))
            d["decay"].append(jnp.exp(jnp.where(incl, gch - gc_t[SM_A + h:SM_A + h + 1, :], -jnp.inf)))
            yield

    def recur(bi, d):
        qs, ks, vs, gcs, bhs, egs, decays = (d[n] for n in ("q", "k", "v", "gc", "bh", "eg", "decay"))
        kbs = [k * bh for k, bh in zip(ks, bhs)]
        ms = [jnp.where(strict, _dot_nt_bf(kb, k) * dc, 0.0) for kb, k, dc in zip(kbs, ks, decays)]
        yield
        attns = [_dot_nt_bf(q, k) * dc for q, k, dc in zip(qs, ks, decays)]
        yield
        tinvs = []
        yield from _unit_lower_inverse_steps(ms, C, tinvs)
        sols = [_dot_bf(t, jnp.concatenate([v * bh, kb * eg], axis=-1))
                for t, v, bh, kb, eg in zip(tinvs, vs, bhs, kbs, egs)]
        yield
        ss = [s_ref[bi, h] for h in heads]
        v_news = [sol[:, :GDN_DV] - _dot_bf(sol[:, GDN_DV:], s) for sol, s in zip(sols, ss)]
        yield
        os_ = [_dot_bf(q * eg, s) + _dot_bf(a, vn) for q, eg, s, a, vn in zip(qs, egs, ss, attns, v_news)]
        yield
        for h in heads:
            glast = gcs[h][C - 1:C, :]
            s_ref[bi, h] = ss[h] * jnp.exp(glast) + _dot_tn_bf(ks[h] * jnp.exp(glast - gcs[h]), v_news[h])
        yield
        for h in heads:
            o = os_[h]
            o = o * lax.rsqrt(jnp.mean(o * o, -1, keepdims=True) + RMS_EPS) * ng_ref[...]
            zh = z_ref[bi, :, h * GDN_DV:(h + 1) * GDN_DV]
            y_ref[bi, :, h * GDN_DV:(h + 1) * GDN_DV] = o * _silu(zh)

    n_seq = hq_ref.shape[0]
    data = [{} for _ in range(n_seq)]
    for _ in prepare(0, data[0]):
        pass
    for bi in range(n_seq):
        filler = prepare(bi + 1, data[bi + 1]) if bi + 1 < n_seq else iter(())
        for _ in recur(bi, data[bi]):
            next(filler, None)
        for _ in filler:
            pass

    @pl.when(c_idx == n_c - 1)
    def _():
        snew_ref[...] = s_ref[...]
        bufnew_ref[...] = xp_ref[:, TAIL_ROWS - keep:TAIL_ROWS, :]


def _gdn(h3, l, conv_buf, s0, conv_w, a_log, dt_bias, norm_g):
    b = h3.shape[0]
    chunk = min(CHUNK, l)
    assert l % chunk == 0 and chunk % SUBLANES == 0 and chunk >= CONV_W - 1
    kern = functools.partial(_gdn_kernel, chunk=chunk)
    keep = CONV_W - 1
    lane_row = lambda vec, at: jnp.zeros((1, LANES), F32).at[0, at:at + vec.shape[0]].set(vec)
    gb = _pick_tile(b, GDN_BATCH_TILE)
    return pl.pallas_call(
        kern,
        grid=(b // gb, l // chunk),
        in_specs=[
            pl.BlockSpec((gb, chunk, CONV_CH), lambda i, c: (i, c, COL_QKV // CONV_CH)),
            pl.BlockSpec((gb, chunk, GDN_V), lambda i, c: (i, c, COL_Z // GDN_V)),
            pl.BlockSpec((gb, chunk, SMALL_W), lambda i, c: (i, c, COL_SMALL // SMALL_W)),
            pl.BlockSpec((gb, keep, CONV_CH), lambda i, c: (i, 0, 0)),
            pl.BlockSpec((gb, GDN_HEADS, GDN_DK, GDN_DV), lambda i, c: (i, 0, 0, 0)),
            pl.BlockSpec((CONV_W, CONV_CH), lambda i, c: (0, 0)),
            pl.BlockSpec((1, LANES), lambda i, c: (0, 0)),
            pl.BlockSpec((1, LANES), lambda i, c: (0, 0)),
            pl.BlockSpec((1, GDN_DV), lambda i, c: (0, 0)),
        ],
        out_specs=[
            pl.BlockSpec((gb, chunk, GDN_V), lambda i, c: (i, c, 0)),
            pl.BlockSpec((gb, GDN_HEADS, GDN_DK, GDN_DV), lambda i, c: (i, 0, 0, 0)),
            pl.BlockSpec((gb, keep, CONV_CH), lambda i, c: (i, 0, 0)),
        ],
        out_shape=[
            jax.ShapeDtypeStruct((b, l, GDN_V), F32),
            jax.ShapeDtypeStruct((b, GDN_HEADS, GDN_DK, GDN_DV), F32),
            jax.ShapeDtypeStruct((b, keep, CONV_CH), F32),
        ],
        scratch_shapes=[
            pltpu.VMEM((gb, TAIL_ROWS + chunk, CONV_CH), F32),
            pltpu.VMEM((gb, GDN_HEADS, GDN_DK, GDN_DV), F32),
        ],
        compiler_params=_cparams(("parallel", "arbitrary")),
        name="gdn",
    )(h3, h3, h3, conv_buf, s0, conv_w, lane_row(a_log, SM_A), lane_row(dt_bias, SM_A),
      norm_g.reshape(1, GDN_DV))


def _rope_tables(pos, rot, period):
    half = rot // 2
    inv_freq = ROPE_THETA ** (-(2.0 / rot) * jnp.arange(half, dtype=F32))
    ang = pos.astype(F32)[:, None] * inv_freq[None, :]
    cos, sin = jnp.cos(ang), jnp.sin(ang)
    n = pos.shape[0]
    rest = period - rot
    c = jnp.concatenate([cos, cos, jnp.ones((n, rest), F32)], -1)
    sa = jnp.concatenate([-sin, jnp.zeros((n, half + rest), F32)], -1)
    sb = jnp.concatenate([jnp.zeros((n, half), F32), sin, jnp.zeros((n, rest), F32)], -1)
    reps = LANES // period
    return jnp.stack([jnp.tile(c, (1, reps)), jnp.tile(sa, (1, reps)), jnp.tile(sb, (1, reps))], 0)


Q_SCALE_LOG2 = HEAD_DIM ** -0.5 * float(np.log2(np.e))


def _rope128(x, tab_ref, half):
    return (x * tab_ref[0] + pltpu.roll(x, LANES - half, 1) * tab_ref[1]
            + pltpu.roll(x, half, 1) * tab_ref[2])


def _prep_kernel(q_ref, qi_ref, k_ref, v_ref, sm_ref, tq_ref, ti_ref, lng_ref, lnb_ref,
                 qt_ref, qit_ref, wit_ref, ko_ref, kb_ref, kio_ref, kib_ref, vo_ref, vt_ref):
    hq = HEAD_DIM // 8
    hi = IDX_DIM // 8
    for h in range(ATT_HEADS):
        x = _rope128(q_ref[:, h * HEAD_DIM:(h + 1) * HEAD_DIM], tq_ref, hq) * Q_SCALE_LOG2
        qt_ref[h * HEAD_DIM:(h + 1) * HEAD_DIM, :] = x.T.astype(BF16)
    for h in range(ATT_KV_HEADS):
        x = _rope128(k_ref[:, h * HEAD_DIM:(h + 1) * HEAD_DIM], tq_ref, hq)
        ko_ref[:, h * HEAD_DIM:(h + 1) * HEAD_DIM] = x
        kb_ref[:, h * HEAD_DIM:(h + 1) * HEAD_DIM] = x.astype(BF16)
        v = v_ref[:, h * HEAD_DIM:(h + 1) * HEAD_DIM]
        vo_ref[:, h * HEAD_DIM:(h + 1) * HEAD_DIM] = v
        vt_ref[h * HEAD_DIM:(h + 1) * HEAD_DIM, :] = v.T.astype(BF16)
    for c in range(IDX_Q // LANES):
        x = _rope128(qi_ref[:, c * LANES:(c + 1) * LANES], ti_ref, hi)
        qit_ref[c * LANES:(c + 1) * LANES, :] = x.T.astype(BF16)
    sm = sm_ref[:, 0:LANES]
    lane = lax.broadcasted_iota(jnp.int32, sm.shape, 1)
    is_ki = lane < IDX_DIM
    mu = jnp.sum(jnp.where(is_ki, sm, 0.0), -1, keepdims=True) * (1.0 / IDX_DIM)
    xc = jnp.where(is_ki, sm - mu, 0.0)
    var = jnp.sum(xc * xc, -1, keepdims=True) * (1.0 / IDX_DIM)
    ki = xc * lax.rsqrt(var + LN_EPS) * lng_ref[...] + lnb_ref[...]
    ki = _rope128(ki, ti_ref, hi)[:, 0:IDX_DIM]
    kio_ref[...] = ki
    kib_ref[...] = ki.astype(BF16)
    wit_ref[...] = sm.T[SM_WI:SM_WI + IDX_HEADS, :] * (IDX_HEADS ** -0.5 * IDX_DIM ** -0.5)


def _prep(h3, pos, ln_g, ln_b):
    b, l, _ = h3.shape
    tm = _pick_tile(l, PREP_TM)
    tab_q = _rope_tables(pos, HEAD_DIM // 4, HEAD_DIM)
    tab_i = _rope_tables(pos, IDX_DIM // 4, IDX_DIM)
    lng = jnp.concatenate([ln_g, jnp.zeros((LANES - IDX_DIM,), F32)]).reshape(1, LANES)
    lnb = jnp.concatenate([ln_b, jnp.zeros((LANES - IDX_DIM,), F32)]).reshape(1, LANES)
    rows = lambda w, col: pl.BlockSpec((None, tm, w), lambda i, t: (i, t, col))
    cols = lambda w: pl.BlockSpec((None, w, tm), lambda i, t: (i, 0, t))
    return pl.pallas_call(
        _prep_kernel,
        grid=(b, l // tm),
        in_specs=[
            rows(ATT_Q, COL_Q // ATT_Q), rows(IDX_Q, COL_QI // IDX_Q), rows(ATT_KV, COL_K // ATT_KV),
            rows(ATT_KV, COL_V // ATT_KV), rows(SMALL_W, COL_SMALL // SMALL_W),
            pl.BlockSpec((3, tm, LANES), lambda i, t: (0, t, 0)),
            pl.BlockSpec((3, tm, LANES), lambda i, t: (0, t, 0)),
            pl.BlockSpec((1, LANES), lambda i, t: (0, 0)),
            pl.BlockSpec((1, LANES), lambda i, t: (0, 0)),
        ],
        out_specs=[cols(ATT_Q), cols(IDX_Q), cols(IDX_HEADS), rows(ATT_KV, 0), rows(ATT_KV, 0),
                   rows(IDX_DIM, 0), rows(IDX_DIM, 0), rows(ATT_KV, 0), cols(ATT_KV)],
        out_shape=[
            jax.ShapeDtypeStruct((b, ATT_Q, l), BF16),
            jax.ShapeDtypeStruct((b, IDX_Q, l), BF16),
            jax.ShapeDtypeStruct((b, IDX_HEADS, l), F32),
            jax.ShapeDtypeStruct((b, l, ATT_KV), F32),
            jax.ShapeDtypeStruct((b, l, ATT_KV), BF16),
            jax.ShapeDtypeStruct((b, l, IDX_DIM), F32),
            jax.ShapeDtypeStruct((b, l, IDX_DIM), BF16),
            jax.ShapeDtypeStruct((b, l, ATT_KV), F32),
            jax.ShapeDtypeStruct((b, ATT_KV, l), BF16),
        ],
        compiler_params=_cparams(("parallel", "parallel")),
        name="dsa_prep",
    )(h3, h3, h3, h3, h3, tab_q, tab_i, lng, lnb)


INT_MIN = -2 ** 31
INT_MAX = 2 ** 31 - 1
NEG_INF_KEY = INT_MIN + 0x7FFFFF
NEG_BIG = -1e30
HEAD_PAIRS = IDX_HEADS // 2
COUNT_CHAINS = 8
DENOM_ROWS = 16


def _dsa_kernel(qt_ref, qit_ref, wit_ref, ki_ref, k_ref, vt_ref, o_ref,
                key_ref, qs_ref, qip_ref, m_ref, acc_ref, cm_ref,
                *, tq, kb, l_true, pos0, topk, idx_bits):
    q0 = pos0 + pl.program_id(1) * tq
    qpos = q0 + lax.broadcasted_iota(jnp.int32, (1, tq), 1)
    lim = jnp.minimum((qpos // CHUNK + 1) * CHUNK, l_true)
    lim_max = jnp.minimum(((q0 + tq - 1) // CHUNK + 1) * CHUNK, l_true)
    nkb = (lim_max + kb - 1) // kb
    key_iota = lax.broadcasted_iota(jnp.int32, (kb, tq), 0)
    wit = wit_ref[...]

    for p in range(HEAD_PAIRS):
        for u in range(2):
            hh = 2 * p + u
            qip_ref[p, :, u * tq:(u + 1) * tq] = qit_ref[hh * IDX_DIM:(hh + 1) * IDX_DIM, :]
    for g in range(ATT_KV_HEADS):
        for r in range(REP):
            hh = g * REP + r
            qs_ref[g, :, r * tq:(r + 1) * tq] = qt_ref[hh * HEAD_DIM:(hh + 1) * HEAD_DIM, :]

    def score_blk(j, carry):
        off = pl.multiple_of(j * kb, kb)
        ki = ki_ref[pl.ds(off, kb), :]
        acc = jnp.zeros((kb, tq), F32)
        for p in range(HEAD_PAIRS):
            s2 = _dot(ki, qip_ref[p])
            acc = (acc + wit[2 * p:2 * p + 1, :] * jnp.maximum(s2[:, :tq], 0.0)
                   + wit[2 * p + 1:2 * p + 2, :] * jnp.maximum(s2[:, tq:], 0.0))
        score = jnp.where(off + key_iota < lim, acc, -jnp.inf)
        bits = pltpu.bitcast(score, jnp.int32)
        key_ref[pl.ds(off, kb), :] = bits ^ ((bits >> 31) & INT_MAX)
        return carry

    lax.fori_loop(0, nkb, score_blk, 0)

    def count(pred_fn):
        def blk(j, c):
            off = pl.multiple_of(j * kb, kb)
            part = jnp.where(pred_fn(key_ref[pl.ds(off, kb), :], off), 1.0, 0.0)
            return c + jnp.sum(part.reshape(kb // (COUNT_CHAINS * SUBLANES), COUNT_CHAINS * SUBLANES, tq), axis=0)
        c = lax.fori_loop(0, nkb, blk, jnp.zeros((COUNT_CHAINS * SUBLANES, tq), F32))
        return jnp.sum(c, axis=0, keepdims=True)

    def bit_step(t, cur):
        cand_u = cur | lax.shift_left(jnp.int32(1), 31 - t)
        cand_s = cand_u ^ INT_MIN
        cnt = count(lambda kk, off: kk >= cand_s)
        return jnp.where(cnt >= topk, cand_u, cur)

    thr = lax.fori_loop(0, 32, bit_step, jnp.zeros((1, tq), jnp.int32)) ^ INT_MIN
    n_ge = count(lambda kk, off: kk >= thr)

    cm_ref[...] = jnp.full((1, tq), INT_MAX, jnp.int32)

    @pl.when(jnp.max(jnp.where((n_ge > topk) & (thr > NEG_INF_KEY), 1, 0)) > 0)
    def _():
        need = topk - count(lambda kk, off: kk > thr)

        def idx_step(t, cm):
            cand = cm | lax.shift_left(jnp.int32(1), idx_bits - 1 - t)
            before = count(lambda kk, off: (kk == thr) & (off + key_iota < cand))
            return jnp.where(before < need, cand, cm)
        cm_ref[...] = lax.fori_loop(0, idx_bits, idx_step, jnp.zeros((1, tq), jnp.int32))

    cm = cm_ref[...]

    m_ref[...] = jnp.full(m_ref.shape, NEG_BIG, F32)
    acc_ref[...] = jnp.zeros(acc_ref.shape, F32)
    groups = range(ATT_KV_HEADS)
    ones_rows = jnp.ones((DENOM_ROWS, kb), BF16)

    def attn_blk(j):
        off = pl.multiple_of(j * kb, kb)
        kk = key_ref[pl.ds(off, kb), :]
        kpos = off + key_iota
        sel = ((kk > thr) | ((kk == thr) & (kpos <= cm))) & (kpos < lim)
        bias = jnp.where(sel, 0.0, NEG_BIG)
        bias = jnp.concatenate([bias] * REP, axis=1)
        logits = [_dot(k_ref[pl.ds(off, kb), g * HEAD_DIM:(g + 1) * HEAD_DIM], qs_ref[g]) + bias for g in groups]
        m_old = [m_ref[g] for g in groups]
        m_new = [jnp.maximum(m_old[g], jnp.max(logits[g], axis=0, keepdims=True)) for g in groups]
        ps = [jnp.exp2(logits[g] - m_new[g]).astype(BF16) for g in groups]
        alphas = [jnp.exp2(m_old[g] - m_new[g]) for g in groups]
        pvs = [_dot(jnp.concatenate([vt_ref[g * HEAD_DIM:(g + 1) * HEAD_DIM, pl.ds(off, kb)], ones_rows], axis=0),
                    ps[g]) for g in groups]
        for g in groups:
            acc_ref[g] = alphas[g] * acc_ref[g] + pvs[g]
            m_ref[g] = m_new[g]

    def attn_pair(jp, carry):
        attn_blk(2 * jp)
        attn_blk(2 * jp + 1)
        return carry

    lax.fori_loop(0, nkb // 2, attn_pair, 0)

    @pl.when(nkb % 2 == 1)
    def _():
        attn_blk(nkb - 1)

    for g in groups:
        o_t = acc_ref[g, 0:HEAD_DIM, :] / acc_ref[g, HEAD_DIM:HEAD_DIM + 1, :]
        for r in range(REP):
            hh = g * REP + r
            o_ref[:, hh * HEAD_DIM:(hh + 1) * HEAD_DIM] = o_t[:, r * tq:(r + 1) * tq].T


def _dsa(qt, qit, wit, ki, k, vt, l_true, pos0):
    b, _, t = qt.shape
    lk = k.shape[1]
    tq, kb = DSA_Q_TILE, DSA_KEY_BLOCK
    assert t % tq == 0 and lk % kb == 0 and tq == HEAD_DIM
    topk = min(TOPK_MAX, l_true // 4)
    kern = functools.partial(_dsa_kernel, tq=tq, kb=kb, l_true=l_true, pos0=pos0, topk=topk,
                             idx_bits=int(lk).bit_length())
    return pl.pallas_call(
        kern,
        grid=(b, t // tq),
        in_specs=[
            pl.BlockSpec((None, ATT_Q, tq), lambda i, t_: (i, 0, t_)),
            pl.BlockSpec((None, IDX_Q, tq), lambda i, t_: (i, 0, t_)),
            pl.BlockSpec((None, IDX_HEADS, tq), lambda i, t_: (i, 0, t_)),
            pl.BlockSpec((None, lk, IDX_DIM), lambda i, t_: (i, 0, 0)),
            pl.BlockSpec((None, lk, ATT_KV), lambda i, t_: (i, 0, 0)),
            pl.BlockSpec((None, ATT_KV, lk), lambda i, t_: (i, 0, 0)),
        ],
        out_specs=pl.BlockSpec((None, tq, ATT_Q), lambda i, t_: (i, t_, 0)),
        out_shape=jax.ShapeDtypeStruct((b, t, ATT_Q), F32),
        scratch_shapes=[
            pltpu.VMEM((lk, tq), jnp.int32),
            pltpu.VMEM((ATT_KV_HEADS, HEAD_DIM, REP * tq), BF16),
            pltpu.VMEM((HEAD_PAIRS, IDX_DIM, 2 * tq), BF16),
            pltpu.VMEM((ATT_KV_HEADS, 1, REP * tq), F32),
            pltpu.VMEM((ATT_KV_HEADS, HEAD_DIM + DENOM_ROWS, REP * tq), F32),
            pltpu.VMEM((1, tq), jnp.int32),
        ],
        compiler_params=_cparams(("parallel", "arbitrary")),
        name="dsa",
    )(qt, qit, wit, ki, k, vt)


def _layer_norm(x, g, b):
    mu = jnp.mean(x, -1, keepdims=True)
    xc = x - mu
    var = jnp.mean(xc * xc, -1, keepdims=True)
    return xc * lax.rsqrt(var + LN_EPS) * g + b


def _merge_kernel(ya_ref, yb_ref, ga_ref, gb_ref, x_ref, wa_ref, wb_ref, wo_ref, g_ref, b_ref, o_ref):
    pa = _dot(ya_ref[...].astype(BF16), wa_ref[...])
    pb = _dot(yb_ref[...].astype(BF16), wb_ref[...])
    merged = _sigmoid(ga_ref[...]) * pa + _sigmoid(gb_ref[...]) * pb
    y = DN_ALPHA * x_ref[...] + _dot(merged.astype(BF16), wo_ref[...])
    o_ref[...] = _layer_norm(y, g_ref[...], b_ref[...])


def _merge(ya, yb, h3, x, wa, wb, wo, g, b):
    bsz, l, _ = x.shape
    tm = _pick_tile(l, TOKEN_TM)
    row = lambda c: pl.BlockSpec((None, tm, D_MODEL), lambda i, t: (i, t, c))
    full = lambda shp: pl.BlockSpec(shp, lambda i, t: (0, 0))
    return pl.pallas_call(
        _merge_kernel,
        grid=(bsz, l // tm),
        in_specs=[row(0), row(0), row(COL_GA // D_MODEL), row(COL_GB // D_MODEL), row(0),
                  full((GDN_V, D_MODEL)), full((ATT_Q, D_MODEL)), full((D_MODEL, D_MODEL)),
                  full((1, D_MODEL)), full((1, D_MODEL))],
        out_specs=row(0),
        out_shape=jax.ShapeDtypeStruct((bsz, l, D_MODEL), F32),
        compiler_params=_cparams(("parallel", "parallel")),
        name="merge_ln1",
    )(ya, yb, h3, h3, x, wa, wb, wo, g.reshape(1, D_MODEL), b.reshape(1, D_MODEL))


GROUP_SIZE = N_EXPERTS // N_GROUPS


def _first_max(cur, rows, n_rows):
    m = jnp.max(cur, axis=0, keepdims=True)
    idx = jnp.min(jnp.where(cur == m, rows, n_rows), axis=0, keepdims=True)
    return m, idx


def _router_kernel(x_ref, wr_ref, rb_ref, ws1_ref, ws3_ref, ws2_ref,
                   eidx_ref, wts_ref, rank_ref, cnt_ref, base_ref, run_ref):
    @pl.when(pl.program_id(0) == 0)
    def _():
        run_ref[...] = jnp.zeros(run_ref.shape, F32)

    x = x_ref[...]
    tm = x.shape[0]
    logits = _dot_nt_hi(wr_ref[...], x)
    scores = _sigmoid(logits)
    biased = scores + rb_ref[...]
    neg = -jnp.inf

    rows_g = lax.broadcasted_iota(jnp.int32, (GROUP_SIZE, tm), 0)
    gs = []
    for g in range(N_GROUPS):
        blk = biased[g * GROUP_SIZE:(g + 1) * GROUP_SIZE, :]
        m1, i1 = _first_max(blk, rows_g, GROUP_SIZE)
        m2 = jnp.max(jnp.where(rows_g == i1, neg, blk), axis=0, keepdims=True)
        gs.append(m1 + m2)
    cur = jnp.concatenate(gs, axis=0)
    rows_n = lax.broadcasted_iota(jnp.int32, (N_GROUPS, tm), 0)
    gsel = jnp.zeros((N_GROUPS, tm), F32)
    for _ in range(TOPK_GROUPS):
        _, ig = _first_max(cur, rows_n, N_GROUPS)
        hit = rows_n == ig
        gsel = jnp.where(hit, 1.0, gsel)
        cur = jnp.where(hit, neg, cur)
    gexp = jnp.concatenate([jnp.broadcast_to(gsel[g:g + 1, :], (GROUP_SIZE, tm)) for g in range(N_GROUPS)], axis=0)
    cur = jnp.where(gexp > 0.0, biased, neg)

    rows_e = lax.broadcasted_iota(jnp.int32, (N_EXPERTS, tm), 0)
    es, ws, hits = [], [], []
    for _ in range(EXPERT_TOPK):
        _, ie = _first_max(cur, rows_e, N_EXPERTS)
        hit = rows_e == ie
        es.append(ie)
        hits.append(hit)
        ws.append(jnp.sum(jnp.where(hit, scores, 0.0), axis=0, keepdims=True))
        cur = jnp.where(hit, neg, cur)
    w = jnp.concatenate(ws, axis=0)
    eidx_ref[...] = jnp.concatenate(es, axis=0)
    wts_ref[...] = w / jnp.sum(w, axis=0, keepdims=True) * ROUTED_SCALE

    assigned = jnp.zeros((N_EXPERTS, tm), F32)
    for hit in hits:
        assigned = jnp.where(hit, 1.0, assigned)
    earlier = (lax.broadcasted_iota(jnp.int32, (tm, tm), 0) < lax.broadcasted_iota(jnp.int32, (tm, tm), 1))
    before = run_ref[...] + _dot(assigned.astype(BF16), earlier.astype(BF16))
    rank_ref[...] = jnp.concatenate(
        [jnp.sum(jnp.where(hit, before, 0.0), axis=0, keepdims=True) for hit in hits], axis=0).astype(jnp.int32)
    run_ref[...] = run_ref[...] + jnp.sum(assigned, axis=1, keepdims=True)
    cnt_ref[...] = run_ref[...].astype(jnp.int32)

    xb = x.astype(BF16)
    hs = _silu(_dot(xb, ws1_ref[...])) * _dot(xb, ws3_ref[...])
    base_ref[...] = DN_ALPHA * x + _dot(hs.astype(BF16), ws2_ref[...])


def _router(x1, wr_t, rbias, ws1, ws3, ws2, tm):
    t = x1.shape[0]
    full = lambda shp: pl.BlockSpec(shp, lambda i: (0, 0))
    return pl.pallas_call(
        _router_kernel,
        grid=(t // tm,),
        in_specs=[pl.BlockSpec((tm, D_MODEL), lambda i: (i, 0)),
                  full((N_EXPERTS, D_MODEL)), full((N_EXPERTS, 1)),
                  full((D_MODEL, SHARED_DIM)), full((D_MODEL, SHARED_DIM)), full((SHARED_DIM, D_MODEL))],
        out_specs=[pl.BlockSpec((EXPERT_TOPK, tm), lambda i: (0, i)),
                   pl.BlockSpec((EXPERT_TOPK, tm), lambda i: (0, i)),
                   pl.BlockSpec((EXPERT_TOPK, tm), lambda i: (0, i)),
                   full((N_EXPERTS, 1)),
                   pl.BlockSpec((tm, D_MODEL), lambda i: (i, 0))],
        out_shape=[jax.ShapeDtypeStruct((EXPERT_TOPK, t), jnp.int32),
                   jax.ShapeDtypeStruct((EXPERT_TOPK, t), F32),
                   jax.ShapeDtypeStruct((EXPERT_TOPK, t), jnp.int32),
                   jax.ShapeDtypeStruct((N_EXPERTS, 1), jnp.int32),
                   jax.ShapeDtypeStruct((t, D_MODEL), F32)],
        scratch_shapes=[pltpu.VMEM((N_EXPERTS, 1), F32)],
        compiler_params=_cparams(("arbitrary",)),
        name="router_shared",
    )(x1, wr_t, rbias.reshape(N_EXPERTS, 1), ws1, ws3, ws2)


def _block_plan(counts, n_blocks):
    padded = (counts + MOE_ROWS - 1) // MOE_ROWS * MOE_ROWS
    pad_end = jnp.cumsum(padded)
    blk_first = jnp.arange(n_blocks, dtype=jnp.int32) * MOE_ROWS
    blk_exp = jnp.minimum(jnp.sum((pad_end[None, :] <= blk_first[:, None]).astype(jnp.int32), axis=1), N_EXPERTS - 1)
    return pad_end - padded, jnp.maximum(pad_end - MOE_ROWS, 0), blk_exp, pad_end[-1:] // MOE_ROWS


def _pos_kernel(e_ref, r_ref, ps_ref, pos_ref):
    tm = e_ref.shape[1]
    rows_e = lax.broadcasted_iota(jnp.int32, (N_EXPERTS, tm), 0)
    first = [jnp.sum(jnp.where(rows_e == e_ref[j:j + 1, :], ps_ref[...], 0), axis=0, keepdims=True)
             for j in range(EXPERT_TOPK)]
    pos_ref[...] = jnp.concatenate(first, axis=0) + r_ref[...]


def _positions(eidx_t, rank_t, pad_start, tm):
    t = eidx_t.shape[1]
    blk = pl.BlockSpec((EXPERT_TOPK, tm), lambda i: (0, i))
    return pl.pallas_call(
        _pos_kernel,
        grid=(t // tm,),
        in_specs=[blk, blk, pl.BlockSpec((N_EXPERTS, 1), lambda i: (0, 0))],
        out_specs=blk,
        out_shape=jax.ShapeDtypeStruct((EXPERT_TOPK, t), jnp.int32),
        compiler_params=_cparams(("parallel",)),
        name="moe_positions",
    )(eidx_t, rank_t, pad_start.reshape(N_EXPERTS, 1))


def _scatter_kernel(last_ref, nu_ref, pos_ref, x_ref, rows_ref, zero_ref, sem, zsem):
    tm = x_ref.shape[0]
    n_blocks = rows_ref.shape[0] // MOE_ROWS

    @pl.when(pl.program_id(0) == 0)
    def _():
        zero_ref[...] = jnp.zeros(zero_ref.shape, F32)

        def zero_copy(first_row):
            first = pl.multiple_of(first_row, MOE_ROWS)
            return pltpu.make_async_copy(zero_ref, rows_ref.at[pl.ds(first, MOE_ROWS), :], zsem)

        def each(fn):
            def expert_block(e, carry):
                fn(zero_copy(last_ref[e]))
                return carry

            def unused_block(blk, carry):
                fn(zero_copy(blk * MOE_ROWS))
                return carry

            lax.fori_loop(0, N_EXPERTS, expert_block, 0)
            lax.fori_loop(nu_ref[0], n_blocks, unused_block, 0)

        each(lambda copy: copy.start())
        each(lambda copy: copy.wait())

    def row_copy(t, j):
        return pltpu.make_async_copy(x_ref.at[pl.ds(t, 1), :], rows_ref.at[pl.ds(pos_ref[j, t], 1), :], sem)

    def issue(t, carry):
        for j in range(EXPERT_TOPK):
            row_copy(t, j).start(priority=j % DMA_PRIORITIES)
        return carry

    lax.fori_loop(0, tm, issue, 0)
    for j in range(EXPERT_TOPK):
        pltpu.make_async_copy(x_ref, rows_ref.at[pl.ds(0, tm), :], sem).wait()


def _scatter_rows(last_block_row, n_used, pos_t, x1, n_rows, tm):
    t = x1.shape[0]
    grid_spec = pltpu.PrefetchScalarGridSpec(
        num_scalar_prefetch=2,
        grid=(t // tm,),
        in_specs=[pl.BlockSpec((EXPERT_TOPK, tm), lambda i, lr, nu: (0, i), memory_space=pltpu.SMEM),
                  pl.BlockSpec((tm, D_MODEL), lambda i, lr, nu: (i, 0))],
        out_specs=pl.BlockSpec(memory_space=pl.ANY),
        scratch_shapes=[pltpu.VMEM((MOE_ROWS, D_MODEL), F32), pltpu.SemaphoreType.DMA(()),
                        pltpu.SemaphoreType.DMA(())],
    )
    return pl.pallas_call(
        _scatter_kernel,
        grid_spec=grid_spec,
        out_shape=jax.ShapeDtypeStruct((n_rows, D_MODEL), F32),
        compiler_params=_cparams(("arbitrary",)),
        name="moe_scatter",
    )(last_block_row, n_used, pos_t, x1)


def _expert_kernel(be_ref, nu_ref, x_ref, w1_ref, w3_ref, w2_ref, o_ref, w1b_ref, w3b_ref, w2b_ref):
    i = pl.program_id(0)
    prev = be_ref[jnp.maximum(i - 1, 0)]

    @pl.when((i == 0) | (be_ref[i] != prev))
    def _():
        w1b_ref[...] = w1_ref[...].astype(BF16)
        w3b_ref[...] = w3_ref[...].astype(BF16)
        w2b_ref[...] = w2_ref[...].astype(BF16)

    @pl.when(i < nu_ref[0])
    def _():
        x = x_ref[...].astype(BF16)
        hmid = _silu(_dot(x, w1b_ref[...])) * _dot(x, w3b_ref[...])
        o_ref[...] = _dot(hmid.astype(BF16), w2b_ref[...])

    @pl.when(i >= nu_ref[0])
    def _():
        o_ref[...] = jnp.zeros(o_ref.shape, F32)


def _experts(xg, blk_exp, n_used, w1, w3, w2):
    rows = xg.shape[0]
    n_blocks = rows // MOE_ROWS
    grid_spec = pltpu.PrefetchScalarGridSpec(
        num_scalar_prefetch=2,
        grid=(n_blocks,),
        in_specs=[
            pl.BlockSpec((MOE_ROWS, D_MODEL), lambda i, be, nu: (jnp.minimum(i, nu[0] - 1), 0)),
            pl.BlockSpec((None, D_MODEL, EXPERT_DIM), lambda i, be, nu: (be[i], 0, 0)),
            pl.BlockSpec((None, D_MODEL, EXPERT_DIM), lambda i, be, nu: (be[i], 0, 0)),
            pl.BlockSpec((None, EXPERT_DIM, D_MODEL), lambda i, be, nu: (be[i], 0, 0)),
        ],
        out_specs=pl.BlockSpec((MOE_ROWS, D_MODEL), lambda i, be, nu: (i, 0)),
        scratch_shapes=[pltpu.VMEM((D_MODEL, EXPERT_DIM), BF16), pltpu.VMEM((D_MODEL, EXPERT_DIM), BF16),
                        pltpu.VMEM((EXPERT_DIM, D_MODEL), BF16)],
    )
    return pl.pallas_call(
        _expert_kernel,
        grid_spec=grid_spec,
        out_shape=jax.ShapeDtypeStruct((rows, D_MODEL), F32),
        compiler_params=_cparams(("arbitrary",)),
        name="experts",
    )(blk_exp, n_used, xg, w1, w3, w2)


def _combine_kernel(pos_ref, pos_next_ref, y_ref, w_ref, base_ref, g_ref, b_ref, o_ref, gbuf_ref, sem):
    tm = base_ref.shape[0]
    i = pl.program_id(0)
    slot = lax.rem(i, 2)

    def issue_tile(p_ref, s):
        def issue(t, carry):
            for j in range(EXPERT_TOPK):
                pltpu.make_async_copy(y_ref.at[pl.ds(p_ref[j, t], 1), :], gbuf_ref.at[s, j, pl.ds(t, 1), :],
                                      sem.at[s]).start(priority=j % DMA_PRIORITIES)
            return carry
        lax.fori_loop(0, tm, issue, 0)

    @pl.when(i == 0)
    def _():
        issue_tile(pos_ref, 0)

    @pl.when(i + 1 < pl.num_programs(0))
    def _():
        issue_tile(pos_next_ref, 1 - slot)

    for j in range(EXPERT_TOPK):
        pltpu.make_async_copy(y_ref.at[pl.ds(0, tm), :], gbuf_ref.at[slot, j], sem.at[slot]).wait()

    acc = base_ref[...]
    w = w_ref[...]
    for j in range(EXPERT_TOPK):
        acc = acc + w[:, j:j + 1] * gbuf_ref[slot, j]
    o_ref[...] = _layer_norm(acc, g_ref[...], b_ref[...])


def _combine(pos_t, y_rows, wts, base, g, b, tm):
    t = base.shape[0]
    n_tiles = t // tm
    return pl.pallas_call(
        _combine_kernel,
        grid=(n_tiles,),
        in_specs=[pl.BlockSpec((EXPERT_TOPK, tm), lambda i: (0, i), memory_space=pltpu.SMEM),
                  pl.BlockSpec((EXPERT_TOPK, tm), lambda i: (0, jnp.minimum(i + 1, n_tiles - 1)),
                               memory_space=pltpu.SMEM),
                  pl.BlockSpec(memory_space=pl.ANY),
                  pl.BlockSpec((tm, EXPERT_TOPK), lambda i: (i, 0)),
                  pl.BlockSpec((tm, D_MODEL), lambda i: (i, 0)),
                  pl.BlockSpec((1, D_MODEL), lambda i: (0, 0)),
                  pl.BlockSpec((1, D_MODEL), lambda i: (0, 0))],
        out_specs=pl.BlockSpec((tm, D_MODEL), lambda i: (i, 0)),
        out_shape=jax.ShapeDtypeStruct((t, D_MODEL), F32),
        scratch_shapes=[pltpu.VMEM((2, EXPERT_TOPK, tm, D_MODEL), F32), pltpu.SemaphoreType.DMA((2,))],
        compiler_params=_cparams(("arbitrary",)),
        name="combine_ln2",
    )(pos_t, pos_t, y_rows, wts, base, g.reshape(1, D_MODEL), b.reshape(1, D_MODEL))


def _moe(x1, prm):
    t = x1.shape[0]
    tm = _pick_tile(t, TOKEN_TM)
    eidx_t, wts_t, rank_t, counts, base = _router(x1, prm["w_router_t"], prm["router_bias"], prm["ws1"],
                                                  prm["ws3"], prm["ws2"], tm)
    n_blocks = -(-t * EXPERT_TOPK // MOE_ROWS) + N_EXPERTS
    pad_start, last_block_row, blk_exp, n_used = _block_plan(counts.reshape(N_EXPERTS), n_blocks)
    pos_t = _positions(eidx_t, rank_t, pad_start, tm)
    x_rows = _scatter_rows(last_block_row, n_used, pos_t, x1, n_blocks * MOE_ROWS, tm)
    y_rows = _experts(x_rows, blk_exp, n_used, prm["w1"], prm["w3"], prm["w2"])
    return _combine(pos_t, y_rows, wts_t.T, base, prm["ln2_g"], prm["ln2_b"], _pick_tile(t, COMBINE_TM))


def _mixer(x, pos0, conv_buf, s0, k_past, v_past, ik_past, prm):
    b, l, _ = x.shape
    lp = _round_up(l, DSA_Q_TILE)
    xp = x if lp == l else jnp.pad(x, ((0, 0), (0, lp - l), (0, 0)))
    h3 = _project(xp.reshape(b * lp, D_MODEL), prm["w_in"]).reshape(b, lp, H_COLS)

    ya, s_new, buf_new = _gdn(h3, l, conv_buf, s0, prm["conv_w"], prm["a_log"], prm["dt_bias"], prm["gdn_norm_g"])

    pos = pos0 + jnp.arange(lp, dtype=jnp.int32)
    qt, qit, wit, k_new, k_bf, ki_new, ki_bf, v_new, vt = _prep(h3, pos, prm["idx_k_ln_g"], prm["idx_k_ln_b"])
    if k_past is None:
        l_all = l
        k_all, ki_all, vt_all = k_bf, ki_bf, vt
    else:
        past = k_past.shape[1]
        l_all = past + l
        k_all = jnp.concatenate([k_past.reshape(b, past, ATT_KV).astype(BF16), k_bf[:, :l]], 1)
        ki_all = jnp.concatenate([ik_past.astype(BF16), ki_bf[:, :l]], 1)
        vt_all = jnp.concatenate([jnp.swapaxes(v_past.reshape(b, past, ATT_KV), 1, 2).astype(BF16), vt[:, :, :l]], 2)
    lk = _round_up(l_all, DSA_KEY_BLOCK)
    if lk != l_all:
        k_all = jnp.pad(k_all, ((0, 0), (0, lk - l_all), (0, 0)))
        ki_all = jnp.pad(ki_all, ((0, 0), (0, lk - l_all), (0, 0)))
        vt_all = jnp.pad(vt_all, ((0, 0), (0, 0), (0, lk - l_all)))
    yb = _dsa(qt, qit, wit, ki_all, k_all, vt_all, l_all, pos0)

    x1 = _merge(ya, yb, h3, x, prm["w_o_gdn"], prm["w_o_dsa"], prm["w_out"], prm["ln1_g"], prm["ln1_b"])
    state = (k_new[:, :l].reshape(b, l, ATT_KV_HEADS, HEAD_DIM), v_new[:, :l].reshape(b, l, ATT_KV_HEADS, HEAD_DIM),
             ki_new[:, :l], s_new, buf_new)
    return x1, state


def kernel(x_prompt, x_sample, cache_k, cache_v, cache_idx_k, state_gdn, state_conv, w_in, conv_w, a_log, dt_bias, gdn_norm_g, w_o_gdn, idx_k_ln_g, idx_k_ln_b, w_o_dsa, w_out, ln1_g, ln1_b, w_router, router_bias, w1, w3, w2, ws1, ws3, ws2, ln2_g, ln2_b):
    assert w_in.shape[0] == DEPTH == 1
    bp, lp_, _ = x_prompt.shape
    bs, ls_, _ = x_sample.shape
    past = cache_k.shape[2]
    prm = dict(
        w_in=_repack_w_in(w_in[0]).astype(BF16), conv_w=conv_w[0], a_log=a_log[0], dt_bias=dt_bias[0],
        gdn_norm_g=gdn_norm_g[0], w_o_gdn=w_o_gdn[0].astype(BF16), idx_k_ln_g=idx_k_ln_g[0],
        idx_k_ln_b=idx_k_ln_b[0], w_o_dsa=w_o_dsa[0].astype(BF16), w_out=w_out[0].astype(BF16),
        ln1_g=ln1_g[0], ln1_b=ln1_b[0], w_router_t=w_router[0].T, router_bias=router_bias[0],
        w1=w1[0], w3=w3[0], w2=w2[0], ws1=ws1[0].astype(BF16), ws3=ws3[0].astype(BF16),
        ws2=ws2[0].astype(BF16), ln2_g=ln2_g[0], ln2_b=ln2_b[0])
    conv0 = jnp.zeros((bp, CONV_W - 1, CONV_CH), F32)
    s0 = jnp.zeros((bp, GDN_HEADS, GDN_DK, GDN_DV), F32)
    x1p, sp = _mixer(x_prompt, 0, conv0, s0, None, None, None, prm)
    x1s, ss = _mixer(x_sample, past, state_conv[0], state_gdn[0], cache_k[0], cache_v[0], cache_idx_k[0], prm)
    tp, ts = bp * lp_, bs * ls_
    y = _moe(jnp.concatenate([x1p.reshape(tp, D_MODEL), x1s.reshape(ts, D_MODEL)], 0), prm)
    yp = y[:tp].reshape(bp, lp_, D_MODEL)
    ys = y[tp:].reshape(bs, ls_, D_MODEL)
    return (yp, ys) + tuple(a[None] for a in sp) + tuple(a[None] for a in ss)
```

```python
import functools

import jax
import jax.numpy as jnp
import numpy as np
from jax import lax
from jax.experimental import pallas as pl
from jax.experimental.pallas import tpu as pltpu

F32 = jnp.float32
BF16 = jnp.bfloat16

D_MODEL = 1024
CHUNK = 64
GDN_HEADS = 8
GDN_DK = 128
GDN_DV = 128
CONV_W = 4
ATT_HEADS = 8
ATT_KV_HEADS = 2
HEAD_DIM = 128
IDX_HEADS = 16
IDX_DIM = 64
TOPK_MAX = 256
ROPE_THETA = 500000.0
N_EXPERTS = 256
EXPERT_TOPK = 8
N_GROUPS = 8
TOPK_GROUPS = 4
EXPERT_DIM = 256
SHARED_DIM = 256
ROUTED_SCALE = 2.5
DEPTH = 1
DN_ALPHA = (2.0 * DEPTH) ** 0.25
LN_EPS = 1e-5
RMS_EPS = 1e-6

GDN_QK = GDN_HEADS * GDN_DK
GDN_V = GDN_HEADS * GDN_DV
CONV_CH = 2 * GDN_QK + GDN_V
ATT_Q = ATT_HEADS * HEAD_DIM
ATT_KV = ATT_KV_HEADS * HEAD_DIM
IDX_Q = IDX_HEADS * IDX_DIM
REP = ATT_HEADS // ATT_KV_HEADS

LANES = 128
SUBLANES = 8
VMEM_LIMIT = 56 * 1024 * 1024
DMA_PRIORITIES = 2

PROJ_TM = 1024
PROJ_TN = 1024
PREP_TM = 512
DSA_Q_TILE = 128
DSA_KEY_BLOCK = 512
TOKEN_TM = 256
COMBINE_TM = 128
GDN_BATCH_TILE = 4
MOE_ROWS = 256

COL_QKV = 0
COL_Z = COL_QKV + CONV_CH
COL_Q = COL_Z + GDN_V
COL_QI = COL_Q + ATT_Q
COL_GA = COL_QI + IDX_Q
COL_GB = COL_GA + D_MODEL
COL_K = COL_GB + D_MODEL
COL_V = COL_K + ATT_KV
COL_SMALL = COL_V + ATT_KV
SMALL_W = 512
SM_A = IDX_DIM
SM_B = SM_A + GDN_HEADS
SM_WI = SM_B + GDN_HEADS
H_COLS = COL_SMALL + SMALL_W


def _cparams(sem):
    return pltpu.CompilerParams(dimension_semantics=sem, vmem_limit_bytes=VMEM_LIMIT)


def _dot(a, b):
    return jnp.dot(a, b, preferred_element_type=F32)


def _dot_bf(a, b):
    return jnp.dot(a.astype(BF16), b.astype(BF16), preferred_element_type=F32)


def _dot_hi(a, b):
    return jnp.dot(a, b, precision=lax.Precision.HIGHEST, preferred_element_type=F32)


def _dot_nt_hi(a, b):
    return lax.dot_general(a, b, (((1,), (1,)), ((), ())), precision=lax.Precision.HIGHEST,
                           preferred_element_type=F32)


def _dot_nt_bf(a, b):
    return lax.dot_general(a.astype(BF16), b.astype(BF16), (((1,), (1,)), ((), ())),
                           preferred_element_type=F32)


def _dot_tn_bf(a, b):
    return lax.dot_general(a.astype(BF16), b.astype(BF16), (((0,), (0,)), ((), ())),
                           preferred_element_type=F32)


def _sigmoid(x):
    return 1.0 / (1.0 + jnp.exp(-x))


def _silu(x):
    return x * _sigmoid(x)


def _round_up(n, m):
    return -(-n // m) * m


def _pick_tile(n, pref):
    t = min(n, pref)
    assert n % t == 0
    return t


def _repack_w_in(w_in):
    sizes = (CONV_CH, GDN_V, GDN_HEADS, GDN_HEADS, ATT_Q, ATT_KV, ATT_KV, IDX_Q, IDX_DIM, IDX_HEADS,
             D_MODEL, D_MODEL)
    offs = np.concatenate([[0], np.cumsum(sizes)])
    (p_qkv, p_z, p_a, p_b, p_q, p_k, p_v, p_qi, p_ki, p_wi, p_ga, p_gb) = [
        w_in[:, offs[i]:offs[i + 1]] for i in range(len(sizes))]
    pad = jnp.zeros((w_in.shape[0], SMALL_W - (SM_WI + IDX_HEADS)), w_in.dtype)
    return jnp.concatenate([p_qkv, p_z, p_q, p_qi, p_ga, p_gb, p_k, p_v, p_ki, p_a, p_b, p_wi, pad], axis=1)


def _proj_kernel(x_ref, w_ref, o_ref):
    o_ref[...] = _dot(x_ref[...].astype(BF16), w_ref[...])


def _project(x2d, w_bf):
    t, d = x2d.shape
    n = w_bf.shape[1]
    tm, tn = _pick_tile(t, PROJ_TM), _pick_tile(n, PROJ_TN)
    return pl.pallas_call(
        _proj_kernel,
        grid=(t // tm, n // tn),
        in_specs=[pl.BlockSpec((tm, d), lambda i, j: (i, 0)),
                  pl.BlockSpec((d, tn), lambda i, j: (0, j))],
        out_specs=pl.BlockSpec((tm, tn), lambda i, j: (i, j)),
        out_shape=jax.ShapeDtypeStruct((t, n), F32),
        compiler_params=_cparams(("parallel", "arbitrary")),
        name="in_proj",
    )(x2d, w_bf)


TAIL_ROWS = SUBLANES
INV_BASE = 8


def _split(a):
    hi = a.astype(BF16)
    return hi, (a - hi.astype(F32)).astype(BF16)


def _dot3(a, b):
    return _dot(a[0], b[0]) + (_dot(a[0], b[1]) + _dot(a[1], b[0]))


def _unit_lower_inverse_steps(ms, c, out):
    ri = lax.broadcasted_iota(jnp.int32, (c, c), 0)
    ci = lax.broadcasted_iota(jnp.int32, (c, c), 1)
    eye = (ri == ci).astype(F32)
    blk = INV_BASE
    same = (ri // blk) == (ci // blk)
    ns = [jnp.where(same, -m, 0.0) for m in ms]
    xs = [eye + n for n in ns]
    span = 1
    while span * 2 < blk:
        nsp = [_split(n) for n in ns]
        ns = [_dot3(n, n) for n in nsp]
        yield
        nsp = [_split(n) for n in ns]
        xs = [x + _dot3(_split(x), n) for x, n in zip(xs, nsp)]
        yield
        span *= 2
    while blk < c:
        nxt = blk * 2
        emask = ((ri // nxt) == (ci // nxt)) & ((ri // blk) != (ci // blk))
        xsp = [_split(x) for x in xs]
        ts = [_dot3(x, _split(jnp.where(emask, m, 0.0))) for x, m in zip(xsp, ms)]
        yield
        xs = [x - _dot3(_split(t), xp) for x, t, xp in zip(xs, ts, xsp)]
        yield
        blk = nxt
    out.extend(xs)


def _gdn_kernel(hq_ref, z_ref, sm_ref, buf_ref, s0_ref, cw_ref, alog_ref, dtb_ref, ng_ref,
                y_ref, snew_ref, bufnew_ref, xp_ref, s_ref, *, chunk):
    c_idx = pl.program_id(1)
    n_c = pl.num_programs(1)
    C = chunk
    keep = CONV_W - 1

    @pl.when(c_idx == 0)
    def _():
        xp_ref[:, TAIL_ROWS - keep:TAIL_ROWS, :] = buf_ref[...]
        s_ref[...] = s0_ref[...]

    ri = lax.broadcasted_iota(jnp.int32, (C, C), 0)
    ci = lax.broadcasted_iota(jnp.int32, (C, C), 1)
    incl = ri >= ci
    strict = ri > ci

    heads = range(GDN_HEADS)

    def prepare(bi, d):
        xp_ref[bi, TAIL_ROWS:TAIL_ROWS + C, :] = hq_ref[bi]
        acc = cw_ref[0:1, :] * xp_ref[bi, TAIL_ROWS - keep:TAIL_ROWS - keep + C, :]
        for j in range(1, CONV_W):
            acc = acc + cw_ref[j:j + 1, :] * xp_ref[bi, TAIL_ROWS - keep + j:TAIL_ROWS - keep + j + C, :]
        conv = _silu(acc)
        xp_ref[bi, TAIL_ROWS - keep:TAIL_ROWS, :] = xp_ref[bi, TAIL_ROWS + C - keep:TAIL_ROWS + C, :]
        yield
        sm = sm_ref[bi, :, 0:LANES]
        xg = sm + dtb_ref[...]
        softplus = jnp.maximum(xg, 0.0) + jnp.log(1.0 + jnp.exp(-jnp.abs(xg)))
        g = -jnp.exp(alog_ref[...]) * softplus
        beta = _sigmoid(sm)
        gc = _dot_hi(incl.astype(F32), g)
        gc_t = gc.T
        yield
        for name in ("q", "k", "v", "gc", "bh", "eg", "decay"):
            d[name] = []
        for h in heads:
            q = conv[:, h * GDN_DK:(h + 1) * GDN_DK]
            k = conv[:, GDN_QK + h * GDN_DK:GDN_QK + (h + 1) * GDN_DK]
            d["q"].append(q * lax.rsqrt(jnp.sum(q * q, -1, keepdims=True) + 1e-6) * (GDN_DK ** -0.5))
            d["k"].append(k * lax.rsqrt(jnp.sum(k * k, -1, keepdims=True) + 1e-6))
            d["v"].append(conv[:, 2 * GDN_QK + h * GDN_DV:2 * GDN_QK + (h + 1) * GDN_DV])
            gch = gc[:, SM_A + h:SM_A + h + 1]
            d["gc"].append(gch)
            d["bh"].append(beta[:, SM_B + h:SM_B + h + 1])
            d["eg"].append(jnp.exp(gch))
            d["decay"].append(jnp.exp(jnp.where(incl, gch - gc_t[SM_A + h:SM_A + h + 1, :], -jnp.inf)))
            yield

    def recur(bi, d):
        qs, ks, vs, gcs, bhs, egs, decays = (d[n] for n in ("q", "k", "v", "gc", "bh", "eg", "decay"))
        kbs = [k * bh for k, bh in zip(ks, bhs)]
        ms = [jnp.where(strict, _dot_nt_bf(kb, k) * dc, 0.0) for kb, k, dc in zip(kbs, ks, decays)]
        yield
        attns = [_dot_nt_bf(q, k) * dc for q, k, dc in zip(qs, ks, decays)]
        yield
        tinvs = []
        yield from _unit_lower_inverse_steps(ms, C, tinvs)
        sols = [_dot_bf(t, jnp.concatenate([v * bh, kb * eg], axis=-1))
                for t, v, bh, kb, eg in zip(tinvs, vs, bhs, kbs, egs)]
        yield
        ss = [s_ref[bi, h] for h in heads]
        v_news = [sol[:, :GDN_DV] - _dot_bf(sol[:, GDN_DV:], s) for sol, s in zip(sols, ss)]
        yield
        os_ = [_dot_bf(q * eg, s) + _dot_bf(a, vn) for q, eg, s, a, vn in zip(qs, egs, ss, attns, v_news)]
        yield
        for h in heads:
            glast = gcs[h][C - 1:C, :]
            s_ref[bi, h] = ss[h] * jnp.exp(glast) + _dot_tn_bf(ks[h] * jnp.exp(glast - gcs[h]), v_news[h])
        yield
        for h in heads:
            o = os_[h]
            o = o * lax.rsqrt(jnp.mean(o * o, -1, keepdims=True) + RMS_EPS) * ng_ref[...]
            zh = z_ref[bi, :, h * GDN_DV:(h + 1) * GDN_DV]
            y_ref[bi, :, h * GDN_DV:(h + 1) * GDN_DV] = o * _silu(zh)

    n_seq = hq_ref.shape[0]
    data = [{} for _ in range(n_seq)]
    for _ in prepare(0, data[0]):
        pass
    for bi in range(n_seq):
        filler = prepare(bi + 1, data[bi + 1]) if bi + 1 < n_seq else iter(())
        for _ in recur(bi, data[bi]):
            next(filler, None)
        for _ in filler:
            pass

    @pl.when(c_idx == n_c - 1)
    def _():
        snew_ref[...] = s_ref[...]
        bufnew_ref[...] = xp_ref[:, TAIL_ROWS - keep:TAIL_ROWS, :]


def _gdn(h3, l, conv_buf, s0, conv_w, a_log, dt_bias, norm_g):
    b = h3.shape[0]
    chunk = min(CHUNK, l)
    assert l % chunk == 0 and chunk % SUBLANES == 0 and chunk >= CONV_W - 1
    kern = functools.partial(_gdn_kernel, chunk=chunk)
    keep = CONV_W - 1
    lane_row = lambda vec, at: jnp.zeros((1, LANES), F32).at[0, at:at + vec.shape[0]].set(vec)
    gb = _pick_tile(b, GDN_BATCH_TILE)
    return pl.pallas_call(
        kern,
        grid=(b // gb, l // chunk),
        in_specs=[
            pl.BlockSpec((gb, chunk, CONV_CH), lambda i, c: (i, c, COL_QKV // CONV_CH)),
            pl.BlockSpec((gb, chunk, GDN_V), lambda i, c: (i, c, COL_Z // GDN_V)),
            pl.BlockSpec((gb, chunk, SMALL_W), lambda i, c: (i, c, COL_SMALL // SMALL_W)),
            pl.BlockSpec((gb, keep, CONV_CH), lambda i, c: (i, 0, 0)),
            pl.BlockSpec((gb, GDN_HEADS, GDN_DK, GDN_DV), lambda i, c: (i, 0, 0, 0)),
            pl.BlockSpec((CONV_W, CONV_CH), lambda i, c: (0, 0)),
            pl.BlockSpec((1, LANES), lambda i, c: (0, 0)),
            pl.BlockSpec((1, LANES), lambda i, c: (0, 0)),
            pl.BlockSpec((1, GDN_DV), lambda i, c: (0, 0)),
        ],
        out_specs=[
            pl.BlockSpec((gb, chunk, GDN_V), lambda i, c: (i, c, 0)),
            pl.BlockSpec((gb, GDN_HEADS, GDN_DK, GDN_DV), lambda i, c: (i, 0, 0, 0)),
            pl.BlockSpec((gb, keep, CONV_CH), lambda i, c: (i, 0, 0)),
        ],
        out_shape=[
            jax.ShapeDtypeStruct((b, l, GDN_V), F32),
            jax.ShapeDtypeStruct((b, GDN_HEADS, GDN_DK, GDN_DV), F32),
            jax.ShapeDtypeStruct((b, keep, CONV_CH), F32),
        ],
        scratch_shapes=[
            pltpu.VMEM((gb, TAIL_ROWS + chunk, CONV_CH), F32),
            pltpu.VMEM((gb, GDN_HEADS, GDN_DK, GDN_DV), F32),
        ],
        compiler_params=_cparams(("parallel", "arbitrary")),
        name="gdn",
    )(h3, h3, h3, conv_buf, s0, conv_w, lane_row(a_log, SM_A), lane_row(dt_bias, SM_A),
      norm_g.reshape(1, GDN_DV))


def _rope_tables(pos, rot, period):
    half = rot // 2
    inv_freq = ROPE_THETA ** (-(2.0 / rot) * jnp.arange(half, dtype=F32))
    ang = pos.astype(F32)[:, None] * inv_freq[None, :]
    cos, sin = jnp.cos(ang), jnp.sin(ang)
    n = pos.shape[0]
    rest = period - rot
    c = jnp.concatenate([cos, cos, jnp.ones((n, rest), F32)], -1)
    sa = jnp.concatenate([-sin, jnp.zeros((n, half + rest), F32)], -1)
    sb = jnp.concatenate([jnp.zeros((n, half), F32), sin, jnp.zeros((n, rest), F32)], -1)
    reps = LANES // period
    return jnp.stack([jnp.tile(c, (1, reps)), jnp.tile(sa, (1, reps)), jnp.tile(sb, (1, reps))], 0)


Q_SCALE_LOG2 = HEAD_DIM ** -0.5 * float(np.log2(np.e))


def _rope128(x, tab_ref, half):
    return (x * tab_ref[0] + pltpu.roll(x, LANES - half, 1) * tab_ref[1]
            + pltpu.roll(x, half, 1) * tab_ref[2])


def _prep_kernel(q_ref, qi_ref, k_ref, v_ref, sm_ref, tq_ref, ti_ref, lng_ref, lnb_ref,
                 qt_ref, qit_ref, wit_ref, ko_ref, kb_ref, kio_ref, kib_ref, vo_ref, vt_ref):
    hq = HEAD_DIM // 8
    hi = IDX_DIM // 8
    for h in range(ATT_HEADS):
        x = _rope128(q_ref[:, h * HEAD_DIM:(h + 1) * HEAD_DIM], tq_ref, hq) * Q_SCALE_LOG2
        qt_ref[h * HEAD_DIM:(h + 1) * HEAD_DIM, :] = x.T.astype(BF16)
    for h in range(ATT_KV_HEADS):
        x = _rope128(k_ref[:, h * HEAD_DIM:(h + 1) * HEAD_DIM], tq_ref, hq)
        ko_ref[:, h * HEAD_DIM:(h + 1) * HEAD_DIM] = x
        kb_ref[:, h * HEAD_DIM:(h + 1) * HEAD_DIM] = x.astype(BF16)
        v = v_ref[:, h * HEAD_DIM:(h + 1) * HEAD_DIM]
        vo_ref[:, h * HEAD_DIM:(h + 1) * HEAD_DIM] = v
        vt_ref[h * HEAD_DIM:(h + 1) * HEAD_DIM, :] = v.T.astype(BF16)
    for c in range(IDX_Q // LANES):
        x = _rope128(qi_ref[:, c * LANES:(c + 1) * LANES], ti_ref, hi)
        qit_ref[c * LANES:(c + 1) * LANES, :] = x.T.astype(BF16)
    sm = sm_ref[:, 0:LANES]
    lane = lax.broadcasted_iota(jnp.int32, sm.shape, 1)
    is_ki = lane < IDX_DIM
    mu = jnp.sum(jnp.where(is_ki, sm, 0.0), -1, keepdims=True) * (1.0 / IDX_DIM)
    xc = jnp.where(is_ki, sm - mu, 0.0)
    var = jnp.sum(xc * xc, -1, keepdims=True) * (1.0 / IDX_DIM)
    ki = xc * lax.rsqrt(var + LN_EPS) * lng_ref[...] + lnb_ref[...]
    ki = _rope128(ki, ti_ref, hi)[:, 0:IDX_DIM]
    kio_ref[...] = ki
    kib_ref[...] = ki.astype(BF16)
    wit_ref[...] = sm.T[SM_WI:SM_WI + IDX_HEADS, :] * (IDX_HEADS ** -0.5 * IDX_DIM ** -0.5)


def _prep(h3, pos, ln_g, ln_b):
    b, l, _ = h3.shape
    tm = _pick_tile(l, PREP_TM)
    tab_q = _rope_tables(pos, HEAD_DIM // 4, HEAD_DIM)
    tab_i = _rope_tables(pos, IDX_DIM // 4, IDX_DIM)
    lng = jnp.concatenate([ln_g, jnp.zeros((LANES - IDX_DIM,), F32)]).reshape(1, LANES)
    lnb = jnp.concatenate([ln_b, jnp.zeros((LANES - IDX_DIM,), F32)]).reshape(1, LANES)
    rows = lambda w, col: pl.BlockSpec((None, tm, w), lambda i, t: (i, t, col))
    cols = lambda w: pl.BlockSpec((None, w, tm), lambda i, t: (i, 0, t))
    return pl.pallas_call(
        _prep_kernel,
        grid=(b, l // tm),
        in_specs=[
            rows(ATT_Q, COL_Q // ATT_Q), rows(IDX_Q, COL_QI // IDX_Q), rows(ATT_KV, COL_K // ATT_KV),
            rows(ATT_KV, COL_V // ATT_KV), rows(SMALL_W, COL_SMALL // SMALL_W),
            pl.BlockSpec((3, tm, LANES), lambda i, t: (0, t, 0)),
            pl.BlockSpec((3, tm, LANES), lambda i, t: (0, t, 0)),
            pl.BlockSpec((1, LANES), lambda i, t: (0, 0)),
            pl.BlockSpec((1, LANES), lambda i, t: (0, 0)),
        ],
        out_specs=[cols(ATT_Q), cols(IDX_Q), cols(IDX_HEADS), rows(ATT_KV, 0), rows(ATT_KV, 0),
                   rows(IDX_DIM, 0), rows(IDX_DIM, 0), rows(ATT_KV, 0), cols(ATT_KV)],
        out_shape=[
            jax.ShapeDtypeStruct((b, ATT_Q, l), BF16),
            jax.ShapeDtypeStruct((b, IDX_Q, l), BF16),
            jax.ShapeDtypeStruct((b, IDX_HEADS, l), F32),
            jax.ShapeDtypeStruct((b, l, ATT_KV), F32),
            jax.ShapeDtypeStruct((b, l, ATT_KV), BF16),
            jax.ShapeDtypeStruct((b, l, IDX_DIM), F32),
            jax.ShapeDtypeStruct((b, l, IDX_DIM), BF16),
            jax.ShapeDtypeStruct((b, l, ATT_KV), F32),
            jax.ShapeDtypeStruct((b, ATT_KV, l), BF16),
        ],
        compiler_params=_cparams(("parallel", "parallel")),
        name="dsa_prep",
    )(h3, h3, h3, h3, h3, tab_q, tab_i, lng, lnb)


INT_MIN = -2 ** 31
INT_MAX = 2 ** 31 - 1
NEG_INF_KEY = INT_MIN + 0x7FFFFF
NEG_BIG = -1e30
HEAD_PAIRS = IDX_HEADS // 2
COUNT_CHAINS = 8
DENOM_ROWS = 16


def _dsa_kernel(qt_ref, qit_ref, wit_ref, ki_ref, k_ref, vt_ref, o_ref,
                key_ref, qs_ref, qip_ref, m_ref, acc_ref, cm_ref,
                *, tq, kb, l_true, pos0, topk, idx_bits):
    q0 = pos0 + pl.program_id(1) * tq
    qpos = q0 + lax.broadcasted_iota(jnp.int32, (1, tq), 1)
    lim = jnp.minimum((qpos // CHUNK + 1) * CHUNK, l_true)
    lim_max = jnp.minimum(((q0 + tq - 1) // CHUNK + 1) * CHUNK, l_true)
    nkb = (lim_max + kb - 1) // kb
    key_iota = lax.broadcasted_iota(jnp.int32, (kb, tq), 0)
    wit = wit_ref[...]

    for p in range(HEAD_PAIRS):
        for u in range(2):
            hh = 2 * p + u
            qip_ref[p, :, u * tq:(u + 1) * tq] = qit_ref[hh * IDX_DIM:(hh + 1) * IDX_DIM, :]
    for g in range(ATT_KV_HEADS):
        for r in range(REP):
            hh = g * REP + r
            qs_ref[g, :, r * tq:(r + 1) * tq] = qt_ref[hh * HEAD_DIM:(hh + 1) * HEAD_DIM, :]

    def score_blk(j, carry):
        off = pl.multiple_of(j * kb, kb)
        ki = ki_ref[pl.ds(off, kb), :]
        acc = jnp.zeros((kb, tq), F32)
        for p in range(HEAD_PAIRS):
            s2 = _dot(ki, qip_ref[p])
            acc = (acc + wit[2 * p:2 * p + 1, :] * jnp.maximum(s2[:, :tq], 0.0)
                   + wit[2 * p + 1:2 * p + 2, :] * jnp.maximum(s2[:, tq:], 0.0))
        score = jnp.where(off + key_iota < lim, acc, -jnp.inf)
        bits = pltpu.bitcast(score, jnp.int32)
        key_ref[pl.ds(off, kb), :] = bits ^ ((bits >> 31) & INT_MAX)
        return carry

    lax.fori_loop(0, nkb, score_blk, 0)

    def count(pred_fn):
        def blk(j, c):
            off = pl.multiple_of(j * kb, kb)
            part = jnp.where(pred_fn(key_ref[pl.ds(off, kb), :], off), 1.0, 0.0)
            return c + jnp.sum(part.reshape(kb // (COUNT_CHAINS * SUBLANES), COUNT_CHAINS * SUBLANES, tq), axis=0)
        c = lax.fori_loop(0, nkb, blk, jnp.zeros((COUNT_CHAINS * SUBLANES, tq), F32))
        return jnp.sum(c, axis=0, keepdims=True)

    def bit_step(t, cur):
        cand_u = cur | lax.shift_left(jnp.int32(1), 31 - t)
        cand_s = cand_u ^ INT_MIN
        cnt = count(lambda kk, off: kk >= cand_s)
        return jnp.where(cnt >= topk, cand_u, cur)

    thr = lax.fori_loop(0, 32, bit_step, jnp.zeros((1, tq), jnp.int32)) ^ INT_MIN
    n_ge = count(lambda kk, off: kk >= thr)

    cm_ref[...] = jnp.full((1, tq), INT_MAX, jnp.int32)

    @pl.when(jnp.max(jnp.where((n_ge > topk) & (thr > NEG_INF_KEY), 1, 0)) > 0)
    def _():
        need = topk - count(lambda kk, off: kk > thr)

        def idx_step(t, cm):
            cand = cm | lax.shift_left(jnp.int32(1), idx_bits - 1 - t)
            before = count(lambda kk, off: (kk == thr) & (off + key_iota < cand))
            return jnp.where(before < need, cand, cm)
        cm_ref[...] = lax.fori_loop(0, idx_bits, idx_step, jnp.zeros((1, tq), jnp.int32))

    cm = cm_ref[...]

    m_ref[...] = jnp.full(m_ref.shape, NEG_BIG, F32)
    acc_ref[...] = jnp.zeros(acc_ref.shape, F32)
    groups = range(ATT_KV_HEADS)
    ones_rows = jnp.ones((DENOM_ROWS, kb), BF16)

    def attn_blk(j):
        off = pl.multiple_of(j * kb, kb)
        kk = key_ref[pl.ds(off, kb), :]
        kpos = off + key_iota
        sel = ((kk > thr) | ((kk == thr) & (kpos <= cm))) & (kpos < lim)
        bias = jnp.where(sel, 0.0, NEG_BIG)
        bias = jnp.concatenate([bias] * REP, axis=1)
        logits = [_dot(k_ref[pl.ds(off, kb), g * HEAD_DIM:(g + 1) * HEAD_DIM], qs_ref[g]) + bias for g in groups]
        m_old = [m_ref[g] for g in groups]
        m_new = [jnp.maximum(m_old[g], jnp.max(logits[g], axis=0, keepdims=True)) for g in groups]
        ps = [jnp.exp2(logits[g] - m_new[g]).astype(BF16) for g in groups]
        alphas = [jnp.exp2(m_old[g] - m_new[g]) for g in groups]
        pvs = [_dot(jnp.concatenate([vt_ref[g * HEAD_DIM:(g + 1) * HEAD_DIM, pl.ds(off, kb)], ones_rows], axis=0),
                    ps[g]) for g in groups]
        for g in groups:
            acc_ref[g] = alphas[g] * acc_ref[g] + pvs[g]
            m_ref[g] = m_new[g]

    def attn_pair(jp, carry):
        attn_blk(2 * jp)
        attn_blk(2 * jp + 1)
        return carry

    lax.fori_loop(0, nkb // 2, attn_pair, 0)

    @pl.when(nkb % 2 == 1)
    def _():
        attn_blk(nkb - 1)

    for g in groups:
        o_t = acc_ref[g, 0:HEAD_DIM, :] / acc_ref[g, HEAD_DIM:HEAD_DIM + 1, :]
        for r in range(REP):
            hh = g * REP + r
            o_ref[:, hh * HEAD_DIM:(hh + 1) * HEAD_DIM] = o_t[:, r * tq:(r + 1) * tq].T


def _dsa(qt, qit, wit, ki, k, vt, l_true, pos0):
    b, _, t = qt.shape
    lk = k.shape[1]
    tq, kb = DSA_Q_TILE, DSA_KEY_BLOCK
    assert t % tq == 0 and lk % kb == 0 and tq == HEAD_DIM
    topk = min(TOPK_MAX, l_true // 4)
    kern = functools.partial(_dsa_kernel, tq=tq, kb=kb, l_true=l_true, pos0=pos0, topk=topk,
                             idx_bits=int(lk).bit_length())
    return pl.pallas_call(
        kern,
        grid=(b, t // tq),
        in_specs=[
            pl.BlockSpec((None, ATT_Q, tq), lambda i, t_: (i, 0, t_)),
            pl.BlockSpec((None, IDX_Q, tq), lambda i, t_: (i, 0, t_)),
            pl.BlockSpec((None, IDX_HEADS, tq), lambda i, t_: (i, 0, t_)),
            pl.BlockSpec((None, lk, IDX_DIM), lambda i, t_: (i, 0, 0)),
            pl.BlockSpec((None, lk, ATT_KV), lambda i, t_: (i, 0, 0)),
            pl.BlockSpec((None, ATT_KV, lk), lambda i, t_: (i, 0, 0)),
        ],
        out_specs=pl.BlockSpec((None, tq, ATT_Q), lambda i, t_: (i, t_, 0)),
        out_shape=jax.ShapeDtypeStruct((b, t, ATT_Q), F32),
        scratch_shapes=[
            pltpu.VMEM((lk, tq), jnp.int32),
            pltpu.VMEM((ATT_KV_HEADS, HEAD_DIM, REP * tq), BF16),
            pltpu.VMEM((HEAD_PAIRS, IDX_DIM, 2 * tq), BF16),
            pltpu.VMEM((ATT_KV_HEADS, 1, REP * tq), F32),
            pltpu.VMEM((ATT_KV_HEADS, HEAD_DIM + DENOM_ROWS, REP * tq), F32),
            pltpu.VMEM((1, tq), jnp.int32),
        ],
        compiler_params=_cparams(("parallel", "arbitrary")),
        name="dsa",
    )(qt, qit, wit, ki, k, vt)


def _layer_norm(x, g, b):
    mu = jnp.mean(x, -1, keepdims=True)
    xc = x - mu
    var = jnp.mean(xc * xc, -1, keepdims=True)
    return xc * lax.rsqrt(var + LN_EPS) * g + b


def _merge_kernel(ya_ref, yb_ref, ga_ref, gb_ref, x_ref, wa_ref, wb_ref, wo_ref, g_ref, b_ref, o_ref):
    pa = _dot(ya_ref[...].astype(BF16), wa_ref[...])
    pb = _dot(yb_ref[...].astype(BF16), wb_ref[...])
    merged = _sigmoid(ga_ref[...]) * pa + _sigmoid(gb_ref[...]) * pb
    y = DN_ALPHA * x_ref[...] + _dot(merged.astype(BF16), wo_ref[...])
    o_ref[...] = _layer_norm(y, g_ref[...], b_ref[...])


def _merge(ya, yb, h3, x, wa, wb, wo, g, b):
    bsz, l, _ = x.shape
    tm = _pick_tile(l, TOKEN_TM)
    row = lambda c: pl.BlockSpec((None, tm, D_MODEL), lambda i, t: (i, t, c))
    full = lambda shp: pl.BlockSpec(shp, lambda i, t: (0, 0))
    return pl.pallas_call(
        _merge_kernel,
        grid=(bsz, l // tm),
        in_specs=[row(0), row(0), row(COL_GA // D_MODEL), row(COL_GB // D_MODEL), row(0),
                  full((GDN_V, D_MODEL)), full((ATT_Q, D_MODEL)), full((D_MODEL, D_MODEL)),
                  full((1, D_MODEL)), full((1, D_MODEL))],
        out_specs=row(0),
        out_shape=jax.ShapeDtypeStruct((bsz, l, D_MODEL), F32),
        compiler_params=_cparams(("parallel", "parallel")),
        name="merge_ln1",
    )(ya, yb, h3, h3, x, wa, wb, wo, g.reshape(1, D_MODEL), b.reshape(1, D_MODEL))


GROUP_SIZE = N_EXPERTS // N_GROUPS

SLAB_ROWS = D_MODEL // LANES


def _load_token_slabs(ref, n_tok, *lead):
    return jnp.concatenate([ref[lead + (pl.ds(c, n_tok, stride=SLAB_ROWS), slice(None))]
                            for c in range(SLAB_ROWS)], axis=1)


def _store_token_slabs(ref, x):
    for c in range(SLAB_ROWS):
        ref[pl.ds(c, x.shape[0], stride=SLAB_ROWS), :] = x[:, c * LANES:(c + 1) * LANES]


def _first_max(cur, rows, n_rows):
    m = jnp.max(cur, axis=0, keepdims=True)
    idx = jnp.min(jnp.where(cur == m, rows, n_rows), axis=0, keepdims=True)
    return m, idx


def _router_kernel(x_ref, wr_ref, rb_ref, ws1_ref, ws3_ref, ws2_ref,
                   eidx_ref, wts_ref, rank_ref, cnt_ref, base_ref, xslab_ref, run_ref):
    @pl.when(pl.program_id(0) == 0)
    def _():
        run_ref[...] = jnp.zeros(run_ref.shape, F32)

    x = x_ref[...]
    tm = x.shape[0]
    logits = _dot_nt_hi(wr_ref[...], x)
    scores = _sigmoid(logits)
    biased = scores + rb_ref[...]
    neg = -jnp.inf

    rows_g = lax.broadcasted_iota(jnp.int32, (GROUP_SIZE, tm), 0)
    gs = []
    for g in range(N_GROUPS):
        blk = biased[g * GROUP_SIZE:(g + 1) * GROUP_SIZE, :]
        m1, i1 = _first_max(blk, rows_g, GROUP_SIZE)
        m2 = jnp.max(jnp.where(rows_g == i1, neg, blk), axis=0, keepdims=True)
        gs.append(m1 + m2)
    cur = jnp.concatenate(gs, axis=0)
    rows_n = lax.broadcasted_iota(jnp.int32, (N_GROUPS, tm), 0)
    gsel = jnp.zeros((N_GROUPS, tm), F32)
    for _ in range(TOPK_GROUPS):
        _, ig = _first_max(cur, rows_n, N_GROUPS)
        hit = rows_n == ig
        gsel = jnp.where(hit, 1.0, gsel)
        cur = jnp.where(hit, neg, cur)
    gexp = jnp.concatenate([jnp.broadcast_to(gsel[g:g + 1, :], (GROUP_SIZE, tm)) for g in range(N_GROUPS)], axis=0)
    cur = jnp.where(gexp > 0.0, biased, neg)

    rows_e = lax.broadcasted_iota(jnp.int32, (N_EXPERTS, tm), 0)
    es, ws, hits = [], [], []
    for _ in range(EXPERT_TOPK):
        _, ie = _first_max(cur, rows_e, N_EXPERTS)
        hit = rows_e == ie
        es.append(ie)
        hits.append(hit)
        ws.append(jnp.sum(jnp.where(hit, scores, 0.0), axis=0, keepdims=True))
        cur = jnp.where(hit, neg, cur)
    w = jnp.concatenate(ws, axis=0)
    eidx_ref[...] = jnp.concatenate(es, axis=0)
    wts_ref[...] = w / jnp.sum(w, axis=0, keepdims=True) * ROUTED_SCALE

    assigned = jnp.zeros((N_EXPERTS, tm), F32)
    for hit in hits:
        assigned = jnp.where(hit, 1.0, assigned)
    earlier = (lax.broadcasted_iota(jnp.int32, (tm, tm), 0) < lax.broadcasted_iota(jnp.int32, (tm, tm), 1))
    before = run_ref[...] + _dot(assigned.astype(BF16), earlier.astype(BF16))
    rank_ref[...] = jnp.concatenate(
        [jnp.sum(jnp.where(hit, before, 0.0), axis=0, keepdims=True) for hit in hits], axis=0).astype(jnp.int32)
    run_ref[...] = run_ref[...] + jnp.sum(assigned, axis=1, keepdims=True)
    cnt_ref[...] = run_ref[...].astype(jnp.int32)

    xb = x.astype(BF16)
    hs = _silu(_dot(xb, ws1_ref[...])) * _dot(xb, ws3_ref[...])
    base_ref[...] = DN_ALPHA * x + _dot(hs.astype(BF16), ws2_ref[...])
    _store_token_slabs(xslab_ref, x)


def _router(x1, wr_t, rbias, ws1, ws3, ws2, tm):
    t = x1.shape[0]
    full = lambda shp: pl.BlockSpec(shp, lambda i: (0, 0))
    return pl.pallas_call(
        _router_kernel,
        grid=(t // tm,),
        in_specs=[pl.BlockSpec((tm, D_MODEL), lambda i: (i, 0)),
                  full((N_EXPERTS, D_MODEL)), full((N_EXPERTS, 1)),
                  full((D_MODEL, SHARED_DIM)), full((D_MODEL, SHARED_DIM)), full((SHARED_DIM, D_MODEL))],
        out_specs=[pl.BlockSpec((EXPERT_TOPK, tm), lambda i: (0, i)),
                   pl.BlockSpec((EXPERT_TOPK, tm), lambda i: (0, i)),
                   pl.BlockSpec((EXPERT_TOPK, tm), lambda i: (0, i)),
                   full((N_EXPERTS, 1)),
                   pl.BlockSpec((tm, D_MODEL), lambda i: (i, 0)),
                   pl.BlockSpec((tm * SLAB_ROWS, LANES), lambda i: (i, 0))],
        out_shape=[jax.ShapeDtypeStruct((EXPERT_TOPK, t), jnp.int32),
                   jax.ShapeDtypeStruct((EXPERT_TOPK, t), F32),
                   jax.ShapeDtypeStruct((EXPERT_TOPK, t), jnp.int32),
                   jax.ShapeDtypeStruct((N_EXPERTS, 1), jnp.int32),
                   jax.ShapeDtypeStruct((t, D_MODEL), F32),
                   jax.ShapeDtypeStruct((t * SLAB_ROWS, LANES), F32)],
        scratch_shapes=[pltpu.VMEM((N_EXPERTS, 1), F32)],
        compiler_params=_cparams(("arbitrary",)),
        name="router_shared",
    )(x1, wr_t, rbias.reshape(N_EXPERTS, 1), ws1, ws3, ws2)


def _block_plan(counts, n_blocks):
    padded = (counts + MOE_ROWS - 1) // MOE_ROWS * MOE_ROWS
    pad_end = jnp.cumsum(padded)
    blk_first = jnp.arange(n_blocks, dtype=jnp.int32) * MOE_ROWS
    blk_exp = jnp.minimum(jnp.sum((pad_end[None, :] <= blk_first[:, None]).astype(jnp.int32), axis=1), N_EXPERTS - 1)
    return pad_end - padded, jnp.maximum(pad_end - MOE_ROWS, 0), blk_exp, pad_end[-1:] // MOE_ROWS


def _pos_kernel(e_ref, r_ref, ps_ref, pos_ref):
    tm = e_ref.shape[1]
    rows_e = lax.broadcasted_iota(jnp.int32, (N_EXPERTS, tm), 0)
    first = [jnp.sum(jnp.where(rows_e == e_ref[j:j + 1, :], ps_ref[...], 0), axis=0, keepdims=True)
             for j in range(EXPERT_TOPK)]
    pos_ref[...] = jnp.concatenate(first, axis=0) + r_ref[...]


def _positions(eidx_t, rank_t, pad_start, tm):
    t = eidx_t.shape[1]
    blk = pl.BlockSpec((EXPERT_TOPK, tm), lambda i: (0, i))
    return pl.pallas_call(
        _pos_kernel,
        grid=(t // tm,),
        in_specs=[blk, blk, pl.BlockSpec((N_EXPERTS, 1), lambda i: (0, 0))],
        out_specs=blk,
        out_shape=jax.ShapeDtypeStruct((EXPERT_TOPK, t), jnp.int32),
        compiler_params=_cparams(("parallel",)),
        name="moe_positions",
    )(eidx_t, rank_t, pad_start.reshape(N_EXPERTS, 1))


def _scatter_kernel(last_ref, nu_ref, pos_ref, x_ref, rows_ref, zero_ref, sem, zsem):
    tm = x_ref.shape[0] // SLAB_ROWS
    block_slabs = MOE_ROWS * SLAB_ROWS
    n_blocks = rows_ref.shape[0] // block_slabs

    def slab(ref, row):
        return ref.at[pl.ds(pl.multiple_of(row * SLAB_ROWS, SLAB_ROWS), SLAB_ROWS), :]

    @pl.when(pl.program_id(0) == 0)
    def _():
        zero_ref[...] = jnp.zeros(zero_ref.shape, F32)

        def zero_copy(first_row):
            first = pl.multiple_of(first_row * SLAB_ROWS, block_slabs)
            return pltpu.make_async_copy(zero_ref, rows_ref.at[pl.ds(first, block_slabs), :], zsem)

        def each(fn):
            def expert_block(e, carry):
                fn(zero_copy(last_ref[e]))
                return carry

            def unused_block(blk, carry):
                fn(zero_copy(blk * MOE_ROWS))
                return carry

            lax.fori_loop(0, N_EXPERTS, expert_block, 0)
            lax.fori_loop(nu_ref[0], n_blocks, unused_block, 0)

        each(lambda copy: copy.start())
        each(lambda copy: copy.wait())

    def row_copy(t, j):
        return pltpu.make_async_copy(slab(x_ref, t), slab(rows_ref, pos_ref[j, t]), sem)

    def issue(t, carry):
        for j in range(EXPERT_TOPK):
            row_copy(t, j).start(priority=j % DMA_PRIORITIES)
        return carry

    lax.fori_loop(0, tm, issue, 0)
    for j in range(EXPERT_TOPK):
        pltpu.make_async_copy(x_ref, rows_ref.at[pl.ds(0, tm * SLAB_ROWS), :], sem).wait()


def _scatter_rows(last_block_row, n_used, pos_t, x_slabs, n_rows, tm):
    t = x_slabs.shape[0] // SLAB_ROWS
    grid_spec = pltpu.PrefetchScalarGridSpec(
        num_scalar_prefetch=2,
        grid=(t // tm,),
        in_specs=[pl.BlockSpec((EXPERT_TOPK, tm), lambda i, lr, nu: (0, i), memory_space=pltpu.SMEM),
                  pl.BlockSpec((tm * SLAB_ROWS, LANES), lambda i, lr, nu: (i, 0))],
        out_specs=pl.BlockSpec(memory_space=pl.ANY),
        scratch_shapes=[pltpu.VMEM((MOE_ROWS * SLAB_ROWS, LANES), F32), pltpu.SemaphoreType.DMA(()),
                        pltpu.SemaphoreType.DMA(())],
    )
    return pl.pallas_call(
        _scatter_kernel,
        grid_spec=grid_spec,
        out_shape=jax.ShapeDtypeStruct((n_rows * SLAB_ROWS, LANES), F32),
        compiler_params=_cparams(("arbitrary",)),
        name="moe_scatter",
    )(last_block_row, n_used, pos_t, x_slabs)


def _expert_kernel(be_ref, nu_ref, x_ref, w1_ref, w3_ref, w2_ref, o_ref, w1b_ref, w3b_ref, w2b_ref):
    i = pl.program_id(0)
    prev = be_ref[jnp.maximum(i - 1, 0)]

    @pl.when((i == 0) | (be_ref[i] != prev))
    def _():
        w1b_ref[...] = w1_ref[...].astype(BF16)
        w3b_ref[...] = w3_ref[...].astype(BF16)
        w2b_ref[...] = w2_ref[...].astype(BF16)

    @pl.when(i < nu_ref[0])
    def _():
        x = _load_token_slabs(x_ref, MOE_ROWS).astype(BF16)
        hmid = _silu(_dot(x, w1b_ref[...])) * _dot(x, w3b_ref[...])
        _store_token_slabs(o_ref, _dot(hmid.astype(BF16), w2b_ref[...]))

    @pl.when(i >= nu_ref[0])
    def _():
        o_ref[...] = jnp.zeros(o_ref.shape, F32)


def _experts(x_slabs, blk_exp, n_used, w1, w3, w2):
    block_slabs = MOE_ROWS * SLAB_ROWS
    n_blocks = x_slabs.shape[0] // block_slabs
    grid_spec = pltpu.PrefetchScalarGridSpec(
        num_scalar_prefetch=2,
        grid=(n_blocks,),
        in_specs=[
            pl.BlockSpec((block_slabs, LANES), lambda i, be, nu: (jnp.minimum(i, nu[0] - 1), 0)),
            pl.BlockSpec((None, D_MODEL, EXPERT_DIM), lambda i, be, nu: (be[i], 0, 0)),
            pl.BlockSpec((None, D_MODEL, EXPERT_DIM), lambda i, be, nu: (be[i], 0, 0)),
            pl.BlockSpec((None, EXPERT_DIM, D_MODEL), lambda i, be, nu: (be[i], 0, 0)),
        ],
        out_specs=pl.BlockSpec((block_slabs, LANES), lambda i, be, nu: (i, 0)),
        scratch_shapes=[pltpu.VMEM((D_MODEL, EXPERT_DIM), BF16), pltpu.VMEM((D_MODEL, EXPERT_DIM), BF16),
                        pltpu.VMEM((EXPERT_DIM, D_MODEL), BF16)],
    )
    return pl.pallas_call(
        _expert_kernel,
        grid_spec=grid_spec,
        out_shape=jax.ShapeDtypeStruct(x_slabs.shape, F32),
        compiler_params=_cparams(("arbitrary",)),
        name="experts",
    )(blk_exp, n_used, x_slabs, w1, w3, w2)


def _combine_kernel(pos_ref, pos_next_ref, y_ref, w_ref, base_ref, g_ref, b_ref, o_ref, gbuf_ref, sem):
    tm = base_ref.shape[0]
    i = pl.program_id(0)
    slot = lax.rem(i, 2)

    def slab(row):
        return pl.ds(pl.multiple_of(row * SLAB_ROWS, SLAB_ROWS), SLAB_ROWS)

    def issue_tile(p_ref, s):
        def issue(t, carry):
            for j in range(EXPERT_TOPK):
                pltpu.make_async_copy(y_ref.at[slab(p_ref[j, t]), :], gbuf_ref.at[s, j, slab(t), :],
                                      sem.at[s]).start(priority=j % DMA_PRIORITIES)
            return carry
        lax.fori_loop(0, tm, issue, 0)

    @pl.when(i == 0)
    def _():
        issue_tile(pos_ref, 0)

    @pl.when(i + 1 < pl.num_programs(0))
    def _():
        issue_tile(pos_next_ref, 1 - slot)

    for j in range(EXPERT_TOPK):
        pltpu.make_async_copy(y_ref.at[pl.ds(0, tm * SLAB_ROWS), :], gbuf_ref.at[slot, j], sem.at[slot]).wait()

    acc = base_ref[...]
    w = w_ref[...]
    for j in range(EXPERT_TOPK):
        acc = acc + w[:, j:j + 1] * _load_token_slabs(gbuf_ref, tm, slot, j)
    o_ref[...] = _layer_norm(acc, g_ref[...], b_ref[...])


def _combine(pos_t, y_rows, wts, base, g, b, tm):
    t = base.shape[0]
    n_tiles = t // tm
    return pl.pallas_call(
        _combine_kernel,
        grid=(n_tiles,),
        in_specs=[pl.BlockSpec((EXPERT_TOPK, tm), lambda i: (0, i), memory_space=pltpu.SMEM),
                  pl.BlockSpec((EXPERT_TOPK, tm), lambda i: (0, jnp.minimum(i + 1, n_tiles - 1)),
                               memory_space=pltpu.SMEM),
                  pl.BlockSpec(memory_space=pl.ANY),
                  pl.BlockSpec((tm, EXPERT_TOPK), lambda i: (i, 0)),
                  pl.BlockSpec((tm, D_MODEL), lambda i: (i, 0)),
                  pl.BlockSpec((1, D_MODEL), lambda i: (0, 0)),
                  pl.BlockSpec((1, D_MODEL), lambda i: (0, 0))],
        out_specs=pl.BlockSpec((tm, D_MODEL), lambda i: (i, 0)),
        out_shape=jax.ShapeDtypeStruct((t, D_MODEL), F32),
        scratch_shapes=[pltpu.VMEM((2, EXPERT_TOPK, tm * SLAB_ROWS, LANES), F32), pltpu.SemaphoreType.DMA((2,))],
        compiler_params=_cparams(("arbitrary",)),
        name="combine_ln2",
    )(pos_t, pos_t, y_rows, wts, base, g.reshape(1, D_MODEL), b.reshape(1, D_MODEL))


def _moe(x1, prm):
    t = x1.shape[0]
    tm = _pick_tile(t, TOKEN_TM)
    eidx_t, wts_t, rank_t, counts, base, x_slabs = _router(x1, prm["w_router_t"], prm["router_bias"], prm["ws1"],
                                                           prm["ws3"], prm["ws2"], tm)
    n_blocks = -(-t * EXPERT_TOPK // MOE_ROWS) + N_EXPERTS
    pad_start, last_block_row, blk_exp, n_used = _block_plan(counts.reshape(N_EXPERTS), n_blocks)
    pos_t = _positions(eidx_t, rank_t, pad_start, tm)
    x_rows = _scatter_rows(last_block_row, n_used, pos_t, x_slabs, n_blocks * MOE_ROWS, tm)
    y_rows = _experts(x_rows, blk_exp, n_used, prm["w1"], prm["w3"], prm["w2"])
    return _combine(pos_t, y_rows, wts_t.T, base, prm["ln2_g"], prm["ln2_b"], _pick_tile(t, COMBINE_TM))


def _mixer(x, pos0, conv_buf, s0, k_past, v_past, ik_past, prm):
    b, l, _ = x.shape
    lp = _round_up(l, DSA_Q_TILE)
    xp = x if lp == l else jnp.pad(x, ((0, 0), (0, lp - l), (0, 0)))
    h3 = _project(xp.reshape(b * lp, D_MODEL), prm["w_in"]).reshape(b, lp, H_COLS)

    ya, s_new, buf_new = _gdn(h3, l, conv_buf, s0, prm["conv_w"], prm["a_log"], prm["dt_bias"], prm["gdn_norm_g"])

    pos = pos0 + jnp.arange(lp, dtype=jnp.int32)
    qt, qit, wit, k_new, k_bf, ki_new, ki_bf, v_new, vt = _prep(h3, pos, prm["idx_k_ln_g"], prm["idx_k_ln_b"])
    if k_past is None:
        l_all = l
        k_all, ki_all, vt_all = k_bf, ki_bf, vt
    else:
        past = k_past.shape[1]
        l_all = past + l
        k_all = jnp.concatenate([k_past.reshape(b, past, ATT_KV).astype(BF16), k_bf[:, :l]], 1)
        ki_all = jnp.concatenate([ik_past.astype(BF16), ki_bf[:, :l]], 1)
        vt_all = jnp.concatenate([jnp.swapaxes(v_past.reshape(b, past, ATT_KV), 1, 2).astype(BF16), vt[:, :, :l]], 2)
    lk = _round_up(l_all, DSA_KEY_BLOCK)
    if lk != l_all:
        k_all = jnp.pad(k_all, ((0, 0), (0, lk - l_all), (0, 0)))
        ki_all = jnp.pad(ki_all, ((0, 0), (0, lk - l_all), (0, 0)))
        vt_all = jnp.pad(vt_all, ((0, 0), (0, 0), (0, lk - l_all)))
    yb = _dsa(qt, qit, wit, ki_all, k_all, vt_all, l_all, pos0)

    x1 = _merge(ya, yb, h3, x, prm["w_o_gdn"], prm["w_o_dsa"], prm["w_out"], prm["ln1_g"], prm["ln1_b"])
    state = (k_new[:, :l].reshape(b, l, ATT_KV_HEADS, HEAD_DIM), v_new[:, :l].reshape(b, l, ATT_KV_HEADS, HEAD_DIM),
             ki_new[:, :l], s_new, buf_new)
    return x1, state


def kernel(x_prompt, x_sample, cache_k, cache_v, cache_idx_k, state_gdn, state_conv, w_in, conv_w, a_log, dt_bias, gdn_norm_g, w_o_gdn, idx_k_ln_g, idx_k_ln_b, w_o_dsa, w_out, ln1_g, ln1_b, w_router, router_bias, w1, w3, w2, ws1, ws3, ws2, ln2_g, ln2_b):
    assert w_in.shape[0] == DEPTH == 1
    bp, lp_, _ = x_prompt.shape
    bs, ls_, _ = x_sample.shape
    past = cache_k.shape[2]
    prm = dict(
        w_in=_repack_w_in(w_in[0]).astype(BF16), conv_w=conv_w[0], a_log=a_log[0], dt_bias=dt_bias[0],
        gdn_norm_g=gdn_norm_g[0], w_o_gdn=w_o_gdn[0].astype(BF16), idx_k_ln_g=idx_k_ln_g[0],
        idx_k_ln_b=idx_k_ln_b[0], w_o_dsa=w_o_dsa[0].astype(BF16), w_out=w_out[0].astype(BF16),
        ln1_g=ln1_g[0], ln1_b=ln1_b[0], w_router_t=w_router[0].T, router_bias=router_bias[0],
        w1=w1[0], w3=w3[0], w2=w2[0], ws1=ws1[0].astype(BF16), ws3=ws3[0].astype(BF16),
        ws2=ws2[0].astype(BF16), ln2_g=ln2_g[0], ln2_b=ln2_b[0])
    conv0 = jnp.zeros((bp, CONV_W - 1, CONV_CH), F32)
    s0 = jnp.zeros((bp, GDN_HEADS, GDN_DK, GDN_DV), F32)
    x1p, sp = _mixer(x_prompt, 0, conv0, s0, None, None, None, prm)
    x1s, ss = _mixer(x_sample, past, state_conv[0], state_gdn[0], cache_k[0], cache_v[0], cache_idx_k[0], prm)
    tp, ts = bp * lp_, bs * ls_
    y = _moe(jnp.concatenate([x1p.reshape(tp, D_MODEL), x1s.reshape(ts, D_MODEL)], 0), prm)
    yp = y[:tp].reshape(bp, lp_, D_MODEL)
    ys = y[tp:].reshape(bs, ls_, D_MODEL)
    return (yp, ys) + tuple(a[None] for a in sp) + tuple(a[None] for a in ss)
```

```python
import functools

import jax
import jax.numpy as jnp
import numpy as np
from jax import lax
from jax.experimental import pallas as pl
from jax.experimental.pallas import tpu as pltpu

F32 = jnp.float32
BF16 = jnp.bfloat16

D_MODEL = 1024
CHUNK = 64
GDN_HEADS = 8
GDN_DK = 128
GDN_DV = 128
CONV_W = 4
ATT_HEADS = 8
ATT_KV_HEADS = 2
HEAD_DIM = 128
IDX_HEADS = 16
IDX_DIM = 64
TOPK_MAX = 256
ROPE_THETA = 500000.0
N_EXPERTS = 256
EXPERT_TOPK = 8
N_GROUPS = 8
TOPK_GROUPS = 4
EXPERT_DIM = 256
SHARED_DIM = 256
ROUTED_SCALE = 2.5
DEPTH = 1
DN_ALPHA = (2.0 * DEPTH) ** 0.25
LN_EPS = 1e-5
RMS_EPS = 1e-6

GDN_QK = GDN_HEADS * GDN_DK
GDN_V = GDN_HEADS * GDN_DV
CONV_CH = 2 * GDN_QK + GDN_V
ATT_Q = ATT_HEADS * HEAD_DIM
ATT_KV = ATT_KV_HEADS * HEAD_DIM
IDX_Q = IDX_HEADS * IDX_DIM
REP = ATT_HEADS // ATT_KV_HEADS

LANES = 128
SUBLANES = 8
VMEM_LIMIT = 56 * 1024 * 1024
DMA_PRIORITIES = 2

PROJ_TM = 1024
PROJ_TN = 1024
PREP_TM = 512
DSA_Q_TILE = 128
DSA_KEY_BLOCK = 512
TOKEN_TM = 256
COMBINE_TM = 128
GDN_BATCH_TILE = 4
MOE_ROWS = 256

COL_QKV = 0
COL_Z = COL_QKV + CONV_CH
COL_Q = COL_Z + GDN_V
COL_QI = COL_Q + ATT_Q
COL_GA = COL_QI + IDX_Q
COL_GB = COL_GA + D_MODEL
COL_K = COL_GB + D_MODEL
COL_V = COL_K + ATT_KV
COL_SMALL = COL_V + ATT_KV
SMALL_W = 512
SM_A = IDX_DIM
SM_B = SM_A + GDN_HEADS
SM_WI = SM_B + GDN_HEADS
H_COLS = COL_SMALL + SMALL_W


def _cparams(sem):
    return pltpu.CompilerParams(dimension_semantics=sem, vmem_limit_bytes=VMEM_LIMIT)


def _dot(a, b):
    return jnp.dot(a, b, preferred_element_type=F32)


def _dot_bf(a, b):
    return jnp.dot(a.astype(BF16), b.astype(BF16), preferred_element_type=F32)


def _dot_hi(a, b):
    return jnp.dot(a, b, precision=lax.Precision.HIGHEST, preferred_element_type=F32)


def _dot_nt_hi(a, b):
    return lax.dot_general(a, b, (((1,), (1,)), ((), ())), precision=lax.Precision.HIGHEST,
                           preferred_element_type=F32)


def _dot_nt_bf(a, b):
    return lax.dot_general(a.astype(BF16), b.astype(BF16), (((1,), (1,)), ((), ())),
                           preferred_element_type=F32)


def _dot_tn_bf(a, b):
    return lax.dot_general(a.astype(BF16), b.astype(BF16), (((0,), (0,)), ((), ())),
                           preferred_element_type=F32)


def _sigmoid(x):
    return 1.0 / (1.0 + jnp.exp(-x))


def _silu(x):
    return x * _sigmoid(x)


def _round_up(n, m):
    return -(-n // m) * m


def _pick_tile(n, pref):
    t = min(n, pref)
    assert n % t == 0
    return t


def _repack_w_in(w_in):
    sizes = (CONV_CH, GDN_V, GDN_HEADS, GDN_HEADS, ATT_Q, ATT_KV, ATT_KV, IDX_Q, IDX_DIM, IDX_HEADS,
             D_MODEL, D_MODEL)
    offs = np.concatenate([[0], np.cumsum(sizes)])
    (p_qkv, p_z, p_a, p_b, p_q, p_k, p_v, p_qi, p_ki, p_wi, p_ga, p_gb) = [
        w_in[:, offs[i]:offs[i + 1]] for i in range(len(sizes))]
    pad = jnp.zeros((w_in.shape[0], SMALL_W - (SM_WI + IDX_HEADS)), w_in.dtype)
    return jnp.concatenate([p_qkv, p_z, p_q, p_qi, p_ga, p_gb, p_k, p_v, p_ki, p_a, p_b, p_wi, pad], axis=1)


def _proj_kernel(x_ref, w_ref, o_ref):
    o_ref[...] = _dot(x_ref[...].astype(BF16), w_ref[...])


def _project(x2d, w_bf):
    t, d = x2d.shape
    n = w_bf.shape[1]
    tm, tn = _pick_tile(t, PROJ_TM), _pick_tile(n, PROJ_TN)
    return pl.pallas_call(
        _proj_kernel,
        grid=(t // tm, n // tn),
        in_specs=[pl.BlockSpec((tm, d), lambda i, j: (i, 0)),
                  pl.BlockSpec((d, tn), lambda i, j: (0, j))],
        out_specs=pl.BlockSpec((tm, tn), lambda i, j: (i, j)),
        out_shape=jax.ShapeDtypeStruct((t, n), F32),
        compiler_params=_cparams(("parallel", "arbitrary")),
        name="in_proj",
    )(x2d, w_bf)


TAIL_ROWS = SUBLANES
INV_BASE = 8


def _split(a):
    hi = a.astype(BF16)
    return hi, (a - hi.astype(F32)).astype(BF16)


def _dot3(a, b):
    return _dot(a[0], b[0]) + (_dot(a[0], b[1]) + _dot(a[1], b[0]))


def _unit_lower_inverse_steps(ms, c, out):
    ri = lax.broadcasted_iota(jnp.int32, (c, c), 0)
    ci = lax.broadcasted_iota(jnp.int32, (c, c), 1)
    eye = (ri == ci).astype(F32)
    blk = INV_BASE
    same = (ri // blk) == (ci // blk)
    ns = [jnp.where(same, -m, 0.0) for m in ms]
    xs = [eye + n for n in ns]
    span = 1
    while span * 2 < blk:
        nsp = [_split(n) for n in ns]
        ns = [_dot3(n, n) for n in nsp]
        yield
        nsp = [_split(n) for n in ns]
        xs = [x + _dot3(_split(x), n) for x, n in zip(xs, nsp)]
        yield
        span *= 2
    while blk < c:
        nxt = blk * 2
        emask = ((ri // nxt) == (ci // nxt)) & ((ri // blk) != (ci // blk))
        xsp = [_split(x) for x in xs]
        ts = [_dot3(x, _split(jnp.where(emask, m, 0.0))) for x, m in zip(xsp, ms)]
        yield
        xs = [x - _dot3(_split(t), xp) for x, t, xp in zip(xs, ts, xsp)]
        yield
        blk = nxt
    out.extend(xs)


def _gdn_kernel(hq_ref, z_ref, sm_ref, buf_ref, s0_ref, cw_ref, alog_ref, dtb_ref, ng_ref,
                y_ref, snew_ref, bufnew_ref, xp_ref, s_ref, *, chunk):
    c_idx = pl.program_id(1)
    n_c = pl.num_programs(1)
    C = chunk
    keep = CONV_W - 1

    @pl.when(c_idx == 0)
    def _():
        xp_ref[:, TAIL_ROWS - keep:TAIL_ROWS, :] = buf_ref[...]
        s_ref[...] = s0_ref[...]

    ri = lax.broadcasted_iota(jnp.int32, (C, C), 0)
    ci = lax.broadcasted_iota(jnp.int32, (C, C), 1)
    incl = ri >= ci
    strict = ri > ci

    heads = range(GDN_HEADS)

    def prepare(bi, d):
        xp_ref[bi, TAIL_ROWS:TAIL_ROWS + C, :] = hq_ref[bi]
        acc = cw_ref[0:1, :] * xp_ref[bi, TAIL_ROWS - keep:TAIL_ROWS - keep + C, :]
        for j in range(1, CONV_W):
            acc = acc + cw_ref[j:j + 1, :] * xp_ref[bi, TAIL_ROWS - keep + j:TAIL_ROWS - keep + j + C, :]
        conv = _silu(acc)
        xp_ref[bi, TAIL_ROWS - keep:TAIL_ROWS, :] = xp_ref[bi, TAIL_ROWS + C - keep:TAIL_ROWS + C, :]
        yield
        sm = sm_ref[bi, :, 0:LANES]
        xg = sm + dtb_ref[...]
        softplus = jnp.maximum(xg, 0.0) + jnp.log(1.0 + jnp.exp(-jnp.abs(xg)))
        g = -jnp.exp(alog_ref[...]) * softplus
        beta = _sigmoid(sm)
        gc = _dot_hi(incl.astype(F32), g)
        gc_t = gc.T
        yield
        for name in ("q", "k", "v", "gc", "bh", "eg", "decay"):
            d[name] = []
        for h in heads:
            q = conv[:, h * GDN_DK:(h + 1) * GDN_DK]
            k = conv[:, GDN_QK + h * GDN_DK:GDN_QK + (h + 1) * GDN_DK]
            d["q"].append(q * lax.rsqrt(jnp.sum(q * q, -1, keepdims=True) + 1e-6) * (GDN_DK ** -0.5))
            d["k"].append(k * lax.rsqrt(jnp.sum(k * k, -1, keepdims=True) + 1e-6))
            d["v"].append(conv[:, 2 * GDN_QK + h * GDN_DV:2 * GDN_QK + (h + 1) * GDN_DV])
            gch = gc[:, SM_A + h:SM_A + h + 1]
            d["gc"].append(gch)
            d["bh"].append(beta[:, SM_B + h:SM_B + h + 1])
            d["eg"].append(jnp.exp(gch))
            d["decay"].append(jnp.exp(jnp.where(incl, gch - gc_t[SM_A + h:SM_A + h + 1, :], -jnp.inf)))
            yield

    def recur(bi, d):
        qs, ks, vs, gcs, bhs, egs, decays = (d[n] for n in ("q", "k", "v", "gc", "bh", "eg", "decay"))
        kbs = [k * bh for k, bh in zip(ks, bhs)]
        ms = [jnp.where(strict, _dot_nt_bf(kb, k) * dc, 0.0) for kb, k, dc in zip(kbs, ks, decays)]
        yield
        attns = [_dot_nt_bf(q, k) * dc for q, k, dc in zip(qs, ks, decays)]
        yield
        tinvs = []
        yield from _unit_lower_inverse_steps(ms, C, tinvs)
        sols = [_dot_bf(t, jnp.concatenate([v * bh, kb * eg], axis=-1))
                for t, v, bh, kb, eg in zip(tinvs, vs, bhs, kbs, egs)]
        yield
        ss = [s_ref[bi, h] for h in heads]
        v_news = [sol[:, :GDN_DV] - _dot_bf(sol[:, GDN_DV:], s) for sol, s in zip(sols, ss)]
        yield
        os_ = [_dot_bf(q * eg, s) + _dot_bf(a, vn) for q, eg, s, a, vn in zip(qs, egs, ss, attns, v_news)]
        yield
        for h in heads:
            glast = gcs[h][C - 1:C, :]
            s_ref[bi, h] = ss[h] * jnp.exp(glast) + _dot_tn_bf(ks[h] * jnp.exp(glast - gcs[h]), v_news[h])
        yield
        for h in heads:
            o = os_[h]
            o = o * lax.rsqrt(jnp.mean(o * o, -1, keepdims=True) + RMS_EPS) * ng_ref[...]
            zh = z_ref[bi, :, h * GDN_DV:(h + 1) * GDN_DV]
            y_ref[bi, :, h * GDN_DV:(h + 1) * GDN_DV] = o * _silu(zh)

    n_seq = hq_ref.shape[0]
    data = [{} for _ in range(n_seq)]
    for _ in prepare(0, data[0]):
        pass
    for bi in range(n_seq):
        filler = prepare(bi + 1, data[bi + 1]) if bi + 1 < n_seq else iter(())
        for _ in recur(bi, data[bi]):
            next(filler, None)
        for _ in filler:
            pass

    @pl.when(c_idx == n_c - 1)
    def _():
        snew_ref[...] = s_ref[...]
        bufnew_ref[...] = xp_ref[:, TAIL_ROWS - keep:TAIL_ROWS, :]


def _gdn(h3, l, conv_buf, s0, conv_w, a_log, dt_bias, norm_g):
    b = h3.shape[0]
    chunk = min(CHUNK, l)
    assert l % chunk == 0 and chunk % SUBLANES == 0 and chunk >= CONV_W - 1
    kern = functools.partial(_gdn_kernel, chunk=chunk)
    keep = CONV_W - 1
    lane_row = lambda vec, at: jnp.zeros((1, LANES), F32).at[0, at:at + vec.shape[0]].set(vec)
    gb = _pick_tile(b, GDN_BATCH_TILE)
    return pl.pallas_call(
        kern,
        grid=(b // gb, l // chunk),
        in_specs=[
            pl.BlockSpec((gb, chunk, CONV_CH), lambda i, c: (i, c, COL_QKV // CONV_CH)),
            pl.BlockSpec((gb, chunk, GDN_V), lambda i, c: (i, c, COL_Z // GDN_V)),
            pl.BlockSpec((gb, chunk, SMALL_W), lambda i, c: (i, c, COL_SMALL // SMALL_W)),
            pl.BlockSpec((gb, keep, CONV_CH), lambda i, c: (i, 0, 0)),
            pl.BlockSpec((gb, GDN_HEADS, GDN_DK, GDN_DV), lambda i, c: (i, 0, 0, 0)),
            pl.BlockSpec((CONV_W, CONV_CH), lambda i, c: (0, 0)),
            pl.BlockSpec((1, LANES), lambda i, c: (0, 0)),
            pl.BlockSpec((1, LANES), lambda i, c: (0, 0)),
            pl.BlockSpec((1, GDN_DV), lambda i, c: (0, 0)),
        ],
        out_specs=[
            pl.BlockSpec((gb, chunk, GDN_V), lambda i, c: (i, c, 0)),
            pl.BlockSpec((gb, GDN_HEADS, GDN_DK, GDN_DV), lambda i, c: (i, 0, 0, 0)),
            pl.BlockSpec((gb, keep, CONV_CH), lambda i, c: (i, 0, 0)),
        ],
        out_shape=[
            jax.ShapeDtypeStruct((b, l, GDN_V), F32),
            jax.ShapeDtypeStruct((b, GDN_HEADS, GDN_DK, GDN_DV), F32),
            jax.ShapeDtypeStruct((b, keep, CONV_CH), F32),
        ],
        scratch_shapes=[
            pltpu.VMEM((gb, TAIL_ROWS + chunk, CONV_CH), F32),
            pltpu.VMEM((gb, GDN_HEADS, GDN_DK, GDN_DV), F32),
        ],
        compiler_params=_cparams(("parallel", "arbitrary")),
        name="gdn",
    )(h3, h3, h3, conv_buf, s0, conv_w, lane_row(a_log, SM_A), lane_row(dt_bias, SM_A),
      norm_g.reshape(1, GDN_DV))


def _rope_tables(pos, rot, period):
    half = rot // 2
    inv_freq = ROPE_THETA ** (-(2.0 / rot) * jnp.arange(half, dtype=F32))
    ang = pos.astype(F32)[:, None] * inv_freq[None, :]
    cos, sin = jnp.cos(ang), jnp.sin(ang)
    n = pos.shape[0]
    rest = period - rot
    c = jnp.concatenate([cos, cos, jnp.ones((n, rest), F32)], -1)
    sa = jnp.concatenate([-sin, jnp.zeros((n, half + rest), F32)], -1)
    sb = jnp.concatenate([jnp.zeros((n, half), F32), sin, jnp.zeros((n, rest), F32)], -1)
    reps = LANES // period
    return jnp.stack([jnp.tile(c, (1, reps)), jnp.tile(sa, (1, reps)), jnp.tile(sb, (1, reps))], 0)


Q_SCALE_LOG2 = HEAD_DIM ** -0.5 * float(np.log2(np.e))


def _rope128(x, tab_ref, half):
    return (x * tab_ref[0] + pltpu.roll(x, LANES - half, 1) * tab_ref[1]
            + pltpu.roll(x, half, 1) * tab_ref[2])


def _prep_kernel(q_ref, qi_ref, k_ref, v_ref, sm_ref, tq_ref, ti_ref, lng_ref, lnb_ref,
                 qt_ref, qit_ref, wit_ref, ko_ref, kb_ref, kio_ref, kib_ref, vo_ref, vt_ref):
    hq = HEAD_DIM // 8
    hi = IDX_DIM // 8
    for h in range(ATT_HEADS):
        x = _rope128(q_ref[:, h * HEAD_DIM:(h + 1) * HEAD_DIM], tq_ref, hq) * Q_SCALE_LOG2
        qt_ref[h * HEAD_DIM:(h + 1) * HEAD_DIM, :] = x.T.astype(BF16)
    for h in range(ATT_KV_HEADS):
        x = _rope128(k_ref[:, h * HEAD_DIM:(h + 1) * HEAD_DIM], tq_ref, hq)
        ko_ref[:, h * HEAD_DIM:(h + 1) * HEAD_DIM] = x
        kb_ref[:, h * HEAD_DIM:(h + 1) * HEAD_DIM] = x.astype(BF16)
        v = v_ref[:, h * HEAD_DIM:(h + 1) * HEAD_DIM]
        vo_ref[:, h * HEAD_DIM:(h + 1) * HEAD_DIM] = v
        vt_ref[h * HEAD_DIM:(h + 1) * HEAD_DIM, :] = v.T.astype(BF16)
    for c in range(IDX_Q // LANES):
        x = _rope128(qi_ref[:, c * LANES:(c + 1) * LANES], ti_ref, hi)
        qit_ref[c * LANES:(c + 1) * LANES, :] = x.T.astype(BF16)
    sm = sm_ref[:, 0:LANES]
    lane = lax.broadcasted_iota(jnp.int32, sm.shape, 1)
    is_ki = lane < IDX_DIM
    mu = jnp.sum(jnp.where(is_ki, sm, 0.0), -1, keepdims=True) * (1.0 / IDX_DIM)
    xc = jnp.where(is_ki, sm - mu, 0.0)
    var = jnp.sum(xc * xc, -1, keepdims=True) * (1.0 / IDX_DIM)
    ki = xc * lax.rsqrt(var + LN_EPS) * lng_ref[...] + lnb_ref[...]
    ki = _rope128(ki, ti_ref, hi)[:, 0:IDX_DIM]
    kio_ref[...] = ki
    kib_ref[...] = ki.astype(BF16)
    wit_ref[...] = sm.T[SM_WI:SM_WI + IDX_HEADS, :] * (IDX_HEADS ** -0.5 * IDX_DIM ** -0.5)


def _prep(h3, pos, ln_g, ln_b):
    b, l, _ = h3.shape
    tm = _pick_tile(l, PREP_TM)
    tab_q = _rope_tables(pos, HEAD_DIM // 4, HEAD_DIM)
    tab_i = _rope_tables(pos, IDX_DIM // 4, IDX_DIM)
    lng = jnp.concatenate([ln_g, jnp.zeros((LANES - IDX_DIM,), F32)]).reshape(1, LANES)
    lnb = jnp.concatenate([ln_b, jnp.zeros((LANES - IDX_DIM,), F32)]).reshape(1, LANES)
    rows = lambda w, col: pl.BlockSpec((None, tm, w), lambda i, t: (i, t, col))
    cols = lambda w: pl.BlockSpec((None, w, tm), lambda i, t: (i, 0, t))
    return pl.pallas_call(
        _prep_kernel,
        grid=(b, l // tm),
        in_specs=[
            rows(ATT_Q, COL_Q // ATT_Q), rows(IDX_Q, COL_QI // IDX_Q), rows(ATT_KV, COL_K // ATT_KV),
            rows(ATT_KV, COL_V // ATT_KV), rows(SMALL_W, COL_SMALL // SMALL_W),
            pl.BlockSpec((3, tm, LANES), lambda i, t: (0, t, 0)),
            pl.BlockSpec((3, tm, LANES), lambda i, t: (0, t, 0)),
            pl.BlockSpec((1, LANES), lambda i, t: (0, 0)),
            pl.BlockSpec((1, LANES), lambda i, t: (0, 0)),
        ],
        out_specs=[cols(ATT_Q), cols(IDX_Q), cols(IDX_HEADS), rows(ATT_KV, 0), rows(ATT_KV, 0),
                   rows(IDX_DIM, 0), rows(IDX_DIM, 0), rows(ATT_KV, 0), cols(ATT_KV)],
        out_shape=[
            jax.ShapeDtypeStruct((b, ATT_Q, l), BF16),
            jax.ShapeDtypeStruct((b, IDX_Q, l), BF16),
            jax.ShapeDtypeStruct((b, IDX_HEADS, l), F32),
            jax.ShapeDtypeStruct((b, l, ATT_KV), F32),
            jax.ShapeDtypeStruct((b, l, ATT_KV), BF16),
            jax.ShapeDtypeStruct((b, l, IDX_DIM), F32),
            jax.ShapeDtypeStruct((b, l, IDX_DIM), BF16),
            jax.ShapeDtypeStruct((b, l, ATT_KV), F32),
            jax.ShapeDtypeStruct((b, ATT_KV, l), BF16),
        ],
        compiler_params=_cparams(("parallel", "parallel")),
        name="dsa_prep",
    )(h3, h3, h3, h3, h3, tab_q, tab_i, lng, lnb)


INT_MIN = -2 ** 31
INT_MAX = 2 ** 31 - 1
NEG_INF_KEY = INT_MIN + 0x7FFFFF
NEG_BIG = -1e30
HEAD_PAIRS = IDX_HEADS // 2
COUNT_CHAINS = 8
ATTN_BLOCKS_PER_STEP = 4
SCORE_CLASSES = TOPK_MAX
DENOM_ROWS = 16


def _dsa_kernel(qt_ref, qit_ref, wit_ref, ki_ref, k_ref, vt_ref, o_ref,
                key_ref, cls_ref, qs_ref, qip_ref, m_ref, acc_ref, cm_ref,
                *, tq, kb, l_true, pos0, topk, idx_bits):
    q0 = pos0 + pl.program_id(1) * tq
    qpos = q0 + lax.broadcasted_iota(jnp.int32, (1, tq), 1)
    lim = jnp.minimum((qpos // CHUNK + 1) * CHUNK, l_true)
    lim_max = jnp.minimum(((q0 + tq - 1) // CHUNK + 1) * CHUNK, l_true)
    nkb = (lim_max + kb - 1) // kb
    key_iota = lax.broadcasted_iota(jnp.int32, (kb, tq), 0)
    wit = wit_ref[...]

    for p in range(HEAD_PAIRS):
        for u in range(2):
            hh = 2 * p + u
            qip_ref[p, :, u * tq:(u + 1) * tq] = qit_ref[hh * IDX_DIM:(hh + 1) * IDX_DIM, :]
    for g in range(ATT_KV_HEADS):
        for r in range(REP):
            hh = g * REP + r
            qs_ref[g, :, r * tq:(r + 1) * tq] = qt_ref[hh * HEAD_DIM:(hh + 1) * HEAD_DIM, :]

    def to_key(score):
        bits = pltpu.bitcast(score, jnp.int32)
        return bits ^ ((bits >> 31) & INT_MAX)

    def score_blk(j, carry):
        off = pl.multiple_of(j * kb, kb)
        ki = ki_ref[pl.ds(off, kb), :]
        acc = jnp.zeros((kb, tq), F32)
        for p in range(HEAD_PAIRS):
            s2 = _dot(ki, qip_ref[p])
            acc = (acc + wit[2 * p:2 * p + 1, :] * jnp.maximum(s2[:, :tq], 0.0)
                   + wit[2 * p + 1:2 * p + 2, :] * jnp.maximum(s2[:, tq:], 0.0))
        score = jnp.where(off + key_iota < lim, acc, -jnp.inf)
        key_ref[pl.ds(off, kb), :] = to_key(score)
        for s in range(kb // SCORE_CLASSES):
            cls_ref[...] = jnp.maximum(cls_ref[...], score[s * SCORE_CLASSES:(s + 1) * SCORE_CLASSES, :])
        return carry

    cls_ref[...] = jnp.full(cls_ref.shape, -jnp.inf, F32)
    lax.fori_loop(0, nkb, score_blk, 0)

    def count(pred_fn):
        def blk(j, c):
            off = pl.multiple_of(j * kb, kb)
            part = jnp.where(pred_fn(key_ref[pl.ds(off, kb), :], off), 1.0, 0.0)
            return c + jnp.sum(part.reshape(kb // (COUNT_CHAINS * SUBLANES), COUNT_CHAINS * SUBLANES, tq), axis=0)
        c = lax.fori_loop(0, nkb, blk, jnp.zeros((COUNT_CHAINS * SUBLANES, tq), F32))
        return jnp.sum(c, axis=0, keepdims=True)

    cls = cls_ref[...]
    u_hi = to_key(jnp.max(cls, axis=0, keepdims=True)) ^ INT_MIN
    u_lo = to_key(jnp.min(cls, axis=0, keepdims=True)) ^ INT_MIN
    n_bits = 32 - jnp.min(lax.clz(u_hi ^ u_lo))
    prefix_mask = jnp.where(n_bits >= 32, 0, lax.shift_left(jnp.int32(-1), jnp.minimum(n_bits, 31)))

    def bit_step(t, cur):
        cand_u = cur | lax.shift_left(jnp.int32(1), n_bits - 1 - t)
        cand_s = cand_u ^ INT_MIN
        cnt = count(lambda kk, off: kk >= cand_s)
        return jnp.where(cnt >= topk, cand_u, cur)

    thr = lax.fori_loop(0, n_bits, bit_step, u_hi & prefix_mask) ^ INT_MIN
    n_ge = count(lambda kk, off: kk >= thr)

    cm_ref[...] = jnp.full((1, tq), INT_MAX, jnp.int32)

    @pl.when(jnp.max(jnp.where((n_ge > topk) & (thr > NEG_INF_KEY), 1, 0)) > 0)
    def _():
        need = topk - count(lambda kk, off: kk > thr)

        def idx_step(t, cm):
            cand = cm | lax.shift_left(jnp.int32(1), idx_bits - 1 - t)
            before = count(lambda kk, off: (kk == thr) & (off + key_iota < cand))
            return jnp.where(before < need, cand, cm)
        cm_ref[...] = lax.fori_loop(0, idx_bits, idx_step, jnp.zeros((1, tq), jnp.int32))

    cm = cm_ref[...]

    m_ref[...] = jnp.full(m_ref.shape, NEG_BIG, F32)
    acc_ref[...] = jnp.zeros(acc_ref.shape, F32)
    groups = range(ATT_KV_HEADS)
    ones_rows = jnp.ones((DENOM_ROWS, kb), BF16)

    def attn_blocks(js):
        offs = [pl.multiple_of(j * kb, kb) for j in js]
        biases = {}

        def logits_of(b, g):
            off = offs[b]
            if b not in biases:
                kk = key_ref[pl.ds(off, kb), :]
                kpos = off + key_iota
                sel = ((kk > thr) | ((kk == thr) & (kpos <= cm))) & (kpos < lim)
                biases[b] = jnp.concatenate([jnp.where(sel, 0.0, NEG_BIG)] * REP, axis=1)
            return _dot(k_ref[pl.ds(off, kb), g * HEAD_DIM:(g + 1) * HEAD_DIM], qs_ref[g]) + biases[b]

        def update(b, g, logits):
            off = offs[b]
            m_old = m_ref[g]
            m_new = jnp.maximum(m_old, jnp.max(logits, axis=0, keepdims=True))
            p = jnp.exp2(logits - m_new).astype(BF16)
            v_aug = jnp.concatenate([vt_ref[g * HEAD_DIM:(g + 1) * HEAD_DIM, pl.ds(off, kb)], ones_rows], axis=0)
            acc_ref[g] = jnp.exp2(m_old - m_new) * acc_ref[g] + _dot(v_aug, p)
            m_ref[g] = m_new

        steps = [(b, g) for b in range(len(js)) for g in groups]
        pending = logits_of(*steps[0])
        for n, (b, g) in enumerate(steps):
            nxt = logits_of(*steps[n + 1]) if n + 1 < len(steps) else None
            update(b, g, pending)
            pending = nxt

    def attn_group(jg, carry):
        attn_blocks([ATTN_BLOCKS_PER_STEP * jg + u for u in range(ATTN_BLOCKS_PER_STEP)])
        return carry

    def attn_single(j, carry):
        attn_blocks([j])
        return carry

    n_groups = nkb // ATTN_BLOCKS_PER_STEP
    lax.fori_loop(0, n_groups, attn_group, 0)
    lax.fori_loop(n_groups * ATTN_BLOCKS_PER_STEP, nkb, attn_single, 0)

    for g in groups:
        o_t = acc_ref[g, 0:HEAD_DIM, :] / acc_ref[g, HEAD_DIM:HEAD_DIM + 1, :]
        for r in range(REP):
            hh = g * REP + r
            o_ref[:, hh * HEAD_DIM:(hh + 1) * HEAD_DIM] = o_t[:, r * tq:(r + 1) * tq].T


def _dsa(qt, qit, wit, ki, k, vt, l_true, pos0):
    b, _, t = qt.shape
    lk = k.shape[1]
    tq, kb = DSA_Q_TILE, DSA_KEY_BLOCK
    assert t % tq == 0 and lk % kb == 0 and tq == HEAD_DIM and kb % SCORE_CLASSES == 0
    topk = min(TOPK_MAX, l_true // 4)
    kern = functools.partial(_dsa_kernel, tq=tq, kb=kb, l_true=l_true, pos0=pos0, topk=topk,
                             idx_bits=int(lk).bit_length())
    return pl.pallas_call(
        kern,
        grid=(b, t // tq),
        in_specs=[
            pl.BlockSpec((None, ATT_Q, tq), lambda i, t_: (i, 0, t_)),
            pl.BlockSpec((None, IDX_Q, tq), lambda i, t_: (i, 0, t_)),
            pl.BlockSpec((None, IDX_HEADS, tq), lambda i, t_: (i, 0, t_)),
            pl.BlockSpec((None, lk, IDX_DIM), lambda i, t_: (i, 0, 0)),
            pl.BlockSpec((None, lk, ATT_KV), lambda i, t_: (i, 0, 0)),
            pl.BlockSpec((None, ATT_KV, lk), lambda i, t_: (i, 0, 0)),
        ],
        out_specs=pl.BlockSpec((None, tq, ATT_Q), lambda i, t_: (i, t_, 0)),
        out_shape=jax.ShapeDtypeStruct((b, t, ATT_Q), F32),
        scratch_shapes=[
            pltpu.VMEM((lk, tq), jnp.int32),
            pltpu.VMEM((SCORE_CLASSES, tq), F32),
            pltpu.VMEM((ATT_KV_HEADS, HEAD_DIM, REP * tq), BF16),
            pltpu.VMEM((HEAD_PAIRS, IDX_DIM, 2 * tq), BF16),
            pltpu.VMEM((ATT_KV_HEADS, 1, REP * tq), F32),
            pltpu.VMEM((ATT_KV_HEADS, HEAD_DIM + DENOM_ROWS, REP * tq), F32),
            pltpu.VMEM((1, tq), jnp.int32),
        ],
        compiler_params=_cparams(("parallel", "arbitrary")),
        name="dsa",
    )(qt, qit, wit, ki, k, vt)


def _layer_norm(x, g, b):
    mu = jnp.mean(x, -1, keepdims=True)
    xc = x - mu
    var = jnp.mean(xc * xc, -1, keepdims=True)
    return xc * lax.rsqrt(var + LN_EPS) * g + b


def _merge_kernel(ya_ref, yb_ref, ga_ref, gb_ref, x_ref, wa_ref, wb_ref, wo_ref, g_ref, b_ref, o_ref):
    pa = _dot(ya_ref[...].astype(BF16), wa_ref[...])
    pb = _dot(yb_ref[...].astype(BF16), wb_ref[...])
    merged = _sigmoid(ga_ref[...]) * pa + _sigmoid(gb_ref[...]) * pb
    y = DN_ALPHA * x_ref[...] + _dot(merged.astype(BF16), wo_ref[...])
    o_ref[...] = _layer_norm(y, g_ref[...], b_ref[...])


def _merge(ya, yb, h3, x, wa, wb, wo, g, b):
    bsz, l, _ = x.shape
    tm = _pick_tile(l, TOKEN_TM)
    row = lambda c: pl.BlockSpec((None, tm, D_MODEL), lambda i, t: (i, t, c))
    full = lambda shp: pl.BlockSpec(shp, lambda i, t: (0, 0))
    return pl.pallas_call(
        _merge_kernel,
        grid=(bsz, l // tm),
        in_specs=[row(0), row(0), row(COL_GA // D_MODEL), row(COL_GB // D_MODEL), row(0),
                  full((GDN_V, D_MODEL)), full((ATT_Q, D_MODEL)), full((D_MODEL, D_MODEL)),
                  full((1, D_MODEL)), full((1, D_MODEL))],
        out_specs=row(0),
        out_shape=jax.ShapeDtypeStruct((bsz, l, D_MODEL), F32),
        compiler_params=_cparams(("parallel", "parallel")),
        name="merge_ln1",
    )(ya, yb, h3, h3, x, wa, wb, wo, g.reshape(1, D_MODEL), b.reshape(1, D_MODEL))


GROUP_SIZE = N_EXPERTS // N_GROUPS

SLAB_ROWS = D_MODEL // LANES


def _load_token_slabs(ref, n_tok, *lead):
    return jnp.concatenate([ref[lead + (pl.ds(c, n_tok, stride=SLAB_ROWS), slice(None))]
                            for c in range(SLAB_ROWS)], axis=1)


def _store_token_slabs(ref, x):
    for c in range(SLAB_ROWS):
        ref[pl.ds(c, x.shape[0], stride=SLAB_ROWS), :] = x[:, c * LANES:(c + 1) * LANES]


def _first_max(cur, rows, n_rows):
    m = jnp.max(cur, axis=0, keepdims=True)
    idx = jnp.min(jnp.where(cur == m, rows, n_rows), axis=0, keepdims=True)
    return m, idx


def _router_kernel(x_ref, wr_ref, rb_ref, ws1_ref, ws3_ref, ws2_ref,
                   eidx_ref, wts_ref, rank_ref, cnt_ref, base_ref, xslab_ref, run_ref):
    @pl.when(pl.program_id(0) == 0)
    def _():
        run_ref[...] = jnp.zeros(run_ref.shape, F32)

    x = x_ref[...]
    tm = x.shape[0]
    logits = _dot_nt_hi(wr_ref[...], x)
    scores = _sigmoid(logits)
    biased = scores + rb_ref[...]
    neg = -jnp.inf

    rows_g = lax.broadcasted_iota(jnp.int32, (GROUP_SIZE, tm), 0)
    gs = []
    for g in range(N_GROUPS):
        blk = biased[g * GROUP_SIZE:(g + 1) * GROUP_SIZE, :]
        m1, i1 = _first_max(blk, rows_g, GROUP_SIZE)
        m2 = jnp.max(jnp.where(rows_g == i1, neg, blk), axis=0, keepdims=True)
        gs.append(m1 + m2)
    cur = jnp.concatenate(gs, axis=0)
    rows_n = lax.broadcasted_iota(jnp.int32, (N_GROUPS, tm), 0)
    gsel = jnp.zeros((N_GROUPS, tm), F32)
    for _ in range(TOPK_GROUPS):
        _, ig = _first_max(cur, rows_n, N_GROUPS)
        hit = rows_n == ig
        gsel = jnp.where(hit, 1.0, gsel)
        cur = jnp.where(hit, neg, cur)
    gexp = jnp.concatenate([jnp.broadcast_to(gsel[g:g + 1, :], (GROUP_SIZE, tm)) for g in range(N_GROUPS)], axis=0)
    cur = jnp.where(gexp > 0.0, biased, neg)

    rows_e = lax.broadcasted_iota(jnp.int32, (N_EXPERTS, tm), 0)
    es, ws, hits = [], [], []
    for _ in range(EXPERT_TOPK):
        _, ie = _first_max(cur, rows_e, N_EXPERTS)
        hit = rows_e == ie
        es.append(ie)
        hits.append(hit)
        ws.append(jnp.sum(jnp.where(hit, scores, 0.0), axis=0, keepdims=True))
        cur = jnp.where(hit, neg, cur)
    w = jnp.concatenate(ws, axis=0)
    eidx_ref[...] = jnp.concatenate(es, axis=0)
    wts_ref[...] = w / jnp.sum(w, axis=0, keepdims=True) * ROUTED_SCALE

    assigned = jnp.zeros((N_EXPERTS, tm), F32)
    for hit in hits:
        assigned = jnp.where(hit, 1.0, assigned)
    earlier = (lax.broadcasted_iota(jnp.int32, (tm, tm), 0) < lax.broadcasted_iota(jnp.int32, (tm, tm), 1))
    before = run_ref[...] + _dot(assigned.astype(BF16), earlier.astype(BF16))
    rank_ref[...] = jnp.concatenate(
        [jnp.sum(jnp.where(hit, before, 0.0), axis=0, keepdims=True) for hit in hits], axis=0).astype(jnp.int32)
    run_ref[...] = run_ref[...] + jnp.sum(assigned, axis=1, keepdims=True)
    cnt_ref[...] = run_ref[...].astype(jnp.int32)

    xb = x.astype(BF16)
    hs = _silu(_dot(xb, ws1_ref[...])) * _dot(xb, ws3_ref[...])
    base_ref[...] = DN_ALPHA * x + _dot(hs.astype(BF16), ws2_ref[...])
    _store_token_slabs(xslab_ref, x)


def _router(x1, wr_t, rbias, ws1, ws3, ws2, tm):
    t = x1.shape[0]
    full = lambda shp: pl.BlockSpec(shp, lambda i: (0, 0))
    return pl.pallas_call(
        _router_kernel,
        grid=(t // tm,),
        in_specs=[pl.BlockSpec((tm, D_MODEL), lambda i: (i, 0)),
                  full((N_EXPERTS, D_MODEL)), full((N_EXPERTS, 1)),
                  full((D_MODEL, SHARED_DIM)), full((D_MODEL, SHARED_DIM)), full((SHARED_DIM, D_MODEL))],
        out_specs=[pl.BlockSpec((EXPERT_TOPK, tm), lambda i: (0, i)),
                   pl.BlockSpec((EXPERT_TOPK, tm), lambda i: (0, i)),
                   pl.BlockSpec((EXPERT_TOPK, tm), lambda i: (0, i)),
                   full((N_EXPERTS, 1)),
                   pl.BlockSpec((tm, D_MODEL), lambda i: (i, 0)),
                   pl.BlockSpec((tm * SLAB_ROWS, LANES), lambda i: (i, 0))],
        out_shape=[jax.ShapeDtypeStruct((EXPERT_TOPK, t), jnp.int32),
                   jax.ShapeDtypeStruct((EXPERT_TOPK, t), F32),
                   jax.ShapeDtypeStruct((EXPERT_TOPK, t), jnp.int32),
                   jax.ShapeDtypeStruct((N_EXPERTS, 1), jnp.int32),
                   jax.ShapeDtypeStruct((t, D_MODEL), F32),
                   jax.ShapeDtypeStruct((t * SLAB_ROWS, LANES), F32)],
        scratch_shapes=[pltpu.VMEM((N_EXPERTS, 1), F32)],
        compiler_params=_cparams(("arbitrary",)),
        name="router_shared",
    )(x1, wr_t, rbias.reshape(N_EXPERTS, 1), ws1, ws3, ws2)


def _block_plan(counts, n_blocks):
    padded = (counts + MOE_ROWS - 1) // MOE_ROWS * MOE_ROWS
    pad_end = jnp.cumsum(padded)
    blk_first = jnp.arange(n_blocks, dtype=jnp.int32) * MOE_ROWS
    blk_exp = jnp.minimum(jnp.sum((pad_end[None, :] <= blk_first[:, None]).astype(jnp.int32), axis=1), N_EXPERTS - 1)
    return pad_end - padded, jnp.maximum(pad_end - MOE_ROWS, 0), blk_exp, pad_end[-1:] // MOE_ROWS


def _pos_kernel(e_ref, r_ref, ps_ref, pos_ref):
    tm = e_ref.shape[1]
    rows_e = lax.broadcasted_iota(jnp.int32, (N_EXPERTS, tm), 0)
    first = [jnp.sum(jnp.where(rows_e == e_ref[j:j + 1, :], ps_ref[...], 0), axis=0, keepdims=True)
             for j in range(EXPERT_TOPK)]
    pos_ref[...] = jnp.concatenate(first, axis=0) + r_ref[...]


def _positions(eidx_t, rank_t, pad_start, tm):
    t = eidx_t.shape[1]
    blk = pl.BlockSpec((EXPERT_TOPK, tm), lambda i: (0, i))
    return pl.pallas_call(
        _pos_kernel,
        grid=(t // tm,),
        in_specs=[blk, blk, pl.BlockSpec((N_EXPERTS, 1), lambda i: (0, 0))],
        out_specs=blk,
        out_shape=jax.ShapeDtypeStruct((EXPERT_TOPK, t), jnp.int32),
        compiler_params=_cparams(("parallel",)),
        name="moe_positions",
    )(eidx_t, rank_t, pad_start.reshape(N_EXPERTS, 1))


def _scatter_kernel(last_ref, nu_ref, pos_ref, x_ref, rows_ref, zero_ref, sem, zsem):
    tm = x_ref.shape[0] // SLAB_ROWS
    block_slabs = MOE_ROWS * SLAB_ROWS
    n_blocks = rows_ref.shape[0] // block_slabs

    def slab(ref, row):
        return ref.at[pl.ds(pl.multiple_of(row * SLAB_ROWS, SLAB_ROWS), SLAB_ROWS), :]

    @pl.when(pl.program_id(0) == 0)
    def _():
        zero_ref[...] = jnp.zeros(zero_ref.shape, F32)

        def zero_copy(first_row):
            first = pl.multiple_of(first_row * SLAB_ROWS, block_slabs)
            return pltpu.make_async_copy(zero_ref, rows_ref.at[pl.ds(first, block_slabs), :], zsem)

        def each(fn):
            def expert_block(e, carry):
                fn(zero_copy(last_ref[e]))
                return carry

            def unused_block(blk, carry):
                fn(zero_copy(blk * MOE_ROWS))
                return carry

            lax.fori_loop(0, N_EXPERTS, expert_block, 0)
            lax.fori_loop(nu_ref[0], n_blocks, unused_block, 0)

        each(lambda copy: copy.start())
        each(lambda copy: copy.wait())

    def row_copy(t, j):
        return pltpu.make_async_copy(slab(x_ref, t), slab(rows_ref, pos_ref[j, t]), sem)

    def issue(t, carry):
        for j in range(EXPERT_TOPK):
            row_copy(t, j).start(priority=j % DMA_PRIORITIES)
        return carry

    lax.fori_loop(0, tm, issue, 0)
    for j in range(EXPERT_TOPK):
        pltpu.make_async_copy(x_ref, rows_ref.at[pl.ds(0, tm * SLAB_ROWS), :], sem).wait()


def _scatter_rows(last_block_row, n_used, pos_t, x_slabs, n_rows, tm):
    t = x_slabs.shape[0] // SLAB_ROWS
    grid_spec = pltpu.PrefetchScalarGridSpec(
        num_scalar_prefetch=2,
        grid=(t // tm,),
        in_specs=[pl.BlockSpec((EXPERT_TOPK, tm), lambda i, lr, nu: (0, i), memory_space=pltpu.SMEM),
                  pl.BlockSpec((tm * SLAB_ROWS, LANES), lambda i, lr, nu: (i, 0))],
        out_specs=pl.BlockSpec(memory_space=pl.ANY),
        scratch_shapes=[pltpu.VMEM((MOE_ROWS * SLAB_ROWS, LANES), F32), pltpu.SemaphoreType.DMA(()),
                        pltpu.SemaphoreType.DMA(())],
    )
    return pl.pallas_call(
        _scatter_kernel,
        grid_spec=grid_spec,
        out_shape=jax.ShapeDtypeStruct((n_rows * SLAB_ROWS, LANES), F32),
        compiler_params=_cparams(("arbitrary",)),
        name="moe_scatter",
    )(last_block_row, n_used, pos_t, x_slabs)


def _expert_kernel(be_ref, nu_ref, x_ref, w1_ref, w3_ref, w2_ref, o_ref, w1b_ref, w3b_ref, w2b_ref):
    i = pl.program_id(0)
    prev = be_ref[jnp.maximum(i - 1, 0)]

    @pl.when((i == 0) | (be_ref[i] != prev))
    def _():
        w1b_ref[...] = w1_ref[...].astype(BF16)
        w3b_ref[...] = w3_ref[...].astype(BF16)
        w2b_ref[...] = w2_ref[...].astype(BF16)

    @pl.when(i < nu_ref[0])
    def _():
        x = _load_token_slabs(x_ref, MOE_ROWS).astype(BF16)
        hmid = _silu(_dot(x, w1b_ref[...])) * _dot(x, w3b_ref[...])
        _store_token_slabs(o_ref, _dot(hmid.astype(BF16), w2b_ref[...]))

    @pl.when(i >= nu_ref[0])
    def _():
        o_ref[...] = jnp.zeros(o_ref.shape, F32)


def _experts(x_slabs, blk_exp, n_used, w1, w3, w2):
    block_slabs = MOE_ROWS * SLAB_ROWS
    n_blocks = x_slabs.shape[0] // block_slabs
    grid_spec = pltpu.PrefetchScalarGridSpec(
        num_scalar_prefetch=2,
        grid=(n_blocks,),
        in_specs=[
            pl.BlockSpec((block_slabs, LANES), lambda i, be, nu: (jnp.minimum(i, nu[0] - 1), 0)),
            pl.BlockSpec((None, D_MODEL, EXPERT_DIM), lambda i, be, nu: (be[i], 0, 0)),
            pl.BlockSpec((None, D_MODEL, EXPERT_DIM), lambda i, be, nu: (be[i], 0, 0)),
            pl.BlockSpec((None, EXPERT_DIM, D_MODEL), lambda i, be, nu: (be[i], 0, 0)),
        ],
        out_specs=pl.BlockSpec((block_slabs, LANES), lambda i, be, nu: (i, 0)),
        scratch_shapes=[pltpu.VMEM((D_MODEL, EXPERT_DIM), BF16), pltpu.VMEM((D_MODEL, EXPERT_DIM), BF16),
                        pltpu.VMEM((EXPERT_DIM, D_MODEL), BF16)],
    )
    return pl.pallas_call(
        _expert_kernel,
        grid_spec=grid_spec,
        out_shape=jax.ShapeDtypeStruct(x_slabs.shape, F32),
        compiler_params=_cparams(("arbitrary",)),
        name="experts",
    )(blk_exp, n_used, x_slabs, w1, w3, w2)


def _combine_kernel(pos_ref, pos_next_ref, y_ref, w_ref, base_ref, g_ref, b_ref, o_ref, gbuf_ref, sem):
    tm = base_ref.shape[0]
    i = pl.program_id(0)
    slot = lax.rem(i, 2)

    def slab(row):
        return pl.ds(pl.multiple_of(row * SLAB_ROWS, SLAB_ROWS), SLAB_ROWS)

    def issue_tile(p_ref, s):
        def issue(t, carry):
            for j in range(EXPERT_TOPK):
                pltpu.make_async_copy(y_ref.at[slab(p_ref[j, t]), :], gbuf_ref.at[s, j, slab(t), :],
                                      sem.at[s]).start(priority=j % DMA_PRIORITIES)
            return carry
        lax.fori_loop(0, tm, issue, 0)

    @pl.when(i == 0)
    def _():
        issue_tile(pos_ref, 0)

    @pl.when(i + 1 < pl.num_programs(0))
    def _():
        issue_tile(pos_next_ref, 1 - slot)

    for j in range(EXPERT_TOPK):
        pltpu.make_async_copy(y_ref.at[pl.ds(0, tm * SLAB_ROWS), :], gbuf_ref.at[slot, j], sem.at[slot]).wait()

    acc = base_ref[...]
    w = w_ref[...]
    for j in range(EXPERT_TOPK):
        acc = acc + w[:, j:j + 1] * _load_token_slabs(gbuf_ref, tm, slot, j)
    o_ref[...] = _layer_norm(acc, g_ref[...], b_ref[...])


def _combine(pos_t, y_rows, wts, base, g, b, tm):
    t = base.shape[0]
    n_tiles = t // tm
    return pl.pallas_call(
        _combine_kernel,
        grid=(n_tiles,),
        in_specs=[pl.BlockSpec((EXPERT_TOPK, tm), lambda i: (0, i), memory_space=pltpu.SMEM),
                  pl.BlockSpec((EXPERT_TOPK, tm), lambda i: (0, jnp.minimum(i + 1, n_tiles - 1)),
                               memory_space=pltpu.SMEM),
                  pl.BlockSpec(memory_space=pl.ANY),
                  pl.BlockSpec((tm, EXPERT_TOPK), lambda i: (i, 0)),
                  pl.BlockSpec((tm, D_MODEL), lambda i: (i, 0)),
                  pl.BlockSpec((1, D_MODEL), lambda i: (0, 0)),
                  pl.BlockSpec((1, D_MODEL), lambda i: (0, 0))],
        out_specs=pl.BlockSpec((tm, D_MODEL), lambda i: (i, 0)),
        out_shape=jax.ShapeDtypeStruct((t, D_MODEL), F32),
        scratch_shapes=[pltpu.VMEM((2, EXPERT_TOPK, tm * SLAB_ROWS, LANES), F32), pltpu.SemaphoreType.DMA((2,))],
        compiler_params=_cparams(("arbitrary",)),
        name="combine_ln2",
    )(pos_t, pos_t, y_rows, wts, base, g.reshape(1, D_MODEL), b.reshape(1, D_MODEL))


def _moe(x1, prm):
    t = x1.shape[0]
    tm = _pick_tile(t, TOKEN_TM)
    eidx_t, wts_t, rank_t, counts, base, x_slabs = _router(x1, prm["w_router_t"], prm["router_bias"], prm["ws1"],
                                                           prm["ws3"], prm["ws2"], tm)
    n_blocks = -(-t * EXPERT_TOPK // MOE_ROWS) + N_EXPERTS
    pad_start, last_block_row, blk_exp, n_used = _block_plan(counts.reshape(N_EXPERTS), n_blocks)
    pos_t = _positions(eidx_t, rank_t, pad_start, tm)
    x_rows = _scatter_rows(last_block_row, n_used, pos_t, x_slabs, n_blocks * MOE_ROWS, tm)
    y_rows = _experts(x_rows, blk_exp, n_used, prm["w1"], prm["w3"], prm["w2"])
    return _combine(pos_t, y_rows, wts_t.T, base, prm["ln2_g"], prm["ln2_b"], _pick_tile(t, COMBINE_TM))


def _mixer(x, pos0, conv_buf, s0, k_past, v_past, ik_past, prm):
    b, l, _ = x.shape
    lp = _round_up(l, DSA_Q_TILE)
    xp = x if lp == l else jnp.pad(x, ((0, 0), (0, lp - l), (0, 0)))
    h3 = _project(xp.reshape(b * lp, D_MODEL), prm["w_in"]).reshape(b, lp, H_COLS)

    ya, s_new, buf_new = _gdn(h3, l, conv_buf, s0, prm["conv_w"], prm["a_log"], prm["dt_bias"], prm["gdn_norm_g"])

    pos = pos0 + jnp.arange(lp, dtype=jnp.int32)
    qt, qit, wit, k_new, k_bf, ki_new, ki_bf, v_new, vt = _prep(h3, pos, prm["idx_k_ln_g"], prm["idx_k_ln_b"])
    if k_past is None:
        l_all = l
        k_all, ki_all, vt_all = k_bf, ki_bf, vt
    else:
        past = k_past.shape[1]
        l_all = past + l
        k_all = jnp.concatenate([k_past.reshape(b, past, ATT_KV).astype(BF16), k_bf[:, :l]], 1)
        ki_all = jnp.concatenate([ik_past.astype(BF16), ki_bf[:, :l]], 1)
        vt_all = jnp.concatenate([jnp.swapaxes(v_past.reshape(b, past, ATT_KV), 1, 2).astype(BF16), vt[:, :, :l]], 2)
    lk = _round_up(l_all, DSA_KEY_BLOCK)
    if lk != l_all:
        k_all = jnp.pad(k_all, ((0, 0), (0, lk - l_all), (0, 0)))
        ki_all = jnp.pad(ki_all, ((0, 0), (0, lk - l_all), (0, 0)))
        vt_all = jnp.pad(vt_all, ((0, 0), (0, 0), (0, lk - l_all)))
    yb = _dsa(qt, qit, wit, ki_all, k_all, vt_all, l_all, pos0)

    x1 = _merge(ya, yb, h3, x, prm["w_o_gdn"], prm["w_o_dsa"], prm["w_out"], prm["ln1_g"], prm["ln1_b"])
    state = (k_new[:, :l].reshape(b, l, ATT_KV_HEADS, HEAD_DIM), v_new[:, :l].reshape(b, l, ATT_KV_HEADS, HEAD_DIM),
             ki_new[:, :l], s_new, buf_new)
    return x1, state


def kernel(x_prompt, x_sample, cache_k, cache_v, cache_idx_k, state_gdn, state_conv, w_in, conv_w, a_log, dt_bias, gdn_norm_g, w_o_gdn, idx_k_ln_g, idx_k_ln_b, w_o_dsa, w_out, ln1_g, ln1_b, w_router, router_bias, w1, w3, w2, ws1, ws3, ws2, ln2_g, ln2_b):
    assert w_in.shape[0] == DEPTH == 1
    bp, lp_, _ = x_prompt.shape
    bs, ls_, _ = x_sample.shape
    past = cache_k.shape[2]
    prm = dict(
        w_in=_repack_w_in(w_in[0]).astype(BF16), conv_w=conv_w[0], a_log=a_log[0], dt_bias=dt_bias[0],
        gdn_norm_g=gdn_norm_g[0], w_o_gdn=w_o_gdn[0].astype(BF16), idx_k_ln_g=idx_k_ln_g[0],
        idx_k_ln_b=idx_k_ln_b[0], w_o_dsa=w_o_dsa[0].astype(BF16), w_out=w_out[0].astype(BF16),
        ln1_g=ln1_g[0], ln1_b=ln1_b[0], w_router_t=w_router[0].T, router_bias=router_bias[0],
        w1=w1[0], w3=w3[0], w2=w2[0], ws1=ws1[0].astype(BF16), ws3=ws3[0].astype(BF16),
        ws2=ws2[0].astype(BF16), ln2_g=ln2_g[0], ln2_b=ln2_b[0])
    conv0 = jnp.zeros((bp, CONV_W - 1, CONV_CH), F32)
    s0 = jnp.zeros((bp, GDN_HEADS, GDN_DK, GDN_DV), F32)
    x1p, sp = _mixer(x_prompt, 0, conv0, s0, None, None, None, prm)
    x1s, ss = _mixer(x_sample, past, state_conv[0], state_gdn[0], cache_k[0], cache_v[0], cache_idx_k[0], prm)
    tp, ts = bp * lp_, bs * ls_
    y = _moe(jnp.concatenate([x1p.reshape(tp, D_MODEL), x1s.reshape(ts, D_MODEL)], 0), prm)
    yp = y[:tp].reshape(bp, lp_, D_MODEL)
    ys = y[tp:].reshape(bs, ls_, D_MODEL)
    return (yp, ys) + tuple(a[None] for a in sp) + tuple(a[None] for a in ss)
```

```python
import functools

import jax
import jax.numpy as jnp
import numpy as np
from jax import lax
from jax.experimental import pallas as pl
from jax.experimental.pallas import tpu as pltpu

F32 = jnp.float32
BF16 = jnp.bfloat16

D_MODEL = 1024
CHUNK = 64
GDN_HEADS = 8
GDN_DK = 128
GDN_DV = 128
CONV_W = 4
ATT_HEADS = 8
ATT_KV_HEADS = 2
HEAD_DIM = 128
IDX_HEADS = 16
IDX_DIM = 64
TOPK_MAX = 256
ROPE_THETA = 500000.0
N_EXPERTS = 256
EXPERT_TOPK = 8
N_GROUPS = 8
TOPK_GROUPS = 4
EXPERT_DIM = 256
SHARED_DIM = 256
ROUTED_SCALE = 2.5
DEPTH = 1
DN_ALPHA = (2.0 * DEPTH) ** 0.25
LN_EPS = 1e-5
RMS_EPS = 1e-6

GDN_QK = GDN_HEADS * GDN_DK
GDN_V = GDN_HEADS * GDN_DV
CONV_CH = 2 * GDN_QK + GDN_V
ATT_Q = ATT_HEADS * HEAD_DIM
ATT_KV = ATT_KV_HEADS * HEAD_DIM
IDX_Q = IDX_HEADS * IDX_DIM
REP = ATT_HEADS // ATT_KV_HEADS

LANES = 128
SUBLANES = 8
VMEM_LIMIT = 56 * 1024 * 1024
DMA_PRIORITIES = 2

PROJ_TM = 1024
PROJ_TN = 1024
PREP_TM = 512
DSA_Q_TILE = 128
DSA_KEY_BLOCK = 512
TOKEN_TM = 256
COMBINE_TM = 128
GDN_BATCH_TILE = 4
MOE_ROWS = 256

COL_QKV = 0
COL_Z = COL_QKV + CONV_CH
COL_Q = COL_Z + GDN_V
COL_QI = COL_Q + ATT_Q
COL_GA = COL_QI + IDX_Q
COL_GB = COL_GA + D_MODEL
COL_K = COL_GB + D_MODEL
COL_V = COL_K + ATT_KV
COL_SMALL = COL_V + ATT_KV
SMALL_W = 512
SM_A = IDX_DIM
SM_B = SM_A + GDN_HEADS
SM_WI = SM_B + GDN_HEADS
H_COLS = COL_SMALL + SMALL_W


def _cparams(sem):
    return pltpu.CompilerParams(dimension_semantics=sem, vmem_limit_bytes=VMEM_LIMIT)


def _dot(a, b):
    return jnp.dot(a, b, preferred_element_type=F32)


def _dot_bf(a, b):
    return jnp.dot(a.astype(BF16), b.astype(BF16), preferred_element_type=F32)


def _dot_hi(a, b):
    return jnp.dot(a, b, precision=lax.Precision.HIGHEST, preferred_element_type=F32)


def _dot_nt_hi(a, b):
    return lax.dot_general(a, b, (((1,), (1,)), ((), ())), precision=lax.Precision.HIGHEST,
                           preferred_element_type=F32)


def _dot_nt_bf(a, b):
    return lax.dot_general(a.astype(BF16), b.astype(BF16), (((1,), (1,)), ((), ())),
                           preferred_element_type=F32)


def _dot_tn_bf(a, b):
    return lax.dot_general(a.astype(BF16), b.astype(BF16), (((0,), (0,)), ((), ())),
                           preferred_element_type=F32)


def _sigmoid(x):
    return 1.0 / (1.0 + jnp.exp(-x))


def _silu(x):
    return x * _sigmoid(x)


def _round_up(n, m):
    return -(-n // m) * m


def _pick_tile(n, pref):
    t = min(n, pref)
    assert n % t == 0
    return t


def _repack_w_in(w_in):
    sizes = (CONV_CH, GDN_V, GDN_HEADS, GDN_HEADS, ATT_Q, ATT_KV, ATT_KV, IDX_Q, IDX_DIM, IDX_HEADS,
             D_MODEL, D_MODEL)
    offs = np.concatenate([[0], np.cumsum(sizes)])
    (p_qkv, p_z, p_a, p_b, p_q, p_k, p_v, p_qi, p_ki, p_wi, p_ga, p_gb) = [
        w_in[:, offs[i]:offs[i + 1]] for i in range(len(sizes))]
    pad = jnp.zeros((w_in.shape[0], SMALL_W - (SM_WI + IDX_HEADS)), w_in.dtype)
    return jnp.concatenate([p_qkv, p_z, p_q, p_qi, p_ga, p_gb, p_k, p_v, p_ki, p_a, p_b, p_wi, pad], axis=1)


def _proj_kernel(x_ref, w_ref, o_ref):
    o_ref[...] = _dot(x_ref[...].astype(BF16), w_ref[...])


def _project(x2d, w_bf):
    t, d = x2d.shape
    n = w_bf.shape[1]
    tm, tn = _pick_tile(t, PROJ_TM), _pick_tile(n, PROJ_TN)
    return pl.pallas_call(
        _proj_kernel,
        grid=(t // tm, n // tn),
        in_specs=[pl.BlockSpec((tm, d), lambda i, j: (i, 0)),
                  pl.BlockSpec((d, tn), lambda i, j: (0, j))],
        out_specs=pl.BlockSpec((tm, tn), lambda i, j: (i, j)),
        out_shape=jax.ShapeDtypeStruct((t, n), F32),
        compiler_params=_cparams(("parallel", "arbitrary")),
        name="in_proj",
    )(x2d, w_bf)


TAIL_ROWS = SUBLANES
INV_BASE = 8


def _split(a):
    hi = a.astype(BF16)
    return hi, (a - hi.astype(F32)).astype(BF16)


def _dot3(a, b):
    return _dot(a[0], b[0]) + (_dot(a[0], b[1]) + _dot(a[1], b[0]))


def _unit_lower_inverse_steps(ms, c, out):
    ri = lax.broadcasted_iota(jnp.int32, (c, c), 0)
    ci = lax.broadcasted_iota(jnp.int32, (c, c), 1)
    eye = (ri == ci).astype(F32)
    blk = INV_BASE
    same = (ri // blk) == (ci // blk)
    ns = [jnp.where(same, -m, 0.0) for m in ms]
    xs = [eye + n for n in ns]
    span = 1
    while span * 2 < blk:
        nsp = [_split(n) for n in ns]
        ns = [_dot3(n, n) for n in nsp]
        yield
        nsp = [_split(n) for n in ns]
        xs = [x + _dot3(_split(x), n) for x, n in zip(xs, nsp)]
        yield
        span *= 2
    while blk < c:
        nxt = blk * 2
        emask = ((ri // nxt) == (ci // nxt)) & ((ri // blk) != (ci // blk))
        xsp = [_split(x) for x in xs]
        ts = [_dot3(x, _split(jnp.where(emask, m, 0.0))) for x, m in zip(xsp, ms)]
        yield
        xs = [x - _dot3(_split(t), xp) for x, t, xp in zip(xs, ts, xsp)]
        yield
        blk = nxt
    out.extend(xs)


def _gdn_kernel(hq_ref, z_ref, sm_ref, buf_ref, s0_ref, cw_ref, alog_ref, dtb_ref, ng_ref,
                y_ref, snew_ref, bufnew_ref, xp_ref, s_ref, *, chunk):
    c_idx = pl.program_id(1)
    n_c = pl.num_programs(1)
    C = chunk
    keep = CONV_W - 1

    @pl.when(c_idx == 0)
    def _():
        xp_ref[:, TAIL_ROWS - keep:TAIL_ROWS, :] = buf_ref[...]
        s_ref[...] = s0_ref[...]

    ri = lax.broadcasted_iota(jnp.int32, (C, C), 0)
    ci = lax.broadcasted_iota(jnp.int32, (C, C), 1)
    incl = ri >= ci
    strict = ri > ci

    heads = range(GDN_HEADS)

    def prepare(bi, d):
        xp_ref[bi, TAIL_ROWS:TAIL_ROWS + C, :] = hq_ref[bi]
        acc = cw_ref[0:1, :] * xp_ref[bi, TAIL_ROWS - keep:TAIL_ROWS - keep + C, :]
        for j in range(1, CONV_W):
            acc = acc + cw_ref[j:j + 1, :] * xp_ref[bi, TAIL_ROWS - keep + j:TAIL_ROWS - keep + j + C, :]
        conv = _silu(acc)
        xp_ref[bi, TAIL_ROWS - keep:TAIL_ROWS, :] = xp_ref[bi, TAIL_ROWS + C - keep:TAIL_ROWS + C, :]
        yield
        sm = sm_ref[bi, :, 0:LANES]
        xg = sm + dtb_ref[...]
        softplus = jnp.maximum(xg, 0.0) + jnp.log(1.0 + jnp.exp(-jnp.abs(xg)))
        g = -jnp.exp(alog_ref[...]) * softplus
        beta = _sigmoid(sm)
        gc = _dot_hi(incl.astype(F32), g)
        gc_t = gc.T
        yield
        for name in ("q", "k", "v", "gc", "bh", "eg", "decay"):
            d[name] = []
        for h in heads:
            q = conv[:, h * GDN_DK:(h + 1) * GDN_DK]
            k = conv[:, GDN_QK + h * GDN_DK:GDN_QK + (h + 1) * GDN_DK]
            d["q"].append(q * lax.rsqrt(jnp.sum(q * q, -1, keepdims=True) + 1e-6) * (GDN_DK ** -0.5))
            d["k"].append(k * lax.rsqrt(jnp.sum(k * k, -1, keepdims=True) + 1e-6))
            d["v"].append(conv[:, 2 * GDN_QK + h * GDN_DV:2 * GDN_QK + (h + 1) * GDN_DV])
            gch = gc[:, SM_A + h:SM_A + h + 1]
            d["gc"].append(gch)
            d["bh"].append(beta[:, SM_B + h:SM_B + h + 1])
            d["eg"].append(jnp.exp(gch))
            d["decay"].append(jnp.exp(jnp.where(incl, gch - gc_t[SM_A + h:SM_A + h + 1, :], -jnp.inf)))
            yield

    def recur(bi, d):
        qs, ks, vs, gcs, bhs, egs, decays = (d[n] for n in ("q", "k", "v", "gc", "bh", "eg", "decay"))
        kbs = [k * bh for k, bh in zip(ks, bhs)]
        ms = [jnp.where(strict, _dot_nt_bf(kb, k) * dc, 0.0) for kb, k, dc in zip(kbs, ks, decays)]
        yield
        attns = [_dot_nt_bf(q, k) * dc for q, k, dc in zip(qs, ks, decays)]
        yield
        tinvs = []
        yield from _unit_lower_inverse_steps(ms, C, tinvs)
        sols = [_dot_bf(t, jnp.concatenate([v * bh, kb * eg], axis=-1))
                for t, v, bh, kb, eg in zip(tinvs, vs, bhs, kbs, egs)]
        yield
        ss = [s_ref[bi, h] for h in heads]
        v_news = [sol[:, :GDN_DV] - _dot_bf(sol[:, GDN_DV:], s) for sol, s in zip(sols, ss)]
        yield
        os_ = [_dot_bf(q * eg, s) + _dot_bf(a, vn) for q, eg, s, a, vn in zip(qs, egs, ss, attns, v_news)]
        yield
        for h in heads:
            glast = gcs[h][C - 1:C, :]
            s_ref[bi, h] = ss[h] * jnp.exp(glast) + _dot_tn_bf(ks[h] * jnp.exp(glast - gcs[h]), v_news[h])
        yield
        for h in heads:
            o = os_[h]
            o = o * lax.rsqrt(jnp.mean(o * o, -1, keepdims=True) + RMS_EPS) * ng_ref[...]
            zh = z_ref[bi, :, h * GDN_DV:(h + 1) * GDN_DV]
            y_ref[bi, :, h * GDN_DV:(h + 1) * GDN_DV] = o * _silu(zh)

    n_seq = hq_ref.shape[0]
    data = [{} for _ in range(n_seq)]
    for _ in prepare(0, data[0]):
        pass
    for bi in range(n_seq):
        filler = prepare(bi + 1, data[bi + 1]) if bi + 1 < n_seq else iter(())
        for _ in recur(bi, data[bi]):
            next(filler, None)
        for _ in filler:
            pass

    @pl.when(c_idx == n_c - 1)
    def _():
        snew_ref[...] = s_ref[...]
        bufnew_ref[...] = xp_ref[:, TAIL_ROWS - keep:TAIL_ROWS, :]


def _gdn(h3, l, conv_buf, s0, conv_w, a_log, dt_bias, norm_g):
    b = h3.shape[0]
    chunk = min(CHUNK, l)
    assert l % chunk == 0 and chunk % SUBLANES == 0 and chunk >= CONV_W - 1
    kern = functools.partial(_gdn_kernel, chunk=chunk)
    keep = CONV_W - 1
    lane_row = lambda vec, at: jnp.zeros((1, LANES), F32).at[0, at:at + vec.shape[0]].set(vec)
    gb = _pick_tile(b, GDN_BATCH_TILE)
    return pl.pallas_call(
        kern,
        grid=(b // gb, l // chunk),
        in_specs=[
            pl.BlockSpec((gb, chunk, CONV_CH), lambda i, c: (i, c, COL_QKV // CONV_CH)),
            pl.BlockSpec((gb, chunk, GDN_V), lambda i, c: (i, c, COL_Z // GDN_V)),
            pl.BlockSpec((gb, chunk, SMALL_W), lambda i, c: (i, c, COL_SMALL // SMALL_W)),
            pl.BlockSpec((gb, keep, CONV_CH), lambda i, c: (i, 0, 0)),
            pl.BlockSpec((gb, GDN_HEADS, GDN_DK, GDN_DV), lambda i, c: (i, 0, 0, 0)),
            pl.BlockSpec((CONV_W, CONV_CH), lambda i, c: (0, 0)),
            pl.BlockSpec((1, LANES), lambda i, c: (0, 0)),
            pl.BlockSpec((1, LANES), lambda i, c: (0, 0)),
            pl.BlockSpec((1, GDN_DV), lambda i, c: (0, 0)),
        ],
        out_specs=[
            pl.BlockSpec((gb, chunk, GDN_V), lambda i, c: (i, c, 0)),
            pl.BlockSpec((gb, GDN_HEADS, GDN_DK, GDN_DV), lambda i, c: (i, 0, 0, 0)),
            pl.BlockSpec((gb, keep, CONV_CH), lambda i, c: (i, 0, 0)),
        ],
        out_shape=[
            jax.ShapeDtypeStruct((b, l, GDN_V), F32),
            jax.ShapeDtypeStruct((b, GDN_HEADS, GDN_DK, GDN_DV), F32),
            jax.ShapeDtypeStruct((b, keep, CONV_CH), F32),
        ],
        scratch_shapes=[
            pltpu.VMEM((gb, TAIL_ROWS + chunk, CONV_CH), F32),
            pltpu.VMEM((gb, GDN_HEADS, GDN_DK, GDN_DV), F32),
        ],
        compiler_params=_cparams(("parallel", "arbitrary")),
        name="gdn",
    )(h3, h3, h3, conv_buf, s0, conv_w, lane_row(a_log, SM_A), lane_row(dt_bias, SM_A),
      norm_g.reshape(1, GDN_DV))


def _rope_tables(pos, rot, period):
    half = rot // 2
    inv_freq = ROPE_THETA ** (-(2.0 / rot) * jnp.arange(half, dtype=F32))
    ang = pos.astype(F32)[:, None] * inv_freq[None, :]
    cos, sin = jnp.cos(ang), jnp.sin(ang)
    n = pos.shape[0]
    rest = period - rot
    c = jnp.concatenate([cos, cos, jnp.ones((n, rest), F32)], -1)
    sa = jnp.concatenate([-sin, jnp.zeros((n, half + rest), F32)], -1)
    sb = jnp.concatenate([jnp.zeros((n, half), F32), sin, jnp.zeros((n, rest), F32)], -1)
    reps = LANES // period
    return jnp.stack([jnp.tile(c, (1, reps)), jnp.tile(sa, (1, reps)), jnp.tile(sb, (1, reps))], 0)


Q_SCALE_LOG2 = HEAD_DIM ** -0.5 * float(np.log2(np.e))


def _rope128(x, tab_ref, half):
    return (x * tab_ref[0] + pltpu.roll(x, LANES - half, 1) * tab_ref[1]
            + pltpu.roll(x, half, 1) * tab_ref[2])


def _prep_kernel(q_ref, qi_ref, k_ref, v_ref, sm_ref, tq_ref, ti_ref, lng_ref, lnb_ref,
                 qt_ref, qit_ref, wit_ref, ko_ref, kb_ref, kio_ref, kib_ref, vo_ref, vt_ref):
    hq = HEAD_DIM // 8
    hi = IDX_DIM // 8
    for h in range(ATT_HEADS):
        x = _rope128(q_ref[:, h * HEAD_DIM:(h + 1) * HEAD_DIM], tq_ref, hq) * Q_SCALE_LOG2
        qt_ref[h * HEAD_DIM:(h + 1) * HEAD_DIM, :] = x.T.astype(BF16)
    for h in range(ATT_KV_HEADS):
        x = _rope128(k_ref[:, h * HEAD_DIM:(h + 1) * HEAD_DIM], tq_ref, hq)
        ko_ref[:, h * HEAD_DIM:(h + 1) * HEAD_DIM] = x
        kb_ref[:, h * HEAD_DIM:(h + 1) * HEAD_DIM] = x.astype(BF16)
        v = v_ref[:, h * HEAD_DIM:(h + 1) * HEAD_DIM]
        vo_ref[:, h * HEAD_DIM:(h + 1) * HEAD_DIM] = v
        vt_ref[h * HEAD_DIM:(h + 1) * HEAD_DIM, :] = v.T.astype(BF16)
    for c in range(IDX_Q // LANES):
        x = _rope128(qi_ref[:, c * LANES:(c + 1) * LANES], ti_ref, hi)
        qit_ref[c * LANES:(c + 1) * LANES, :] = x.T.astype(BF16)
    sm = sm_ref[:, 0:LANES]
    lane = lax.broadcasted_iota(jnp.int32, sm.shape, 1)
    is_ki = lane < IDX_DIM
    mu = jnp.sum(jnp.where(is_ki, sm, 0.0), -1, keepdims=True) * (1.0 / IDX_DIM)
    xc = jnp.where(is_ki, sm - mu, 0.0)
    var = jnp.sum(xc * xc, -1, keepdims=True) * (1.0 / IDX_DIM)
    ki = xc * lax.rsqrt(var + LN_EPS) * lng_ref[...] + lnb_ref[...]
    ki = _rope128(ki, ti_ref, hi)[:, 0:IDX_DIM]
    kio_ref[...] = ki
    kib_ref[...] = ki.astype(BF16)
    wit_ref[...] = sm.T[SM_WI:SM_WI + IDX_HEADS, :] * (IDX_HEADS ** -0.5 * IDX_DIM ** -0.5)


def _prep(h3, pos, ln_g, ln_b):
    b, l, _ = h3.shape
    tm = _pick_tile(l, PREP_TM)
    tab_q = _rope_tables(pos, HEAD_DIM // 4, HEAD_DIM)
    tab_i = _rope_tables(pos, IDX_DIM // 4, IDX_DIM)
    lng = jnp.concatenate([ln_g, jnp.zeros((LANES - IDX_DIM,), F32)]).reshape(1, LANES)
    lnb = jnp.concatenate([ln_b, jnp.zeros((LANES - IDX_DIM,), F32)]).reshape(1, LANES)
    rows = lambda w, col: pl.BlockSpec((None, tm, w), lambda i, t: (i, t, col))
    cols = lambda w: pl.BlockSpec((None, w, tm), lambda i, t: (i, 0, t))
    return pl.pallas_call(
        _prep_kernel,
        grid=(b, l // tm),
        in_specs=[
            rows(ATT_Q, COL_Q // ATT_Q), rows(IDX_Q, COL_QI // IDX_Q), rows(ATT_KV, COL_K // ATT_KV),
            rows(ATT_KV, COL_V // ATT_KV), rows(SMALL_W, COL_SMALL // SMALL_W),
            pl.BlockSpec((3, tm, LANES), lambda i, t: (0, t, 0)),
            pl.BlockSpec((3, tm, LANES), lambda i, t: (0, t, 0)),
            pl.BlockSpec((1, LANES), lambda i, t: (0, 0)),
            pl.BlockSpec((1, LANES), lambda i, t: (0, 0)),
        ],
        out_specs=[cols(ATT_Q), cols(IDX_Q), cols(IDX_HEADS), rows(ATT_KV, 0), rows(ATT_KV, 0),
                   rows(IDX_DIM, 0), rows(IDX_DIM, 0), rows(ATT_KV, 0), cols(ATT_KV)],
        out_shape=[
            jax.ShapeDtypeStruct((b, ATT_Q, l), BF16),
            jax.ShapeDtypeStruct((b, IDX_Q, l), BF16),
            jax.ShapeDtypeStruct((b, IDX_HEADS, l), F32),
            jax.ShapeDtypeStruct((b, l, ATT_KV), F32),
            jax.ShapeDtypeStruct((b, l, ATT_KV), BF16),
            jax.ShapeDtypeStruct((b, l, IDX_DIM), F32),
            jax.ShapeDtypeStruct((b, l, IDX_DIM), BF16),
            jax.ShapeDtypeStruct((b, l, ATT_KV), F32),
            jax.ShapeDtypeStruct((b, ATT_KV, l), BF16),
        ],
        compiler_params=_cparams(("parallel", "parallel")),
        name="dsa_prep",
    )(h3, h3, h3, h3, h3, tab_q, tab_i, lng, lnb)


INT_MIN = -2 ** 31
INT_MAX = 2 ** 31 - 1
NEG_INF_KEY = INT_MIN + 0x7FFFFF
NEG_BIG = -1e30
HEAD_PAIRS = IDX_HEADS // 2
COUNT_CHAINS = 8
ATTN_BLOCKS_PER_STEP = 4
SCORE_CLASSES = TOPK_MAX // 2
DENOM_ROWS = 16


def _dsa_kernel(qt_ref, qit_ref, wit_ref, ki_ref, k_ref, vt_ref, o_ref,
                key_ref, cls_ref, qs_ref, qip_ref, m_ref, acc_ref, cm_ref,
                *, tq, kb, l_true, pos0, topk, idx_bits):
    q0 = pos0 + pl.program_id(1) * tq
    qpos = q0 + lax.broadcasted_iota(jnp.int32, (1, tq), 1)
    lim = jnp.minimum((qpos // CHUNK + 1) * CHUNK, l_true)
    lim_max = jnp.minimum(((q0 + tq - 1) // CHUNK + 1) * CHUNK, l_true)
    nkb = (lim_max + kb - 1) // kb
    key_iota = lax.broadcasted_iota(jnp.int32, (kb, tq), 0)
    wit = wit_ref[...]

    for p in range(HEAD_PAIRS):
        for u in range(2):
            hh = 2 * p + u
            qip_ref[p, :, u * tq:(u + 1) * tq] = qit_ref[hh * IDX_DIM:(hh + 1) * IDX_DIM, :]
    for g in range(ATT_KV_HEADS):
        for r in range(REP):
            hh = g * REP + r
            qs_ref[g, :, r * tq:(r + 1) * tq] = qt_ref[hh * HEAD_DIM:(hh + 1) * HEAD_DIM, :]

    def to_key(score):
        bits = pltpu.bitcast(score, jnp.int32)
        return bits ^ ((bits >> 31) & INT_MAX)

    def score_blk(j, carry):
        off = pl.multiple_of(j * kb, kb)
        ki = ki_ref[pl.ds(off, kb), :]
        acc = jnp.zeros((kb, tq), F32)
        for p in range(HEAD_PAIRS):
            s2 = _dot(ki, qip_ref[p])
            acc = (acc + wit[2 * p:2 * p + 1, :] * jnp.maximum(s2[:, :tq], 0.0)
                   + wit[2 * p + 1:2 * p + 2, :] * jnp.maximum(s2[:, tq:], 0.0))
        score = jnp.where(off + key_iota < lim, acc, -jnp.inf)
        key_ref[pl.ds(off, kb), :] = to_key(score)
        top1, top2 = cls_ref[0], cls_ref[1]
        for s in range(kb // SCORE_CLASSES):
            x = score[s * SCORE_CLASSES:(s + 1) * SCORE_CLASSES, :]
            top2 = jnp.maximum(top2, jnp.minimum(top1, x))
            top1 = jnp.maximum(top1, x)
        cls_ref[0], cls_ref[1] = top1, top2
        return carry

    cls_ref[...] = jnp.full(cls_ref.shape, -jnp.inf, F32)
    lax.fori_loop(0, nkb, score_blk, 0)

    def count(pred_fn):
        def blk(j, c):
            off = pl.multiple_of(j * kb, kb)
            part = jnp.where(pred_fn(key_ref[pl.ds(off, kb), :], off), 1.0, 0.0)
            return c + jnp.sum(part.reshape(kb // (COUNT_CHAINS * SUBLANES), COUNT_CHAINS * SUBLANES, tq), axis=0)
        c = lax.fori_loop(0, nkb, blk, jnp.zeros((COUNT_CHAINS * SUBLANES, tq), F32))
        return jnp.sum(c, axis=0, keepdims=True)

    u_hi = to_key(jnp.max(cls_ref[0], axis=0, keepdims=True)) ^ INT_MIN
    u_lo = to_key(jnp.min(cls_ref[1], axis=0, keepdims=True)) ^ INT_MIN
    n_bits = 32 - jnp.min(lax.clz(u_hi ^ u_lo))
    prefix_mask = jnp.where(n_bits >= 32, 0, lax.shift_left(jnp.int32(-1), jnp.minimum(n_bits, 31)))

    def bit_step(t, cur):
        cand_u = cur | lax.shift_left(jnp.int32(1), n_bits - 1 - t)
        cand_s = cand_u ^ INT_MIN
        cnt = count(lambda kk, off: kk >= cand_s)
        return jnp.where(cnt >= topk, cand_u, cur)

    thr = lax.fori_loop(0, n_bits, bit_step, u_hi & prefix_mask) ^ INT_MIN
    n_ge = count(lambda kk, off: kk >= thr)

    cm_ref[...] = jnp.full((1, tq), INT_MAX, jnp.int32)

    @pl.when(jnp.max(jnp.where((n_ge > topk) & (thr > NEG_INF_KEY), 1, 0)) > 0)
    def _():
        need = topk - count(lambda kk, off: kk > thr)

        def idx_step(t, cm):
            cand = cm | lax.shift_left(jnp.int32(1), idx_bits - 1 - t)
            before = count(lambda kk, off: (kk == thr) & (off + key_iota < cand))
            return jnp.where(before < need, cand, cm)
        cm_ref[...] = lax.fori_loop(0, idx_bits, idx_step, jnp.zeros((1, tq), jnp.int32))

    cm = cm_ref[...]

    m_ref[...] = jnp.full(m_ref.shape, NEG_BIG, F32)
    acc_ref[...] = jnp.zeros(acc_ref.shape, F32)
    groups = range(ATT_KV_HEADS)
    ones_rows = jnp.ones((DENOM_ROWS, kb), BF16)

    def attn_blocks(js):
        offs = [pl.multiple_of(j * kb, kb) for j in js]
        biases = {}

        def logits_of(b, g):
            off = offs[b]
            if b not in biases:
                kk = key_ref[pl.ds(off, kb), :]
                kpos = off + key_iota
                sel = ((kk > thr) | ((kk == thr) & (kpos <= cm))) & (kpos < lim)
                biases[b] = jnp.concatenate([jnp.where(sel, 0.0, NEG_BIG)] * REP, axis=1)
            return _dot(k_ref[pl.ds(off, kb), g * HEAD_DIM:(g + 1) * HEAD_DIM], qs_ref[g]) + biases[b]

        def update(b, g, logits):
            off = offs[b]
            m_old = m_ref[g]
            m_new = jnp.maximum(m_old, jnp.max(logits, axis=0, keepdims=True))
            p = jnp.exp2(logits - m_new).astype(BF16)
            v_aug = jnp.concatenate([vt_ref[g * HEAD_DIM:(g + 1) * HEAD_DIM, pl.ds(off, kb)], ones_rows], axis=0)
            acc_ref[g] = jnp.exp2(m_old - m_new) * acc_ref[g] + _dot(v_aug, p)
            m_ref[g] = m_new

        steps = [(b, g) for b in range(len(js)) for g in groups]
        pending = logits_of(*steps[0])
        for n, (b, g) in enumerate(steps):
            nxt = logits_of(*steps[n + 1]) if n + 1 < len(steps) else None
            update(b, g, pending)
            pending = nxt

    def attn_group(jg, carry):
        attn_blocks([ATTN_BLOCKS_PER_STEP * jg + u for u in range(ATTN_BLOCKS_PER_STEP)])
        return carry

    def attn_single(j, carry):
        attn_blocks([j])
        return carry

    n_groups = nkb // ATTN_BLOCKS_PER_STEP
    lax.fori_loop(0, n_groups, attn_group, 0)
    lax.fori_loop(n_groups * ATTN_BLOCKS_PER_STEP, nkb, attn_single, 0)

    for g in groups:
        o_t = acc_ref[g, 0:HEAD_DIM, :] / acc_ref[g, HEAD_DIM:HEAD_DIM + 1, :]
        for r in range(REP):
            hh = g * REP + r
            o_ref[:, hh * HEAD_DIM:(hh + 1) * HEAD_DIM] = o_t[:, r * tq:(r + 1) * tq].T


def _dsa(qt, qit, wit, ki, k, vt, l_true, pos0):
    b, _, t = qt.shape
    lk = k.shape[1]
    tq, kb = DSA_Q_TILE, DSA_KEY_BLOCK
    assert t % tq == 0 and lk % kb == 0 and tq == HEAD_DIM and kb % SCORE_CLASSES == 0
    topk = min(TOPK_MAX, l_true // 4)
    kern = functools.partial(_dsa_kernel, tq=tq, kb=kb, l_true=l_true, pos0=pos0, topk=topk,
                             idx_bits=int(lk).bit_length())
    return pl.pallas_call(
        kern,
        grid=(b, t // tq),
        in_specs=[
            pl.BlockSpec((None, ATT_Q, tq), lambda i, t_: (i, 0, t_)),
            pl.BlockSpec((None, IDX_Q, tq), lambda i, t_: (i, 0, t_)),
            pl.BlockSpec((None, IDX_HEADS, tq), lambda i, t_: (i, 0, t_)),
            pl.BlockSpec((None, lk, IDX_DIM), lambda i, t_: (i, 0, 0)),
            pl.BlockSpec((None, lk, ATT_KV), lambda i, t_: (i, 0, 0)),
            pl.BlockSpec((None, ATT_KV, lk), lambda i, t_: (i, 0, 0)),
        ],
        out_specs=pl.BlockSpec((None, tq, ATT_Q), lambda i, t_: (i, t_, 0)),
        out_shape=jax.ShapeDtypeStruct((b, t, ATT_Q), F32),
        scratch_shapes=[
            pltpu.VMEM((lk, tq), jnp.int32),
            pltpu.VMEM((2, SCORE_CLASSES, tq), F32),
            pltpu.VMEM((ATT_KV_HEADS, HEAD_DIM, REP * tq), BF16),
            pltpu.VMEM((HEAD_PAIRS, IDX_DIM, 2 * tq), BF16),
            pltpu.VMEM((ATT_KV_HEADS, 1, REP * tq), F32),
            pltpu.VMEM((ATT_KV_HEADS, HEAD_DIM + DENOM_ROWS, REP * tq), F32),
            pltpu.VMEM((1, tq), jnp.int32),
        ],
        compiler_params=_cparams(("parallel", "arbitrary")),
        name="dsa",
    )(qt, qit, wit, ki, k, vt)


def _layer_norm(x, g, b):
    mu = jnp.mean(x, -1, keepdims=True)
    xc = x - mu
    var = jnp.mean(xc * xc, -1, keepdims=True)
    return xc * lax.rsqrt(var + LN_EPS) * g + b


def _merge_kernel(ya_ref, yb_ref, ga_ref, gb_ref, x_ref, wa_ref, wb_ref, wo_ref, g_ref, b_ref, o_ref):
    pa = _dot(ya_ref[...].astype(BF16), wa_ref[...])
    pb = _dot(yb_ref[...].astype(BF16), wb_ref[...])
    merged = _sigmoid(ga_ref[...]) * pa + _sigmoid(gb_ref[...]) * pb
    y = DN_ALPHA * x_ref[...] + _dot(merged.astype(BF16), wo_ref[...])
    o_ref[...] = _layer_norm(y, g_ref[...], b_ref[...])


def _merge(ya, yb, h3, x, wa, wb, wo, g, b):
    bsz, l, _ = x.shape
    tm = _pick_tile(l, TOKEN_TM)
    row = lambda c: pl.BlockSpec((None, tm, D_MODEL), lambda i, t: (i, t, c))
    full = lambda shp: pl.BlockSpec(shp, lambda i, t: (0, 0))
    return pl.pallas_call(
        _merge_kernel,
        grid=(bsz, l // tm),
        in_specs=[row(0), row(0), row(COL_GA // D_MODEL), row(COL_GB // D_MODEL), row(0),
                  full((GDN_V, D_MODEL)), full((ATT_Q, D_MODEL)), full((D_MODEL, D_MODEL)),
                  full((1, D_MODEL)), full((1, D_MODEL))],
        out_specs=row(0),
        out_shape=jax.ShapeDtypeStruct((bsz, l, D_MODEL), F32),
        compiler_params=_cparams(("parallel", "parallel")),
        name="merge_ln1",
    )(ya, yb, h3, h3, x, wa, wb, wo, g.reshape(1, D_MODEL), b.reshape(1, D_MODEL))


GROUP_SIZE = N_EXPERTS // N_GROUPS

SLAB_ROWS = D_MODEL // LANES


def _load_token_slabs(ref, n_tok, *lead):
    return jnp.concatenate([ref[lead + (pl.ds(c, n_tok, stride=SLAB_ROWS), slice(None))]
                            for c in range(SLAB_ROWS)], axis=1)


def _store_token_slabs(ref, x):
    for c in range(SLAB_ROWS):
        ref[pl.ds(c, x.shape[0], stride=SLAB_ROWS), :] = x[:, c * LANES:(c + 1) * LANES]


def _first_max(cur, rows, n_rows):
    m = jnp.max(cur, axis=0, keepdims=True)
    idx = jnp.min(jnp.where(cur == m, rows, n_rows), axis=0, keepdims=True)
    return m, idx


def _router_kernel(x_ref, wr_ref, rb_ref, ws1_ref, ws3_ref, ws2_ref,
                   eidx_ref, wts_ref, rank_ref, cnt_ref, base_ref, xslab_ref, run_ref):
    @pl.when(pl.program_id(0) == 0)
    def _():
        run_ref[...] = jnp.zeros(run_ref.shape, F32)

    x = x_ref[...]
    tm = x.shape[0]
    logits = _dot_nt_hi(wr_ref[...], x)
    scores = _sigmoid(logits)
    biased = scores + rb_ref[...]
    neg = -jnp.inf

    rows_g = lax.broadcasted_iota(jnp.int32, (GROUP_SIZE, tm), 0)
    gs = []
    for g in range(N_GROUPS):
        blk = biased[g * GROUP_SIZE:(g + 1) * GROUP_SIZE, :]
        m1, i1 = _first_max(blk, rows_g, GROUP_SIZE)
        m2 = jnp.max(jnp.where(rows_g == i1, neg, blk), axis=0, keepdims=True)
        gs.append(m1 + m2)
    cur = jnp.concatenate(gs, axis=0)
    rows_n = lax.broadcasted_iota(jnp.int32, (N_GROUPS, tm), 0)
    gsel = jnp.zeros((N_GROUPS, tm), F32)
    for _ in range(TOPK_GROUPS):
        _, ig = _first_max(cur, rows_n, N_GROUPS)
        hit = rows_n == ig
        gsel = jnp.where(hit, 1.0, gsel)
        cur = jnp.where(hit, neg, cur)
    gexp = jnp.concatenate([jnp.broadcast_to(gsel[g:g + 1, :], (GROUP_SIZE, tm)) for g in range(N_GROUPS)], axis=0)
    cur = jnp.where(gexp > 0.0, biased, neg)

    rows_e = lax.broadcasted_iota(jnp.int32, (N_EXPERTS, tm), 0)
    es, ws, hits = [], [], []
    for _ in range(EXPERT_TOPK):
        _, ie = _first_max(cur, rows_e, N_EXPERTS)
        hit = rows_e == ie
        es.append(ie)
        hits.append(hit)
        ws.append(jnp.sum(jnp.where(hit, scores, 0.0), axis=0, keepdims=True))
        cur = jnp.where(hit, neg, cur)
    w = jnp.concatenate(ws, axis=0)
    eidx_ref[...] = jnp.concatenate(es, axis=0)
    wts_ref[...] = w / jnp.sum(w, axis=0, keepdims=True) * ROUTED_SCALE

    assigned = jnp.zeros((N_EXPERTS, tm), F32)
    for hit in hits:
        assigned = jnp.where(hit, 1.0, assigned)
    earlier = (lax.broadcasted_iota(jnp.int32, (tm, tm), 0) < lax.broadcasted_iota(jnp.int32, (tm, tm), 1))
    before = run_ref[...] + _dot(assigned.astype(BF16), earlier.astype(BF16))
    rank_ref[...] = jnp.concatenate(
        [jnp.sum(jnp.where(hit, before, 0.0), axis=0, keepdims=True) for hit in hits], axis=0).astype(jnp.int32)
    run_ref[...] = run_ref[...] + jnp.sum(assigned, axis=1, keepdims=True)
    cnt_ref[...] = run_ref[...].astype(jnp.int32)

    xb = x.astype(BF16)
    hs = _silu(_dot(xb, ws1_ref[...])) * _dot(xb, ws3_ref[...])
    base_ref[...] = DN_ALPHA * x + _dot(hs.astype(BF16), ws2_ref[...])
    _store_token_slabs(xslab_ref, x)


def _router(x1, wr_t, rbias, ws1, ws3, ws2, tm):
    t = x1.shape[0]
    full = lambda shp: pl.BlockSpec(shp, lambda i: (0, 0))
    return pl.pallas_call(
        _router_kernel,
        grid=(t // tm,),
        in_specs=[pl.BlockSpec((tm, D_MODEL), lambda i: (i, 0)),
                  full((N_EXPERTS, D_MODEL)), full((N_EXPERTS, 1)),
                  full((D_MODEL, SHARED_DIM)), full((D_MODEL, SHARED_DIM)), full((SHARED_DIM, D_MODEL))],
        out_specs=[pl.BlockSpec((EXPERT_TOPK, tm), lambda i: (0, i)),
                   pl.BlockSpec((EXPERT_TOPK, tm), lambda i: (0, i)),
                   pl.BlockSpec((EXPERT_TOPK, tm), lambda i: (0, i)),
                   full((N_EXPERTS, 1)),
                   pl.BlockSpec((tm, D_MODEL), lambda i: (i, 0)),
                   pl.BlockSpec((tm * SLAB_ROWS, LANES), lambda i: (i, 0))],
        out_shape=[jax.ShapeDtypeStruct((EXPERT_TOPK, t), jnp.int32),
                   jax.ShapeDtypeStruct((EXPERT_TOPK, t), F32),
                   jax.ShapeDtypeStruct((EXPERT_TOPK, t), jnp.int32),
                   jax.ShapeDtypeStruct((N_EXPERTS, 1), jnp.int32),
                   jax.ShapeDtypeStruct((t, D_MODEL), F32),
                   jax.ShapeDtypeStruct((t * SLAB_ROWS, LANES), F32)],
        scratch_shapes=[pltpu.VMEM((N_EXPERTS, 1), F32)],
        compiler_params=_cparams(("arbitrary",)),
        name="router_shared",
    )(x1, wr_t, rbias.reshape(N_EXPERTS, 1), ws1, ws3, ws2)


def _block_plan(counts, n_blocks):
    padded = (counts + MOE_ROWS - 1) // MOE_ROWS * MOE_ROWS
    pad_end = jnp.cumsum(padded)
    blk_first = jnp.arange(n_blocks, dtype=jnp.int32) * MOE_ROWS
    blk_exp = jnp.minimum(jnp.sum((pad_end[None, :] <= blk_first[:, None]).astype(jnp.int32), axis=1), N_EXPERTS - 1)
    return pad_end - padded, jnp.maximum(pad_end - MOE_ROWS, 0), blk_exp, pad_end[-1:] // MOE_ROWS


def _pos_kernel(e_ref, r_ref, ps_ref, pos_ref):
    tm = e_ref.shape[1]
    rows_e = lax.broadcasted_iota(jnp.int32, (N_EXPERTS, tm), 0)
    first = [jnp.sum(jnp.where(rows_e == e_ref[j:j + 1, :], ps_ref[...], 0), axis=0, keepdims=True)
             for j in range(EXPERT_TOPK)]
    pos_ref[...] = jnp.concatenate(first, axis=0) + r_ref[...]


def _positions(eidx_t, rank_t, pad_start, tm):
    t = eidx_t.shape[1]
    blk = pl.BlockSpec((EXPERT_TOPK, tm), lambda i: (0, i))
    return pl.pallas_call(
        _pos_kernel,
        grid=(t // tm,),
        in_specs=[blk, blk, pl.BlockSpec((N_EXPERTS, 1), lambda i: (0, 0))],
        out_specs=blk,
        out_shape=jax.ShapeDtypeStruct((EXPERT_TOPK, t), jnp.int32),
        compiler_params=_cparams(("parallel",)),
        name="moe_positions",
    )(eidx_t, rank_t, pad_start.reshape(N_EXPERTS, 1))


def _scatter_kernel(last_ref, nu_ref, pos_ref, x_ref, rows_ref, zero_ref, sem, zsem):
    tm = x_ref.shape[0] // SLAB_ROWS
    block_slabs = MOE_ROWS * SLAB_ROWS
    n_blocks = rows_ref.shape[0] // block_slabs

    def slab(ref, row):
        return ref.at[pl.ds(pl.multiple_of(row * SLAB_ROWS, SLAB_ROWS), SLAB_ROWS), :]

    @pl.when(pl.program_id(0) == 0)
    def _():
        zero_ref[...] = jnp.zeros(zero_ref.shape, F32)

        def zero_copy(first_row):
            first = pl.multiple_of(first_row * SLAB_ROWS, block_slabs)
            return pltpu.make_async_copy(zero_ref, rows_ref.at[pl.ds(first, block_slabs), :], zsem)

        def each(fn):
            def expert_block(e, carry):
                fn(zero_copy(last_ref[e]))
                return carry

            def unused_block(blk, carry):
                fn(zero_copy(blk * MOE_ROWS))
                return carry

            lax.fori_loop(0, N_EXPERTS, expert_block, 0)
            lax.fori_loop(nu_ref[0], n_blocks, unused_block, 0)

        each(lambda copy: copy.start())
        each(lambda copy: copy.wait())

    def row_copy(t, j):
        return pltpu.make_async_copy(slab(x_ref, t), slab(rows_ref, pos_ref[j, t]), sem)

    def issue(t, carry):
        for j in range(EXPERT_TOPK):
            row_copy(t, j).start(priority=j % DMA_PRIORITIES)
        return carry

    lax.fori_loop(0, tm, issue, 0)
    for j in range(EXPERT_TOPK):
        pltpu.make_async_copy(x_ref, rows_ref.at[pl.ds(0, tm * SLAB_ROWS), :], sem).wait()


def _scatter_rows(last_block_row, n_used, pos_t, x_slabs, n_rows, tm):
    t = x_slabs.shape[0] // SLAB_ROWS
    grid_spec = pltpu.PrefetchScalarGridSpec(
        num_scalar_prefetch=2,
        grid=(t // tm,),
        in_specs=[pl.BlockSpec((EXPERT_TOPK, tm), lambda i, lr, nu: (0, i), memory_space=pltpu.SMEM),
                  pl.BlockSpec((tm * SLAB_ROWS, LANES), lambda i, lr, nu: (i, 0))],
        out_specs=pl.BlockSpec(memory_space=pl.ANY),
        scratch_shapes=[pltpu.VMEM((MOE_ROWS * SLAB_ROWS, LANES), F32), pltpu.SemaphoreType.DMA(()),
                        pltpu.SemaphoreType.DMA(())],
    )
    return pl.pallas_call(
        _scatter_kernel,
        grid_spec=grid_spec,
        out_shape=jax.ShapeDtypeStruct((n_rows * SLAB_ROWS, LANES), F32),
        compiler_params=_cparams(("arbitrary",)),
        name="moe_scatter",
    )(last_block_row, n_used, pos_t, x_slabs)


def _expert_kernel(be_ref, nu_ref, x_ref, w1_ref, w3_ref, w2_ref, o_ref, w1b_ref, w3b_ref, w2b_ref):
    i = pl.program_id(0)
    prev = be_ref[jnp.maximum(i - 1, 0)]

    @pl.when((i == 0) | (be_ref[i] != prev))
    def _():
        w1b_ref[...] = w1_ref[...].astype(BF16)
        w3b_ref[...] = w3_ref[...].astype(BF16)
        w2b_ref[...] = w2_ref[...].astype(BF16)

    @pl.when(i < nu_ref[0])
    def _():
        x = _load_token_slabs(x_ref, MOE_ROWS).astype(BF16)
        hmid = _silu(_dot(x, w1b_ref[...])) * _dot(x, w3b_ref[...])
        _store_token_slabs(o_ref, _dot(hmid.astype(BF16), w2b_ref[...]))

    @pl.when(i >= nu_ref[0])
    def _():
        o_ref[...] = jnp.zeros(o_ref.shape, F32)


def _experts(x_slabs, blk_exp, n_used, w1, w3, w2):
    block_slabs = MOE_ROWS * SLAB_ROWS
    n_blocks = x_slabs.shape[0] // block_slabs
    grid_spec = pltpu.PrefetchScalarGridSpec(
        num_scalar_prefetch=2,
        grid=(n_blocks,),
        in_specs=[
            pl.BlockSpec((block_slabs, LANES), lambda i, be, nu: (jnp.minimum(i, nu[0] - 1), 0)),
            pl.BlockSpec((None, D_MODEL, EXPERT_DIM), lambda i, be, nu: (be[i], 0, 0)),
            pl.BlockSpec((None, D_MODEL, EXPERT_DIM), lambda i, be, nu: (be[i], 0, 0)),
            pl.BlockSpec((None, EXPERT_DIM, D_MODEL), lambda i, be, nu: (be[i], 0, 0)),
        ],
        out_specs=pl.BlockSpec((block_slabs, LANES), lambda i, be, nu: (i, 0)),
        scratch_shapes=[pltpu.VMEM((D_MODEL, EXPERT_DIM), BF16), pltpu.VMEM((D_MODEL, EXPERT_DIM), BF16),
                        pltpu.VMEM((EXPERT_DIM, D_MODEL), BF16)],
    )
    return pl.pallas_call(
        _expert_kernel,
        grid_spec=grid_spec,
        out_shape=jax.ShapeDtypeStruct(x_slabs.shape, F32),
        compiler_params=_cparams(("arbitrary",)),
        name="experts",
    )(blk_exp, n_used, x_slabs, w1, w3, w2)


def _combine_kernel(pos_ref, pos_next_ref, y_ref, w_ref, base_ref, g_ref, b_ref, o_ref, gbuf_ref, sem):
    tm = base_ref.shape[0]
    i = pl.program_id(0)
    slot = lax.rem(i, 2)

    def slab(row):
        return pl.ds(pl.multiple_of(row * SLAB_ROWS, SLAB_ROWS), SLAB_ROWS)

    def issue_tile(p_ref, s):
        def issue(t, carry):
            for j in range(EXPERT_TOPK):
                pltpu.make_async_copy(y_ref.at[slab(p_ref[j, t]), :], gbuf_ref.at[s, j, slab(t), :],
                                      sem.at[s]).start(priority=j % DMA_PRIORITIES)
            return carry
        lax.fori_loop(0, tm, issue, 0)

    @pl.when(i == 0)
    def _():
        issue_tile(pos_ref, 0)

    @pl.when(i + 1 < pl.num_programs(0))
    def _():
        issue_tile(pos_next_ref, 1 - slot)

    for j in range(EXPERT_TOPK):
        pltpu.make_async_copy(y_ref.at[pl.ds(0, tm * SLAB_ROWS), :], gbuf_ref.at[slot, j], sem.at[slot]).wait()

    acc = base_ref[...]
    w = w_ref[...]
    for j in range(EXPERT_TOPK):
        acc = acc + w[:, j:j + 1] * _load_token_slabs(gbuf_ref, tm, slot, j)
    o_ref[...] = _layer_norm(acc, g_ref[...], b_ref[...])


def _combine(pos_t, y_rows, wts, base, g, b, tm):
    t = base.shape[0]
    n_tiles = t // tm
    return pl.pallas_call(
        _combine_kernel,
        grid=(n_tiles,),
        in_specs=[pl.BlockSpec((EXPERT_TOPK, tm), lambda i: (0, i), memory_space=pltpu.SMEM),
                  pl.BlockSpec((EXPERT_TOPK, tm), lambda i: (0, jnp.minimum(i + 1, n_tiles - 1)),
                               memory_space=pltpu.SMEM),
                  pl.BlockSpec(memory_space=pl.ANY),
                  pl.BlockSpec((tm, EXPERT_TOPK), lambda i: (i, 0)),
                  pl.BlockSpec((tm, D_MODEL), lambda i: (i, 0)),
                  pl.BlockSpec((1, D_MODEL), lambda i: (0, 0)),
                  pl.BlockSpec((1, D_MODEL), lambda i: (0, 0))],
        out_specs=pl.BlockSpec((tm, D_MODEL), lambda i: (i, 0)),
        out_shape=jax.ShapeDtypeStruct((t, D_MODEL), F32),
        scratch_shapes=[pltpu.VMEM((2, EXPERT_TOPK, tm * SLAB_ROWS, LANES), F32), pltpu.SemaphoreType.DMA((2,))],
        compiler_params=_cparams(("arbitrary",)),
        name="combine_ln2",
    )(pos_t, pos_t, y_rows, wts, base, g.reshape(1, D_MODEL), b.reshape(1, D_MODEL))


def _moe(x1, prm):
    t = x1.shape[0]
    tm = _pick_tile(t, TOKEN_TM)
    eidx_t, wts_t, rank_t, counts, base, x_slabs = _router(x1, prm["w_router_t"], prm["router_bias"], prm["ws1"],
                                                           prm["ws3"], prm["ws2"], tm)
    n_blocks = -(-t * EXPERT_TOPK // MOE_ROWS) + N_EXPERTS
    pad_start, last_block_row, blk_exp, n_used = _block_plan(counts.reshape(N_EXPERTS), n_blocks)
    pos_t = _positions(eidx_t, rank_t, pad_start, tm)
    x_rows = _scatter_rows(last_block_row, n_used, pos_t, x_slabs, n_blocks * MOE_ROWS, tm)
    y_rows = _experts(x_rows, blk_exp, n_used, prm["w1"], prm["w3"], prm["w2"])
    return _combine(pos_t, y_rows, wts_t.T, base, prm["ln2_g"], prm["ln2_b"], _pick_tile(t, COMBINE_TM))


def _mixer(x, pos0, conv_buf, s0, k_past, v_past, ik_past, prm):
    b, l, _ = x.shape
    lp = _round_up(l, DSA_Q_TILE)
    xp = x if lp == l else jnp.pad(x, ((0, 0), (0, lp - l), (0, 0)))
    h3 = _project(xp.reshape(b * lp, D_MODEL), prm["w_in"]).reshape(b, lp, H_COLS)

    ya, s_new, buf_new = _gdn(h3, l, conv_buf, s0, prm["conv_w"], prm["a_log"], prm["dt_bias"], prm["gdn_norm_g"])

    pos = pos0 + jnp.arange(lp, dtype=jnp.int32)
    qt, qit, wit, k_new, k_bf, ki_new, ki_bf, v_new, vt = _prep(h3, pos, prm["idx_k_ln_g"], prm["idx_k_ln_b"])
    if k_past is None:
        l_all = l
        k_all, ki_all, vt_all = k_bf, ki_bf, vt
    else:
        past = k_past.shape[1]
        l_all = past + l
        k_all = jnp.concatenate([k_past.reshape(b, past, ATT_KV).astype(BF16), k_bf[:, :l]], 1)
        ki_all = jnp.concatenate([ik_past.astype(BF16), ki_bf[:, :l]], 1)
        vt_all = jnp.concatenate([jnp.swapaxes(v_past.reshape(b, past, ATT_KV), 1, 2).astype(BF16), vt[:, :, :l]], 2)
    lk = _round_up(l_all, DSA_KEY_BLOCK)
    if lk != l_all:
        k_all = jnp.pad(k_all, ((0, 0), (0, lk - l_all), (0, 0)))
        ki_all = jnp.pad(ki_all, ((0, 0), (0, lk - l_all), (0, 0)))
        vt_all = jnp.pad(vt_all, ((0, 0), (0, 0), (0, lk - l_all)))
    yb = _dsa(qt, qit, wit, ki_all, k_all, vt_all, l_all, pos0)

    x1 = _merge(ya, yb, h3, x, prm["w_o_gdn"], prm["w_o_dsa"], prm["w_out"], prm["ln1_g"], prm["ln1_b"])
    state = (k_new[:, :l].reshape(b, l, ATT_KV_HEADS, HEAD_DIM), v_new[:, :l].reshape(b, l, ATT_KV_HEADS, HEAD_DIM),
             ki_new[:, :l], s_new, buf_new)
    return x1, state


def kernel(x_prompt, x_sample, cache_k, cache_v, cache_idx_k, state_gdn, state_conv, w_in, conv_w, a_log, dt_bias, gdn_norm_g, w_o_gdn, idx_k_ln_g, idx_k_ln_b, w_o_dsa, w_out, ln1_g, ln1_b, w_router, router_bias, w1, w3, w2, ws1, ws3, ws2, ln2_g, ln2_b):
    assert w_in.shape[0] == DEPTH == 1
    bp, lp_, _ = x_prompt.shape
    bs, ls_, _ = x_sample.shape
    past = cache_k.shape[2]
    prm = dict(
        w_in=_repack_w_in(w_in[0]).astype(BF16), conv_w=conv_w[0], a_log=a_log[0], dt_bias=dt_bias[0],
        gdn_norm_g=gdn_norm_g[0], w_o_gdn=w_o_gdn[0].astype(BF16), idx_k_ln_g=idx_k_ln_g[0],
        idx_k_ln_b=idx_k_ln_b[0], w_o_dsa=w_o_dsa[0].astype(BF16), w_out=w_out[0].astype(BF16),
        ln1_g=ln1_g[0], ln1_b=ln1_b[0], w_router_t=w_router[0].T, router_bias=router_bias[0],
        w1=w1[0], w3=w3[0], w2=w2[0], ws1=ws1[0].astype(BF16), ws3=ws3[0].astype(BF16),
        ws2=ws2[0].astype(BF16), ln2_g=ln2_g[0], ln2_b=ln2_b[0])
    conv0 = jnp.zeros((bp, CONV_W - 1, CONV_CH), F32)
    s0 = jnp.zeros((bp, GDN_HEADS, GDN_DK, GDN_DV), F32)
    x1p, sp = _mixer(x_prompt, 0, conv0, s0, None, None, None, prm)
    x1s, ss = _mixer(x_sample, past, state_conv[0], state_gdn[0], cache_k[0], cache_v[0], cache_idx_k[0], prm)
    tp, ts = bp * lp_, bs * ls_
    y = _moe(jnp.concatenate([x1p.reshape(tp, D_MODEL), x1s.reshape(ts, D_MODEL)], 0), prm)
    yp = y[:tp].reshape(bp, lp_, D_MODEL)
    ys = y[tp:].reshape(bs, ls_, D_MODEL)
    return (yp, ys) + tuple(a[None] for a in sp) + tuple(a[None] for a in ss)
```
